```python
import jax, jax.numpy as jnp
from jax import lax
import numpy as np

D_MODEL = 2048
BATCH = 4
SEQ = 2048
DEPTH = 2

CHUNK = 64
D_RWKV = D_MODEL // 2
D_HGRN = D_MODEL - D_RWKV
RWKV_HEAD = 64
RWKV_HEADS = D_RWKV // RWKV_HEAD
DECAY_RANK = max(32, int(round(1.8 * D_RWKV ** 0.5 / 32)) * 32)
A_RANK = max(32, int(round(1.8 * D_RWKV ** 0.5 / 32)) * 32)
VRES_RANK = max(32, int(round(1.3 * D_RWKV ** 0.5 / 32)) * 32)
HGRN_EXPAND = 128
HGRN_HEADS = D_HGRN // HGRN_EXPAND
HGRN_HEAD_V = D_HGRN // HGRN_HEADS
RWKV_COLS = 4 * D_RWKV + DECAY_RANK + A_RANK
HGRN_COLS = 4 * D_HGRN
IN_COLS = RWKV_COLS + HGRN_COLS
ALPHA = (2 * DEPTH) ** 0.25
BETA = (8 * DEPTH) ** -0.25
LN_EPS = 1e-5
GN_EPS = 64e-5
RMS_EPS = 1e-5
LB_FLOOR = 1e-30

kernel_name = "rwkv7_hgrn2_parallel_deepnorm"


def _heads(t, n):
    return t.reshape(t.shape[:-1] + (-1, n))


def _layer_norm(x, w, b):
    x = x.astype(jnp.float32)
    mu = jnp.mean(x, -1, keepdims=True)
    var = jnp.mean(jnp.square(x - mu), -1, keepdims=True)
    return (x - mu) * lax.rsqrt(var + LN_EPS) * w + b


def _token_shift(y, mu):
    y_prev = jnp.pad(y, ((0, 0), (1, 0), (0, 0)))[:, :-1]
    return y + mu * (y_prev - y)


def _rwkv7_scan(r, w, k, v, a_vec, b_vec):
    Bsz, T, H, N = r.shape

    def step(S, inp):
        r_t, w_t, k_t, v_t, a_t, b_t = inp
        sa = jnp.einsum('bhij,bhj->bhi', S, a_t)
        S = S * w_t[:, :, None, :] + sa[..., None] * b_t[:, :, None, :] + v_t[..., None] * k_t[:, :, None, :]
        return S, jnp.einsum('bhij,bhj->bhi', S, r_t)

    S0 = jnp.zeros((Bsz, H, N, N), jnp.float32)
    xs = tuple(jnp.moveaxis(t, 1, 0) for t in (r, w, k, v, a_vec, b_vec))
    _, o = lax.scan(step, S0, xs)
    return jnp.moveaxis(o, 0, 1)


def _rwkv7_branch(rw, v_first, w0, w_up, a0, a_up, k_k, k_a, r_k, gn_w, gn_b, v_mix):
    r, k, v, z, wd, ad = jnp.split(
        rw, [D_RWKV, 2 * D_RWKV, 3 * D_RWKV, 4 * D_RWKV, 4 * D_RWKV + DECAY_RANK], axis=-1)
    w_raw = w0 + jnp.tanh(wd) @ w_up
    decay = jnp.exp(-jnp.exp(-jax.nn.softplus(-w_raw) - 0.5))
    a = jax.nn.sigmoid(a0 + ad @ a_up)
    if v_mix is None:
        v_first = v
    else:
        v0, v_down, v_up = v_mix
        v = v + (v_first - v) * jax.nn.sigmoid(v0 + (v @ v_down) @ v_up)
    kk = _heads(k * k_k, RWKV_HEAD)
    kk = kk / jnp.maximum(jnp.sqrt(jnp.sum(kk * kk, -1, keepdims=True)), 1e-12)
    k = k * (1.0 + (a - 1.0) * k_a)
    rh, kh, vh = _heads(r, RWKV_HEAD), _heads(k, RWKV_HEAD), _heads(v, RWKV_HEAD)
    ah = _heads(a, RWKV_HEAD)
    o = _rwkv7_scan(rh, _heads(decay, RWKV_HEAD), kh, vh, -kk, kk * ah)
    mu = jnp.mean(o, -1, keepdims=True)
    var = jnp.mean(jnp.square(o - mu), -1, keepdims=True)
    o = (o - mu) * lax.rsqrt(var + GN_EPS) * _heads(gn_w, RWKV_HEAD) + _heads(gn_b, RWKV_HEAD)
    o = o + jnp.sum(rh * kh * _heads(r_k, RWKV_HEAD), -1, keepdims=True) * vh
    return o.reshape(rw.shape[:-1] + (D_RWKV,)) * jax.nn.silu(z), v_first


def _hgrn2_chunkwise(q, log_f, k, i):
    Bsz, T, H, DK = q.shape
    DV = i.shape[-1]
    NC = T // CHUNK

    def to_chunks(t):
        return jnp.moveaxis(t.reshape(Bsz, NC, CHUNK, H, t.shape[-1]), 1, 0)

    causal = jnp.tril(jnp.ones((CHUNK, CHUNK), bool))[None, :, :, None, None]

    def step(S, inp):
        q_c, lf_c, k_c, i_c = inp
        b = jnp.cumsum(lf_c, axis=1)
        diff = b[:, :, None] - b[:, None, :]
        decay = jnp.where(causal, jnp.exp(jnp.where(causal, diff, 0.0)), 0.0)
        att = jnp.einsum('btshd,bshd->btsh', q_c[:, :, None] * decay, k_c)
        o_intra = jnp.einsum('btsh,bshv->bthv', att, i_c)
        o_inter = jnp.einsum('bthd,bhdv->bthv', q_c * jnp.exp(b), S)
        b_last = b[:, -1]
        k_dec = k_c * jnp.exp(b_last[:, None] - b)
        S = S * jnp.exp(b_last)[..., None] + jnp.einsum('bshd,bshv->bhdv', k_dec, i_c)
        return S, o_intra + o_inter

    S0 = jnp.zeros((Bsz, H, DK, DV), jnp.float32)
    _, o = lax.scan(step, S0, (to_chunks(q), to_chunks(log_f), to_chunks(k), to_chunks(i)))
    return jnp.moveaxis(o, 0, 1).reshape(Bsz, T, H, DV)


def _hgrn2_branch(hg, lb, g_norm_w):
    q, f_raw, i_in, z = jnp.split(hg, 4, axis=-1)
    q = jax.nn.silu(q)
    log_lb = jnp.log(jnp.maximum(lb, LB_FLOOR))
    log_f = jnp.logaddexp(log_lb, jnp.log1p(-lb) + jax.nn.log_sigmoid(f_raw))
    k = (1.0 - lb) * jax.nn.sigmoid(-f_raw)
    o = _hgrn2_chunkwise(_heads(q, HGRN_EXPAND), _heads(log_f, HGRN_EXPAND),
                         _heads(k, HGRN_EXPAND), _heads(i_in, HGRN_HEAD_V))
    o = o * lax.rsqrt(jnp.mean(o * o, -1, keepdims=True) + RMS_EPS)
    return o.reshape(hg.shape[:-1] + (D_HGRN,)) * g_norm_w * jax.nn.silu(z)


def setup_inputs(seed: int = 0) -> dict:
    key = jax.random.key(seed)
    ks = jax.random.split(key, 24)
    L, L1 = DEPTH, DEPTH - 1

    def nrm(k, shape, s):
        return s * jax.random.normal(k, shape, jnp.float32)

    x = nrm(ks[0], (BATCH, SEQ, D_MODEL), 1.0)
    col_scale = jnp.concatenate([
        jnp.ones((2 * D_RWKV,), jnp.float32), jnp.full((D_RWKV,), BETA, jnp.float32),
        jnp.ones((D_RWKV + DECAY_RANK + A_RANK,), jnp.float32),
        jnp.ones((2 * D_HGRN,), jnp.float32), jnp.full((D_HGRN,), BETA, jnp.float32),
        jnp.ones((D_HGRN,), jnp.float32)])
    w_in = nrm(ks[1], (L, D_MODEL, IN_COLS), D_MODEL ** -0.5) * col_scale
    shift_mu = jax.random.uniform(ks[2], (L, RWKV_COLS), jnp.float32)
    ramp = (jnp.arange(D_RWKV, dtype=jnp.float32) / (D_RWKV - 1)) ** 0.85
    w_decay0 = -6.0 + 5.0 * ramp + nrm(ks[3], (L, D_RWKV), 0.1)
    w_decay_up = nrm(ks[4], (L, DECAY_RANK, D_RWKV), 0.3 * DECAY_RANK ** -0.5)
    a0 = nrm(ks[5], (L, D_RWKV), 0.1)
    a_up = nrm(ks[6], (L, A_RANK, D_RWKV), 0.3 * A_RANK ** -0.5)
    k_k = 0.85 + nrm(ks[7], (L, D_RWKV), 0.05)
    k_a = 1.0 + nrm(ks[8], (L, D_RWKV), 0.05)
    r_k = nrm(ks[9], (L, D_RWKV), 0.1)
    ln_x_w = 1.0 + nrm(ks[10], (L, D_RWKV), 0.05)
    ln_x_b = nrm(ks[11], (L, D_RWKV), 0.02)
    v_mix0 = 1.0 + nrm(ks[12], (L1, D_RWKV), 0.1)
    v_mix_down = nrm(ks[13], (L1, D_RWKV, VRES_RANK), D_RWKV ** -0.5)
    v_mix_up = nrm(ks[14], (L1, VRES_RANK, D_RWKV), 0.3 * VRES_RANK ** -0.5)
    lb_logits = nrm(ks[15], (L, D_HGRN), 0.5)
    g_norm_w = 1.0 + nrm(ks[16], (L, D_HGRN), 0.05)
    w_out = nrm(ks[17], (L, D_MODEL, D_MODEL), BETA * D_MODEL ** -0.5)
    ln_w = 1.0 + nrm(ks[18], (L, D_MODEL), 0.05)
    ln_b = nrm(ks[19], (L, D_MODEL), 0.02)
    return {"x": x, "w_in": w_in, "shift_mu": shift_mu, "w_decay0": w_decay0,
            "w_decay_up": w_decay_up, "a0": a0, "a_up": a_up, "k_k": k_k, "k_a": k_a,
            "r_k": r_k, "ln_x_w": ln_x_w, "ln_x_b": ln_x_b, "v_mix0": v_mix0,
            "v_mix_down": v_mix_down, "v_mix_up": v_mix_up, "lb_logits": lb_logits,
            "g_norm_w": g_norm_w, "w_out": w_out, "ln_w": ln_w, "ln_b": ln_b}


def reference(x, w_in, shift_mu, w_decay0, w_decay_up, a0, a_up, k_k, k_a, r_k, ln_x_w, ln_x_b,
              v_mix0, v_mix_down, v_mix_up, lb_logits, g_norm_w, w_out, ln_w, ln_b):
    out_dtype = x.dtype
    lb_sm = jax.nn.softmax(lb_logits.astype(jnp.float32), axis=0)
    lower_bounds = jnp.cumsum(lb_sm, axis=0) - lb_sm[0]
    h = x.astype(jnp.float32)
    v_first = None
    for l in range(DEPTH):
        proj = jnp.einsum('btd,dc->btc', h, w_in[l].astype(jnp.float32))
        rw = _token_shift(proj[..., :RWKV_COLS], shift_mu[l])
        hg = proj[..., RWKV_COLS:]
        v_mix = None if l == 0 else (v_mix0[l - 1], v_mix_down[l - 1], v_mix_up[l - 1])
        o_rwkv, v_first = _rwkv7_branch(rw, v_first, w_decay0[l], w_decay_up[l], a0[l], a_up[l],
                                        k_k[l], k_a[l], r_k[l], ln_x_w[l], ln_x_b[l], v_mix)
        o_hgrn = _hgrn2_branch(hg, lower_bounds[l], g_norm_w[l])
        y = jnp.einsum('btc,cd->btd', jnp.concatenate([o_rwkv, o_hgrn], axis=-1), w_out[l])
        h = _layer_norm(ALPHA * h + y, ln_w[l], ln_b[l])
    return h.astype(out_dtype)
```

```python
import functools
import math

import jax
import jax.numpy as jnp
from jax import lax
from jax.experimental import pallas as pl
from jax.experimental.pallas import tpu as pltpu

D_MODEL = 2048
DEPTH = 2
D_RWKV = D_MODEL // 2
D_HGRN = D_MODEL - D_RWKV
RWKV_HEAD = 64
DECAY_RANK = 64
A_RANK = 64
VRES_RANK = 32
HGRN_EXPAND = 128
HGRN_HEADS = D_HGRN // HGRN_EXPAND
RWKV_COLS = 4 * D_RWKV + DECAY_RANK + A_RANK
HGRN_COLS = 4 * D_HGRN
ALPHA = (2 * DEPTH) ** 0.25
LN_EPS = 1e-5
GN_EPS = 64e-5
RMS_EPS = 1e-5
LB_FLOOR = 1e-30

LANES = 128
CHUNK = 64
PAIR_ROWS = 2 * CHUNK
RWKV_PAIRS = D_RWKV // LANES
VMEM_LIMIT = 56 * 1024 * 1024

HI = lax.Precision.HIGHEST
F32 = jnp.float32


def _dot(a, b, precision=HI):
    return jnp.dot(a, b, precision=precision, preferred_element_type=F32)


def _dot_nt(a, b, precision=HI):
    return lax.dot_general(a, b, (((1,), (1,)), ((), ())), precision=precision,
                           preferred_element_type=F32)


def _dot_tn(a, b, precision=HI):
    return lax.dot_general(a, b, (((0,), (0,)), ((), ())), precision=precision,
                           preferred_element_type=F32)


def _sigmoid(x):
    return 1.0 / (1.0 + jnp.exp(-x))


def _silu(x):
    return x * _sigmoid(x)


def _iota2(shape, dim):
    return lax.broadcasted_iota(jnp.int32, shape, dim)


def _mm_kernel(x_ref, w_ref, o_ref):
    o_ref[...] = jnp.dot(x_ref[...].astype(jnp.bfloat16), w_ref[...],
                         preferred_element_type=F32)


def _matmul(x, w, tm, tn):
    m, k = x.shape
    n = w.shape[1]
    return pl.pallas_call(
        _mm_kernel,
        out_shape=jax.ShapeDtypeStruct((m, n), F32),
        grid=(n // tn, m // tm),
        in_specs=[pl.BlockSpec((tm, k), lambda j, i: (i, 0)),
                  pl.BlockSpec((k, tn), lambda j, i: (0, j))],
        out_specs=pl.BlockSpec((tm, tn), lambda j, i: (i, j)),
        compiler_params=pltpu.CompilerParams(
            dimension_semantics=("arbitrary", "arbitrary"), vmem_limit_bytes=VMEM_LIMIT),
        name="in_proj",
    )(x, w)


def _out_kernel(or_ref, oh_ref, h_ref, w_ref, lnw_ref, lnb_ref, o_ref):
    y = jnp.dot(or_ref[...].astype(jnp.bfloat16), w_ref[:D_RWKV, :], preferred_element_type=F32)
    y = y + jnp.dot(oh_ref[...].astype(jnp.bfloat16), w_ref[D_RWKV:, :],
                    preferred_element_type=F32)
    u = ALPHA * h_ref[...] + y
    mu = jnp.mean(u, axis=-1, keepdims=True)
    d = u - mu
    var = jnp.mean(d * d, axis=-1, keepdims=True)
    o_ref[...] = d * lax.rsqrt(var + LN_EPS) * lnw_ref[...] + lnb_ref[...]


def _out_proj_ln(o_r, o_h, h, w, lnw, lnb, tm):
    m = h.shape[0]
    return pl.pallas_call(
        _out_kernel,
        out_shape=jax.ShapeDtypeStruct((m, D_MODEL), F32),
        grid=(m // tm,),
        in_specs=[pl.BlockSpec((tm, D_RWKV), lambda i: (i, 0)),
                  pl.BlockSpec((tm, D_HGRN), lambda i: (i, 0)),
                  pl.BlockSpec((tm, D_MODEL), lambda i: (i, 0)),
                  pl.BlockSpec((D_MODEL, D_MODEL), lambda i: (0, 0)),
                  pl.BlockSpec((1, D_MODEL), lambda i: (0, 0)),
                  pl.BlockSpec((1, D_MODEL), lambda i: (0, 0))],
        out_specs=pl.BlockSpec((tm, D_MODEL), lambda i: (i, 0)),
        compiler_params=pltpu.CompilerParams(
            dimension_semantics=("arbitrary",), vmem_limit_bytes=VMEM_LIMIT),
        name="out_proj_ln",
    )(o_r, o_h, h, w, lnw, lnb)


def _rwkv_kernel(has_vmix, *refs):
    if has_vmix:
        (rw_ref, mu_ref, w0_ref, wup_ref, a0_ref, aup_ref, kk_ref, ka_ref, rk_ref, gnw_ref,
         gnb_ref, vf_ref, v0_ref, vdn_ref, vup_ref, o_ref, s_ref, prev_ref) = refs
    else:
        (rw_ref, mu_ref, w0_ref, wup_ref, a0_ref, aup_ref, kk_ref, ka_ref, rk_ref, gnw_ref,
         gnb_ref, o_ref, vf_out_ref, s_ref, prev_ref) = refs
    c = pl.program_id(1)

    @pl.when(c == 0)
    def _():
        s_ref[...] = jnp.zeros_like(s_ref)
        prev_ref[...] = jnp.zeros_like(prev_ref)

    y = rw_ref[0]
    row = _iota2(y.shape, 0)
    y_prev = jnp.where(row == 0, prev_ref[...], pltpu.roll(y, shift=1, axis=0))
    prev_ref[...] = y[CHUNK - 1:CHUNK, :]
    xs = y + mu_ref[...] * (y_prev - y)

    r = xs[:, 0:D_RWKV]
    k = xs[:, D_RWKV:2 * D_RWKV]
    v = xs[:, 2 * D_RWKV:3 * D_RWKV]
    z = xs[:, 3 * D_RWKV:4 * D_RWKV]
    wd = xs[:, 4 * D_RWKV:4 * D_RWKV + DECAY_RANK]
    ad = xs[:, 4 * D_RWKV + DECAY_RANK:RWKV_COLS]

    w_raw = w0_ref[...] + _dot(jnp.tanh(wd), wup_ref[...])
    logw = (-math.exp(-0.5)) * _sigmoid(w_raw)
    a = _sigmoid(a0_ref[...] + _dot(ad, aup_ref[...]))
    if has_vmix:
        gate = _sigmoid(v0_ref[...] + _dot(_dot(v, vdn_ref[...]), vup_ref[...]))
        v = v + (vf_ref[0] - v) * gate
    else:
        vf_out_ref[0] = v
    kk_raw = k * kk_ref[...]
    k = k * (1.0 + (a - 1.0) * ka_ref[...])

    tri = (_iota2((CHUNK, CHUNK), 1) <= _iota2((CHUNK, CHUNK), 0)).astype(F32)
    cl = _dot(tri, logw)
    cl_last = cl[CHUNK - 1:CHUNK, :]
    g_inc = jnp.exp(cl)
    g_exc = jnp.exp(cl - logw)
    g_inv = jnp.exp(-cl)
    g_end = jnp.exp(cl_last - cl)
    g_all = jnp.exp(cl_last)

    lane = _iota2((CHUNK, LANES), 1)
    m_lo = (lane < RWKV_HEAD).astype(F32)
    m_hi = 1.0 - m_lo
    head_mask = jnp.concatenate([m_lo, m_hi], axis=0)
    prow = _iota2((PAIR_ROWS, PAIR_ROWS), 0)
    pcol = _iota2((PAIR_ROWS, PAIR_ROWS), 1)
    same_head = (prow // CHUNK) == (pcol // CHUNK)
    t_idx = prow % CHUNK
    s_idx = pcol % CHUNK
    strict = (same_head & (s_idx < t_idx)).astype(F32)
    incl = (same_head & (s_idx <= t_idx)).astype(F32)
    eye = (prow == pcol).astype(F32)

    for p in range(RWKV_PAIRS):
        sl = slice(p * LANES, (p + 1) * LANES)

        def st(x):
            xp = x[:, sl]
            return jnp.concatenate([xp, xp], axis=0) * head_mask

        def tile(x):
            xp = x[:, sl]
            return jnp.concatenate([xp, xp], axis=0)

        r_s = st(r)
        k_s = st(k)
        v_s = st(v)
        kk_s = st(kk_raw)
        nrm = jnp.sqrt(jnp.sum(kk_s * kk_s, axis=-1, keepdims=True))
        kk_s = kk_s / jnp.maximum(nrm, 1e-12)
        b_s = kk_s * tile(a)
        a_t = -kk_s * tile(g_exc)
        r_t = r_s * tile(g_inc)
        ginv = tile(g_inv)
        b_t = b_s * ginv
        k_t = k_s * ginv
        gend = tile(g_end)
        b_h = b_s * gend
        k_h = k_s * gend

        a_ab = strict * _dot_nt(a_t, b_t)
        a_ak = strict * _dot_nt(a_t, k_t)
        a_rb = incl * _dot_nt(r_t, b_t)
        a_rk = incl * _dot_nt(r_t, k_t)

        pw = a_ab
        tinv = eye + pw
        for _ in range(int(math.log2(CHUNK)) - 1):
            pw = _dot(pw, pw)
            tinv = tinv + _dot(tinv, pw)

        s_old = s_ref[p]
        y0 = _dot_nt(a_t, s_old) + _dot(a_ak, v_s)
        u = _dot(tinv, y0)
        o = _dot_nt(r_t, s_old) + _dot(a_rb, u) + _dot(a_rk, v_s)
        s_ref[p] = s_old * g_all[:, sl] + _dot_tn(u, b_h) + _dot_tn(v_s, k_h)

        mean = jnp.sum(o, axis=-1, keepdims=True) * (1.0 / RWKV_HEAD)
        d = (o - mean) * head_mask
        var = jnp.sum(d * d, axis=-1, keepdims=True) * (1.0 / RWKV_HEAD)
        on = (d * lax.rsqrt(var + GN_EPS) * gnw_ref[:, sl] + gnb_ref[:, sl]) * head_mask
        bonus = jnp.sum(r_s * k_s * rk_ref[:, sl], axis=-1, keepdims=True)
        out_s = on + bonus * v_s
        out = out_s[:CHUNK, :] + out_s[CHUNK:, :]
        o_ref[0, :, sl] = out * _silu(z[:, sl])


def _rwkv_layer(proj_r, params, v_first, vmix):
    bsz, seq, _ = proj_r.shape
    has_vmix = vmix is not None
    nc = seq // CHUNK
    row_spec = lambda n: pl.BlockSpec((1, n), lambda b, c: (0, 0))
    full_spec = lambda s: pl.BlockSpec(s, lambda b, c: (0, 0))
    seq_spec = lambda n: pl.BlockSpec((1, CHUNK, n), lambda b, c: (b, c, 0))
    mu, w0, wup, a0, aup, kk, ka, rk, gnw, gnb = params
    args = [proj_r, mu, w0, wup, a0, aup, kk, ka, rk, gnw, gnb]
    in_specs = [seq_spec(RWKV_COLS), row_spec(RWKV_COLS), row_spec(D_RWKV),
                full_spec((DECAY_RANK, D_RWKV)), row_spec(D_RWKV), full_spec((A_RANK, D_RWKV)),
                row_spec(D_RWKV), row_spec(D_RWKV), row_spec(D_RWKV), row_spec(D_RWKV),
                row_spec(D_RWKV)]
    out_sds = jax.ShapeDtypeStruct((bsz, seq, D_RWKV), F32)
    if has_vmix:
        v0, vdn, vup = vmix
        args += [v_first, v0, vdn, vup]
        in_specs += [seq_spec(D_RWKV), row_spec(D_RWKV), full_spec((D_RWKV, VRES_RANK)),
                     full_spec((VRES_RANK, D_RWKV))]
        out_shape = out_sds
        out_specs = seq_spec(D_RWKV)
    else:
        out_shape = (out_sds, out_sds)
        out_specs = (seq_spec(D_RWKV), seq_spec(D_RWKV))
    res = pl.pallas_call(
        functools.partial(_rwkv_kernel, has_vmix),
        out_shape=out_shape,
        grid=(bsz, nc),
        in_specs=in_specs,
        out_specs=out_specs,
        scratch_shapes=[pltpu.VMEM((RWKV_PAIRS, PAIR_ROWS, LANES), F32),
                        pltpu.VMEM((1, RWKV_COLS), F32)],
        compiler_params=pltpu.CompilerParams(
            dimension_semantics=("arbitrary", "arbitrary"), vmem_limit_bytes=VMEM_LIMIT),
        name="rwkv7_chunk",
    )(*args)
    if has_vmix:
        return res, v_first
    return res


_HGRN_LEVELS = tuple(CHUNK >> (i + 1) for i in range(int(math.log2(CHUNK))))


def _hgrn_kernel(layer, hg_ref, lbl_ref, gw_ref, o_ref, s_ref):
    c = pl.program_id(1)

    @pl.when(c == 0)
    def _():
        s_ref[...] = jnp.zeros_like(s_ref)

    lg = lbl_ref[...]
    e = jnp.exp(lg - jnp.max(lg, axis=0, keepdims=True))
    sm = e / jnp.sum(e, axis=0, keepdims=True)
    lb = jnp.sum(sm[0:layer + 1, :], axis=0, keepdims=True) - sm[0:1, :]

    hg = hg_ref[0]
    q = _silu(hg[:, 0:D_HGRN])
    f_raw = hg[:, D_HGRN:2 * D_HGRN]
    i_in = hg[:, 2 * D_HGRN:3 * D_HGRN]
    z = hg[:, 3 * D_HGRN:4 * D_HGRN]

    log_lb = jnp.log(jnp.maximum(lb, LB_FLOOR))
    log_sig = jnp.minimum(f_raw, 0.0) - jnp.log1p(jnp.exp(-jnp.abs(f_raw)))
    yv = jnp.log1p(-lb) + log_sig
    log_f = jnp.maximum(log_lb, yv) + jnp.log1p(jnp.exp(-jnp.abs(log_lb - yv)))
    k = (1.0 - lb) * _sigmoid(-f_raw)

    trow = _iota2((CHUNK, CHUNK), 0)
    tcol = _iota2((CHUNK, CHUNK), 1)
    mats = [(tcol <= trow).astype(F32)]
    keep = []
    for m in _HGRN_LEVELS:
        mid = (trow // (2 * m)) * (2 * m) + m
        mats.append((tcol < mid).astype(F32))
        keep.append(((trow >= mid) & (tcol < mid)
                     & (tcol // (2 * m) == trow // (2 * m))).astype(F32))
    cums = _dot(jnp.concatenate(mats, axis=0), log_f)
    b = cums[0:CHUNK, :]
    b_last = b[CHUNK - 1:CHUNK, :]
    eye = (trow == tcol).astype(F32)

    q_in = q * jnp.exp(b)
    k_dec = k * jnp.exp(b_last - b)
    g_all = jnp.exp(b_last)

    for h in range(HGRN_HEADS):
        sl = slice(h * LANES, (h + 1) * LANES)
        q_h = q[:, sl]
        k_h = k[:, sl]
        b_h = b[:, sl]
        att = eye * jnp.sum(q_h * k_h, axis=-1, keepdims=True)
        for li in range(len(_HGRN_LEVELS)):
            b_ref = cums[(li + 1) * CHUNK:(li + 2) * CHUNK, sl]
            qe = q_h * jnp.exp(jnp.minimum(b_h - b_ref, 0.0))
            ke = k_h * jnp.exp(jnp.minimum(b_ref - b_h, 0.0))
            att = att + keep[li] * _dot_nt(qe, ke)
        s_old = s_ref[h]
        i_h = i_in[:, sl]
        o = _dot(att, i_h) + _dot_nt(q_in[:, sl], s_old)
        s_ref[h] = s_old * g_all[:, sl] + _dot_tn(i_h, k_dec[:, sl])
        o = o * lax.rsqrt(jnp.mean(o * o, axis=-1, keepdims=True) + RMS_EPS)
        o_ref[0, :, sl] = o * gw_ref[:, sl] * _silu(z[:, sl])


def _hgrn_layer(proj_h, lb_logits, g_norm_w, layer):
    bsz, seq, _ = proj_h.shape
    nc = seq // CHUNK
    return pl.pallas_call(
        functools.partial(_hgrn_kernel, layer),
        out_shape=jax.ShapeDtypeStruct((bsz, seq, D_HGRN), F32),
        grid=(bsz, nc),
        in_specs=[pl.BlockSpec((1, CHUNK, HGRN_COLS), lambda b, c: (b, c, 0)),
                  pl.BlockSpec((DEPTH, D_HGRN), lambda b, c: (0, 0)),
                  pl.BlockSpec((1, D_HGRN), lambda b, c: (0, 0))],
        out_specs=pl.BlockSpec((1, CHUNK, D_HGRN), lambda b, c: (b, c, 0)),
        scratch_shapes=[pltpu.VMEM((HGRN_HEADS, HGRN_EXPAND, LANES), F32)],
        compiler_params=pltpu.CompilerParams(
            dimension_semantics=("arbitrary", "arbitrary"), vmem_limit_bytes=VMEM_LIMIT),
        name="hgrn2_chunk",
    )(proj_h, lb_logits, g_norm_w)


def kernel(x, w_in, shift_mu, w_decay0, w_decay_up, a0, a_up, k_k, k_a, r_k, ln_x_w, ln_x_b,
           v_mix0, v_mix_down, v_mix_up, lb_logits, g_norm_w, w_out, ln_w, ln_b):
    out_dtype = x.dtype
    bsz, seq, _ = x.shape
    h = x.astype(F32).reshape(bsz * seq, D_MODEL)
    lb_logits = lb_logits.astype(F32)
    row = lambda t: t.reshape(1, -1)
    v_first = None
    for l in range(DEPTH):
        w_l = w_in[l].astype(jnp.bfloat16)
        proj_r = _matmul(h, w_l[:, :RWKV_COLS], 512, RWKV_COLS // 3)
        proj_h = _matmul(h, w_l[:, RWKV_COLS:], 512, HGRN_COLS // 4)
        proj_r = proj_r.reshape(bsz, seq, RWKV_COLS)
        proj_h = proj_h.reshape(bsz, seq, HGRN_COLS)
        params = (row(shift_mu[l]), row(w_decay0[l]), w_decay_up[l], row(a0[l]), a_up[l],
                  row(k_k[l]), row(k_a[l]), row(r_k[l]), row(ln_x_w[l]), row(ln_x_b[l]))
        vmix = None if l == 0 else (row(v_mix0[l - 1]), v_mix_down[l - 1], v_mix_up[l - 1])
        o_r, v_first = _rwkv_layer(proj_r, params, v_first, vmix)
        o_h = _hgrn_layer(proj_h, lb_logits, row(g_norm_w[l]), l)
        h = _out_proj_ln(o_r.reshape(bsz * seq, D_RWKV), o_h.reshape(bsz * seq, D_HGRN), h,
                         w_out[l].astype(jnp.bfloat16), row(ln_w[l]), row(ln_b[l]), 256)
    return h.reshape(bsz, seq, D_MODEL).astype(out_dtype)
```

```python
import functools
import math

import jax
import jax.numpy as jnp
from jax import lax
from jax.experimental import pallas as pl
from jax.experimental.pallas import tpu as pltpu

D_MODEL = 2048
DEPTH = 2
D_RWKV = D_MODEL // 2
D_HGRN = D_MODEL - D_RWKV
RWKV_HEAD = 64
DECAY_RANK = 64
A_RANK = 64
VRES_RANK = 32
HGRN_EXPAND = 128
HGRN_HEADS = D_HGRN // HGRN_EXPAND
RWKV_COLS = 4 * D_RWKV + DECAY_RANK + A_RANK
HGRN_COLS = 4 * D_HGRN
ALPHA = (2 * DEPTH) ** 0.25
LN_EPS = 1e-5
GN_EPS = 64e-5
RMS_EPS = 1e-5
LB_FLOOR = 1e-30

LANES = 128
CHUNK = 64
PAIR_ROWS = 2 * CHUNK
RWKV_PAIRS = D_RWKV // LANES
VMEM_LIMIT = 56 * 1024 * 1024

HI = lax.Precision.HIGHEST
F32 = jnp.float32


def _dot(a, b, precision=HI):
    return jnp.dot(a, b, precision=precision, preferred_element_type=F32)


def _dot_nt(a, b, precision=HI):
    return lax.dot_general(a, b, (((1,), (1,)), ((), ())), precision=precision,
                           preferred_element_type=F32)


def _dot_tn(a, b, precision=HI):
    return lax.dot_general(a, b, (((0,), (0,)), ((), ())), precision=precision,
                           preferred_element_type=F32)


def _bf(x):
    return x.astype(jnp.bfloat16)


def _bdot(a, b):
    return jnp.dot(_bf(a), _bf(b), preferred_element_type=F32)


def _bdot_nt(a, b):
    return lax.dot_general(_bf(a), _bf(b), (((1,), (1,)), ((), ())), preferred_element_type=F32)


def _bdot_tn(a, b):
    return lax.dot_general(_bf(a), _bf(b), (((0,), (0,)), ((), ())), preferred_element_type=F32)


def _sigmoid(x):
    return 1.0 / (1.0 + jnp.exp(-x))


def _silu(x):
    return x * _sigmoid(x)


def _iota2(shape, dim):
    return lax.broadcasted_iota(jnp.int32, shape, dim)


def _mm_kernel(x_ref, w_ref, o_ref):
    o_ref[...] = jnp.dot(x_ref[...].astype(jnp.bfloat16), w_ref[...],
                         preferred_element_type=F32)


def _matmul(x, w, tm, tn):
    m, k = x.shape
    n = w.shape[1]
    return pl.pallas_call(
        _mm_kernel,
        out_shape=jax.ShapeDtypeStruct((m, n), F32),
        grid=(n // tn, m // tm),
        in_specs=[pl.BlockSpec((tm, k), lambda j, i: (i, 0)),
                  pl.BlockSpec((k, tn), lambda j, i: (0, j))],
        out_specs=pl.BlockSpec((tm, tn), lambda j, i: (i, j)),
        compiler_params=pltpu.CompilerParams(
            dimension_semantics=("arbitrary", "arbitrary"), vmem_limit_bytes=VMEM_LIMIT),
        name="in_proj",
    )(x, w)


def _out_kernel(or_ref, oh_ref, h_ref, w_ref, lnw_ref, lnb_ref, o_ref):
    y = jnp.dot(or_ref[...].astype(jnp.bfloat16), w_ref[:D_RWKV, :], preferred_element_type=F32)
    y = y + jnp.dot(oh_ref[...].astype(jnp.bfloat16), w_ref[D_RWKV:, :],
                    preferred_element_type=F32)
    u = ALPHA * h_ref[...] + y
    mu = jnp.mean(u, axis=-1, keepdims=True)
    d = u - mu
    var = jnp.mean(d * d, axis=-1, keepdims=True)
    o_ref[...] = d * lax.rsqrt(var + LN_EPS) * lnw_ref[...] + lnb_ref[...]


def _out_proj_ln(o_r, o_h, h, w, lnw, lnb, tm):
    m = h.shape[0]
    return pl.pallas_call(
        _out_kernel,
        out_shape=jax.ShapeDtypeStruct((m, D_MODEL), F32),
        grid=(m // tm,),
        in_specs=[pl.BlockSpec((tm, D_RWKV), lambda i: (i, 0)),
                  pl.BlockSpec((tm, D_HGRN), lambda i: (i, 0)),
                  pl.BlockSpec((tm, D_MODEL), lambda i: (i, 0)),
                  pl.BlockSpec((D_MODEL, D_MODEL), lambda i: (0, 0)),
                  pl.BlockSpec((1, D_MODEL), lambda i: (0, 0)),
                  pl.BlockSpec((1, D_MODEL), lambda i: (0, 0))],
        out_specs=pl.BlockSpec((tm, D_MODEL), lambda i: (i, 0)),
        compiler_params=pltpu.CompilerParams(
            dimension_semantics=("arbitrary",), vmem_limit_bytes=VMEM_LIMIT),
        name="out_proj_ln",
    )(o_r, o_h, h, w, lnw, lnb)


def _rwkv_kernel(has_vmix, *refs):
    if has_vmix:
        (rw_ref, mu_ref, w0_ref, wup_ref, a0_ref, aup_ref, kk_ref, ka_ref, rk_ref, gnw_ref,
         gnb_ref, vf_ref, v0_ref, vdn_ref, vup_ref, o_ref, s_ref, prev_ref) = refs
    else:
        (rw_ref, mu_ref, w0_ref, wup_ref, a0_ref, aup_ref, kk_ref, ka_ref, rk_ref, gnw_ref,
         gnb_ref, o_ref, vf_out_ref, s_ref, prev_ref) = refs
    c = pl.program_id(1)

    @pl.when(c == 0)
    def _():
        s_ref[...] = jnp.zeros_like(s_ref)
        prev_ref[...] = jnp.zeros_like(prev_ref)

    y = rw_ref[0]
    row = _iota2(y.shape, 0)
    y_prev = jnp.where(row == 0, prev_ref[...], pltpu.roll(y, shift=1, axis=0))
    prev_ref[...] = y[CHUNK - 1:CHUNK, :]
    xs = y + mu_ref[...] * (y_prev - y)

    r = xs[:, 0:D_RWKV]
    k = xs[:, D_RWKV:2 * D_RWKV]
    v = xs[:, 2 * D_RWKV:3 * D_RWKV]
    z = xs[:, 3 * D_RWKV:4 * D_RWKV]
    wd = xs[:, 4 * D_RWKV:4 * D_RWKV + DECAY_RANK]
    ad = xs[:, 4 * D_RWKV + DECAY_RANK:RWKV_COLS]

    w_raw = w0_ref[...] + _bdot(jnp.tanh(wd), wup_ref[...])
    logw = (-math.exp(-0.5)) * _sigmoid(w_raw)
    a = _sigmoid(a0_ref[...] + _bdot(ad, aup_ref[...]))
    if has_vmix:
        gate = _sigmoid(v0_ref[...] + _bdot(_bdot(v, vdn_ref[...]), vup_ref[...]))
        v = v + (vf_ref[0] - v) * gate
    else:
        vf_out_ref[0] = v
    kk_raw = k * kk_ref[...]
    k = k * (1.0 + (a - 1.0) * ka_ref[...])

    tri = (_iota2((CHUNK, CHUNK), 1) <= _iota2((CHUNK, CHUNK), 0)).astype(F32)
    cl = _dot(tri, logw)
    cl_last = cl[CHUNK - 1:CHUNK, :]
    g_inc = jnp.exp(cl)
    g_exc = jnp.exp(cl - logw)
    g_inv = jnp.exp(-cl)
    g_end = jnp.exp(cl_last - cl)
    g_all = jnp.exp(cl_last)

    lane = _iota2((CHUNK, LANES), 1)
    m_lo = (lane < RWKV_HEAD).astype(F32)
    m_hi = 1.0 - m_lo
    head_mask = jnp.concatenate([m_lo, m_hi], axis=0)
    prow = _iota2((PAIR_ROWS, PAIR_ROWS), 0)
    pcol = _iota2((PAIR_ROWS, PAIR_ROWS), 1)
    same_head = (prow // CHUNK) == (pcol // CHUNK)
    t_idx = prow % CHUNK
    s_idx = pcol % CHUNK
    strict = (same_head & (s_idx < t_idx)).astype(F32)
    incl = (same_head & (s_idx <= t_idx)).astype(F32)
    eye = (prow == pcol).astype(F32)

    pairs = range(RWKV_PAIRS)
    sls = [slice(p * LANES, (p + 1) * LANES) for p in pairs]

    def tile(x, p):
        xp = x[:, sls[p]]
        return jnp.concatenate([xp, xp], axis=0)

    def st(x, p):
        return tile(x, p) * head_mask

    r_s, k_s, v_s, b_h, k_h, ar, bk = [], [], [], [], [], [], []
    for p in pairs:
        kk_s = st(kk_raw, p)
        nrm = jnp.sqrt(jnp.sum(kk_s * kk_s, axis=-1, keepdims=True))
        kk_s = kk_s / jnp.maximum(nrm, 1e-12)
        b_s = kk_s * tile(a, p)
        k_s.append(st(k, p))
        v_s.append(st(v, p))
        r_s.append(tile(r, p))
        ginv = tile(g_inv, p)
        gend = tile(g_end, p)
        b_h.append(b_s * gend)
        k_h.append(k_s[p] * gend)
        a_t = -kk_s * tile(g_exc, p)
        r_t = r_s[p] * tile(g_inc, p)
        ar.append(_bf(jnp.concatenate([a_t, r_t], axis=0)))
        bk.append(_bf(jnp.concatenate([b_s * ginv, k_s[p] * ginv], axis=0)))

    sc = [_bdot_nt(ar[p], bk[p]) for p in pairs]
    pw = [strict * sc[p][:PAIR_ROWS, :PAIR_ROWS] for p in pairs]
    tinv = [eye + pw[p] for p in pairs]
    for _ in range(int(math.log2(CHUNK)) - 1):
        pwb = [_bf(x) for x in pw]
        pw = [_bdot(x, x) for x in pwb]
        tinv = [tinv[p] + _bdot(tinv[p], pw[p]) for p in pairs]
    a_kv = [_bf(jnp.concatenate([strict * sc[p][:PAIR_ROWS, PAIR_ROWS:],
                                 incl * sc[p][PAIR_ROWS:, PAIR_ROWS:]], axis=0)) for p in pairs]
    a_rb = [_bf(incl * sc[p][PAIR_ROWS:, :PAIR_ROWS]) for p in pairs]

    s_old = [s_ref[p] for p in pairs]
    ars = [_bdot_nt(ar[p], s_old[p]) for p in pairs]
    akv = [_bdot(a_kv[p], v_s[p]) for p in pairs]
    u = [head_mask * _bdot(tinv[p], ars[p][:PAIR_ROWS] + akv[p][:PAIR_ROWS]) for p in pairs]
    for p in pairs:
        s_ref[p] = s_old[p] * g_all[:, sls[p]] + _bdot_tn(
            jnp.concatenate([u[p], v_s[p]], axis=0), jnp.concatenate([b_h[p], k_h[p]], axis=0))
    o_s = [ars[p][PAIR_ROWS:] + _bdot(a_rb[p], u[p]) + akv[p][PAIR_ROWS:] for p in pairs]

    for p in pairs:
        sl = sls[p]
        o = o_s[p] * head_mask
        mean = jnp.sum(o, axis=-1, keepdims=True) * (1.0 / RWKV_HEAD)
        d = (o - mean) * head_mask
        var = jnp.sum(d * d, axis=-1, keepdims=True) * (1.0 / RWKV_HEAD)
        on = (d * lax.rsqrt(var + GN_EPS) * gnw_ref[:, sl] + gnb_ref[:, sl]) * head_mask
        bonus = jnp.sum(r_s[p] * k_s[p] * rk_ref[:, sl], axis=-1, keepdims=True)
        out_s = on + bonus * v_s[p]
        out = out_s[:CHUNK, :] + out_s[CHUNK:, :]
        o_ref[0, :, sl] = out * _silu(z[:, sl])


def _rwkv_layer(proj_r, params, v_first, vmix):
    bsz, seq, _ = proj_r.shape
    has_vmix = vmix is not None
    nc = seq // CHUNK
    row_spec = lambda n: pl.BlockSpec((1, n), lambda b, c: (0, 0))
    full_spec = lambda s: pl.BlockSpec(s, lambda b, c: (0, 0))
    seq_spec = lambda n: pl.BlockSpec((1, CHUNK, n), lambda b, c: (b, c, 0))
    mu, w0, wup, a0, aup, kk, ka, rk, gnw, gnb = params
    args = [proj_r, mu, w0, wup, a0, aup, kk, ka, rk, gnw, gnb]
    in_specs = [seq_spec(RWKV_COLS), row_spec(RWKV_COLS), row_spec(D_RWKV),
                full_spec((DECAY_RANK, D_RWKV)), row_spec(D_RWKV), full_spec((A_RANK, D_RWKV)),
                row_spec(D_RWKV), row_spec(D_RWKV), row_spec(D_RWKV), row_spec(D_RWKV),
                row_spec(D_RWKV)]
    out_sds = jax.ShapeDtypeStruct((bsz, seq, D_RWKV), F32)
    if has_vmix:
        v0, vdn, vup = vmix
        args += [v_first, v0, vdn, vup]
        in_specs += [seq_spec(D_RWKV), row_spec(D_RWKV), full_spec((D_RWKV, VRES_RANK)),
                     full_spec((VRES_RANK, D_RWKV))]
        out_shape = out_sds
        out_specs = seq_spec(D_RWKV)
    else:
        out_shape = (out_sds, out_sds)
        out_specs = (seq_spec(D_RWKV), seq_spec(D_RWKV))
    res = pl.pallas_call(
        functools.partial(_rwkv_kernel, has_vmix),
        out_shape=out_shape,
        grid=(bsz, nc),
        in_specs=in_specs,
        out_specs=out_specs,
        scratch_shapes=[pltpu.VMEM((RWKV_PAIRS, PAIR_ROWS, LANES), F32),
                        pltpu.VMEM((1, RWKV_COLS), F32)],
        compiler_params=pltpu.CompilerParams(
            dimension_semantics=("arbitrary", "arbitrary"), vmem_limit_bytes=VMEM_LIMIT),
        name="rwkv7_chunk",
    )(*args)
    if has_vmix:
        return res, v_first
    return res


_HGRN_LEVELS = tuple(CHUNK >> (i + 1) for i in range(int(math.log2(CHUNK))))


def _hgrn_kernel(layer, hg_ref, lbl_ref, gw_ref, o_ref, s_ref):
    c = pl.program_id(1)

    @pl.when(c == 0)
    def _():
        s_ref[...] = jnp.zeros_like(s_ref)

    lg = lbl_ref[...]
    e = jnp.exp(lg - jnp.max(lg, axis=0, keepdims=True))
    sm = e / jnp.sum(e, axis=0, keepdims=True)
    lb = jnp.sum(sm[0:layer + 1, :], axis=0, keepdims=True) - sm[0:1, :]

    hg = hg_ref[0]
    q = _silu(hg[:, 0:D_HGRN])
    f_raw = hg[:, D_HGRN:2 * D_HGRN]
    i_in = hg[:, 2 * D_HGRN:3 * D_HGRN]
    z = hg[:, 3 * D_HGRN:4 * D_HGRN]

    log_lb = jnp.log(jnp.maximum(lb, LB_FLOOR))
    log_sig = jnp.minimum(f_raw, 0.0) - jnp.log1p(jnp.exp(-jnp.abs(f_raw)))
    yv = jnp.log1p(-lb) + log_sig
    log_f = jnp.maximum(log_lb, yv) + jnp.log1p(jnp.exp(-jnp.abs(log_lb - yv)))
    k = (1.0 - lb) * _sigmoid(-f_raw)

    trow = _iota2((CHUNK, CHUNK), 0)
    tcol = _iota2((CHUNK, CHUNK), 1)
    mats = [(tcol <= trow).astype(F32)]
    keep = []
    for m in _HGRN_LEVELS:
        mid = (trow // (2 * m)) * (2 * m) + m
        mats.append((tcol < mid).astype(F32))
        keep.append(((trow >= mid) & (tcol < mid)
                     & (tcol // (2 * m) == trow // (2 * m))).astype(F32))
    cums = _dot(jnp.concatenate(mats, axis=0), log_f)
    b = cums[0:CHUNK, :]
    b_last = b[CHUNK - 1:CHUNK, :]
    eye = (trow == tcol).astype(F32)

    q_in = q * jnp.exp(b)
    k_dec = k * jnp.exp(b_last - b)
    g_all = jnp.exp(b_last)

    heads = range(HGRN_HEADS)
    sls = [slice(h * LANES, (h + 1) * LANES) for h in heads]
    att = [eye * jnp.sum(q[:, sl] * k[:, sl], axis=-1, keepdims=True) for sl in sls]
    for li in range(len(_HGRN_LEVELS)):
        b_ref = cums[(li + 1) * CHUNK:(li + 2) * CHUNK, :]
        qe = _bf(q * jnp.exp(jnp.minimum(b - b_ref, 0.0)))
        ke = _bf(k * jnp.exp(jnp.minimum(b_ref - b, 0.0)))
        att = [att[h] + keep[li] * _bdot_nt(qe[:, sls[h]], ke[:, sls[h]]) for h in heads]
    s_old = [s_ref[h] for h in heads]
    i_b = _bf(i_in)
    o_h = [_bdot(att[h], i_b[:, sls[h]]) + _bdot_nt(q_in[:, sls[h]], s_old[h]) for h in heads]
    for h in heads:
        sl = sls[h]
        s_ref[h] = s_old[h] * g_all[:, sl] + _bdot_tn(i_b[:, sl], k_dec[:, sl])
    for h in heads:
        sl = sls[h]
        o = o_h[h]
        o = o * lax.rsqrt(jnp.mean(o * o, axis=-1, keepdims=True) + RMS_EPS)
        o_ref[0, :, sl] = o * gw_ref[:, sl] * _silu(z[:, sl])


def _hgrn_layer(proj_h, lb_logits, g_norm_w, layer):
    bsz, seq, _ = proj_h.shape
    nc = seq // CHUNK
    return pl.pallas_call(
        functools.partial(_hgrn_kernel, layer),
        out_shape=jax.ShapeDtypeStruct((bsz, seq, D_HGRN), F32),
        grid=(bsz, nc),
        in_specs=[pl.BlockSpec((1, CHUNK, HGRN_COLS), lambda b, c: (b, c, 0)),
                  pl.BlockSpec((DEPTH, D_HGRN), lambda b, c: (0, 0)),
                  pl.BlockSpec((1, D_HGRN), lambda b, c: (0, 0))],
        out_specs=pl.BlockSpec((1, CHUNK, D_HGRN), lambda b, c: (b, c, 0)),
        scratch_shapes=[pltpu.VMEM((HGRN_HEADS, HGRN_EXPAND, LANES), F32)],
        compiler_params=pltpu.CompilerParams(
            dimension_semantics=("arbitrary", "arbitrary"), vmem_limit_bytes=VMEM_LIMIT),
        name="hgrn2_chunk",
    )(proj_h, lb_logits, g_norm_w)


def kernel(x, w_in, shift_mu, w_decay0, w_decay_up, a0, a_up, k_k, k_a, r_k, ln_x_w, ln_x_b,
           v_mix0, v_mix_down, v_mix_up, lb_logits, g_norm_w, w_out, ln_w, ln_b):
    out_dtype = x.dtype
    bsz, seq, _ = x.shape
    h = x.astype(F32).reshape(bsz * seq, D_MODEL)
    lb_logits = lb_logits.astype(F32)
    row = lambda t: t.reshape(1, -1)
    v_first = None
    for l in range(DEPTH):
        w_l = w_in[l].astype(jnp.bfloat16)
        proj_r = _matmul(h, w_l[:, :RWKV_COLS], 512, RWKV_COLS // 3)
        proj_h = _matmul(h, w_l[:, RWKV_COLS:], 512, HGRN_COLS // 4)
        proj_r = proj_r.reshape(bsz, seq, RWKV_COLS)
        proj_h = proj_h.reshape(bsz, seq, HGRN_COLS)
        params = (row(shift_mu[l]), row(w_decay0[l]), w_decay_up[l], row(a0[l]), a_up[l],
                  row(k_k[l]), row(k_a[l]), row(r_k[l]), row(ln_x_w[l]), row(ln_x_b[l]))
        vmix = None if l == 0 else (row(v_mix0[l - 1]), v_mix_down[l - 1], v_mix_up[l - 1])
        o_r, v_first = _rwkv_layer(proj_r, params, v_first, vmix)
        o_h = _hgrn_layer(proj_h, lb_logits, row(g_norm_w[l]), l)
        h = _out_proj_ln(o_r.reshape(bsz * seq, D_RWKV), o_h.reshape(bsz * seq, D_HGRN), h,
                         w_out[l].astype(jnp.bfloat16), row(ln_w[l]), row(ln_b[l]), 256)
    return h.reshape(bsz, seq, D_MODEL).astype(out_dtype)
```

```python
import functools
import math

import jax
import jax.numpy as jnp
from jax import lax
from jax.experimental import pallas as pl
from jax.experimental.pallas import tpu as pltpu

D_MODEL = 2048
DEPTH = 2
D_RWKV = D_MODEL // 2
D_HGRN = D_MODEL - D_RWKV
RWKV_HEAD = 64
DECAY_RANK = 64
A_RANK = 64
VRES_RANK = 32
HGRN_EXPAND = 128
HGRN_HEADS = D_HGRN // HGRN_EXPAND
RWKV_COLS = 4 * D_RWKV + DECAY_RANK + A_RANK
HGRN_COLS = 4 * D_HGRN
ALPHA = (2 * DEPTH) ** 0.25
LN_EPS = 1e-5
GN_EPS = 64e-5
RMS_EPS = 1e-5
LB_FLOOR = 1e-30

LANES = 128
CHUNK = 64
PAIR_ROWS = 2 * CHUNK
RWKV_PAIRS = D_RWKV // LANES
VMEM_LIMIT = 56 * 1024 * 1024

HI = lax.Precision.HIGHEST
F32 = jnp.float32


def _dot(a, b, precision=HI):
    return jnp.dot(a, b, precision=precision, preferred_element_type=F32)


def _dot_nt(a, b, precision=HI):
    return lax.dot_general(a, b, (((1,), (1,)), ((), ())), precision=precision,
                           preferred_element_type=F32)


def _dot_tn(a, b, precision=HI):
    return lax.dot_general(a, b, (((0,), (0,)), ((), ())), precision=precision,
                           preferred_element_type=F32)


def _bf(x):
    return x.astype(jnp.bfloat16)


def _bdot(a, b):
    return jnp.dot(_bf(a), _bf(b), preferred_element_type=F32)


def _bdot_nt(a, b):
    return lax.dot_general(_bf(a), _bf(b), (((1,), (1,)), ((), ())), preferred_element_type=F32)


def _bdot_tn(a, b):
    return lax.dot_general(_bf(a), _bf(b), (((0,), (0,)), ((), ())), preferred_element_type=F32)


def _split_dot(m01, x, terms):
    mb = _bf(m01)
    acc = None
    rem = x
    for t in range(terms):
        piece = _bf(rem)
        part = jnp.dot(mb, piece, preferred_element_type=F32)
        acc = part if acc is None else acc + part
        if t + 1 < terms:
            rem = rem - piece.astype(F32)
    return acc


def _sigmoid(x):
    return 1.0 / (1.0 + jnp.exp(-x))


def _silu(x):
    return x * _sigmoid(x)


def _iota2(shape, dim):
    return lax.broadcasted_iota(jnp.int32, shape, dim)


def _mm_kernel(x_ref, w_ref, o_ref, wb_ref):
    @pl.when(pl.program_id(1) == 0)
    def _():
        wb_ref[...] = w_ref[0].astype(jnp.bfloat16)

    o_ref[...] = jnp.dot(x_ref[...].astype(jnp.bfloat16), wb_ref[...],
                         preferred_element_type=F32)


def _in_proj(x, w_in, layer, col0, n, tm, tn):
    m, k = x.shape
    return pl.pallas_call(
        _mm_kernel,
        out_shape=jax.ShapeDtypeStruct((m, n), F32),
        grid=(n // tn, m // tm),
        in_specs=[pl.BlockSpec((tm, k), lambda j, i: (i, 0)),
                  pl.BlockSpec((pl.Element(1), pl.Element(k), pl.Element(tn)),
                               lambda j, i: (layer, 0, pl.multiple_of(col0 + j * tn, LANES)))],
        out_specs=pl.BlockSpec((tm, tn), lambda j, i: (i, j)),
        scratch_shapes=[pltpu.VMEM((k, tn), jnp.bfloat16)],
        compiler_params=pltpu.CompilerParams(
            dimension_semantics=("arbitrary", "arbitrary"), vmem_limit_bytes=VMEM_LIMIT),
        name="in_proj",
    )(x, w_in)


def _out_kernel(or_ref, oh_ref, h_ref, w_ref, lnw_ref, lnb_ref, o_ref, wb_ref):
    @pl.when(pl.program_id(0) == 0)
    def _():
        wb_ref[...] = w_ref[...].astype(jnp.bfloat16)

    y = jnp.dot(or_ref[...].astype(jnp.bfloat16), wb_ref[:D_RWKV, :], preferred_element_type=F32)
    y = y + jnp.dot(oh_ref[...].astype(jnp.bfloat16), wb_ref[D_RWKV:, :],
                    preferred_element_type=F32)
    u = ALPHA * h_ref[...] + y
    mu = jnp.mean(u, axis=-1, keepdims=True)
    d = u - mu
    var = jnp.mean(d * d, axis=-1, keepdims=True)
    o_ref[...] = d * lax.rsqrt(var + LN_EPS) * lnw_ref[...] + lnb_ref[...]


def _out_proj_ln(o_r, o_h, h, w_out, layer, lnw, lnb, tm):
    m = h.shape[0]
    return pl.pallas_call(
        _out_kernel,
        out_shape=jax.ShapeDtypeStruct((m, D_MODEL), F32),
        grid=(m // tm,),
        in_specs=[pl.BlockSpec((tm, D_RWKV), lambda i: (i, 0)),
                  pl.BlockSpec((tm, D_HGRN), lambda i: (i, 0)),
                  pl.BlockSpec((tm, D_MODEL), lambda i: (i, 0)),
                  pl.BlockSpec((None, D_MODEL, D_MODEL), lambda i: (layer, 0, 0),
                               pipeline_mode=pl.Buffered(1)),
                  pl.BlockSpec((1, D_MODEL), lambda i: (0, 0)),
                  pl.BlockSpec((1, D_MODEL), lambda i: (0, 0))],
        out_specs=pl.BlockSpec((tm, D_MODEL), lambda i: (i, 0)),
        scratch_shapes=[pltpu.VMEM((D_MODEL, D_MODEL), jnp.bfloat16)],
        compiler_params=pltpu.CompilerParams(
            dimension_semantics=("arbitrary",), vmem_limit_bytes=VMEM_LIMIT),
        name="out_proj_ln",
    )(o_r, o_h, h, w_out, lnw, lnb)


def _rwkv_kernel(has_vmix, *refs):
    if has_vmix:
        (rw_ref, mu_ref, w0_ref, wup_ref, a0_ref, aup_ref, kk_ref, ka_ref, rk_ref, gnw_ref,
         gnb_ref, vf_ref, v0_ref, vdn_ref, vup_ref, o_ref, s_ref, prev_ref) = refs
    else:
        (rw_ref, mu_ref, w0_ref, wup_ref, a0_ref, aup_ref, kk_ref, ka_ref, rk_ref, gnw_ref,
         gnb_ref, o_ref, vf_out_ref, s_ref, prev_ref) = refs
    c = pl.program_id(1)

    @pl.when(c == 0)
    def _():
        s_ref[...] = jnp.zeros_like(s_ref)
        prev_ref[...] = jnp.zeros_like(prev_ref)

    y = rw_ref[0]
    row = _iota2(y.shape, 0)
    y_prev = jnp.where(row == 0, prev_ref[...], pltpu.roll(y, shift=1, axis=0))
    prev_ref[...] = y[CHUNK - 1:CHUNK, :]
    xs = y + mu_ref[...] * (y_prev - y)

    r = xs[:, 0:D_RWKV]
    k = xs[:, D_RWKV:2 * D_RWKV]
    v = xs[:, 2 * D_RWKV:3 * D_RWKV]
    z = xs[:, 3 * D_RWKV:4 * D_RWKV]
    wd = xs[:, 4 * D_RWKV:4 * D_RWKV + DECAY_RANK]
    ad = xs[:, 4 * D_RWKV + DECAY_RANK:RWKV_COLS]

    w_raw = w0_ref[...] + _bdot(jnp.tanh(wd), wup_ref[...])
    logw = (-math.exp(-0.5)) * _sigmoid(w_raw)
    a = _sigmoid(a0_ref[...] + _bdot(ad, aup_ref[...]))
    if has_vmix:
        gate = _sigmoid(v0_ref[...] + _bdot(_bdot(v, vdn_ref[...]), vup_ref[...]))
        v = v + (vf_ref[0] - v) * gate
    else:
        vf_out_ref[0] = v
    kk_raw = k * kk_ref[...]
    k = k * (1.0 + (a - 1.0) * ka_ref[...])

    tri = (_iota2((CHUNK, CHUNK), 1) <= _iota2((CHUNK, CHUNK), 0)).astype(F32)
    cl = _split_dot(tri, logw, 3)
    cl_last = cl[CHUNK - 1:CHUNK, :]
    g_inc = jnp.exp(cl)
    g_exc = jnp.exp(cl - logw)
    g_inv = jnp.exp(-cl)
    g_end = jnp.exp(cl_last - cl)
    g_all = jnp.exp(cl_last)

    lane = _iota2((CHUNK, LANES), 1)
    m_lo = (lane < RWKV_HEAD).astype(F32)
    m_hi = 1.0 - m_lo
    head_mask = jnp.concatenate([m_lo, m_hi], axis=0)
    prow = _iota2((PAIR_ROWS, PAIR_ROWS), 0)
    pcol = _iota2((PAIR_ROWS, PAIR_ROWS), 1)
    same_head = (prow // CHUNK) == (pcol // CHUNK)
    t_idx = prow % CHUNK
    s_idx = pcol % CHUNK
    strict = (same_head & (s_idx < t_idx)).astype(F32)
    incl = (same_head & (s_idx <= t_idx)).astype(F32)
    eye = (prow == pcol).astype(F32)

    pairs = range(RWKV_PAIRS)
    sls = [slice(p * LANES, (p + 1) * LANES) for p in pairs]

    def tile(x, p):
        xp = x[:, sls[p]]
        return jnp.concatenate([xp, xp], axis=0)

    def st(x, p):
        return tile(x, p) * head_mask

    r_s, k_s, v_s, b_h, k_h, ar, bk = [], [], [], [], [], [], []
    for p in pairs:
        kk_s = st(kk_raw, p)
        nrm = jnp.sqrt(jnp.sum(kk_s * kk_s, axis=-1, keepdims=True))
        kk_s = kk_s / jnp.maximum(nrm, 1e-12)
        b_s = kk_s * tile(a, p)
        k_s.append(st(k, p))
        v_s.append(st(v, p))
        r_s.append(tile(r, p))
        ginv = tile(g_inv, p)
        gend = tile(g_end, p)
        b_h.append(b_s * gend)
        k_h.append(k_s[p] * gend)
        a_t = -kk_s * tile(g_exc, p)
        r_t = r_s[p] * tile(g_inc, p)
        ar.append(_bf(jnp.concatenate([a_t, r_t], axis=0)))
        bk.append(_bf(jnp.concatenate([b_s * ginv, k_s[p] * ginv], axis=0)))

    sc = [_bdot_nt(ar[p], bk[p]) for p in pairs]
    pw = [strict * sc[p][:PAIR_ROWS, :PAIR_ROWS] for p in pairs]
    tinv = [eye + pw[p] for p in pairs]
    for _ in range(int(math.log2(CHUNK)) - 1):
        pwb = [_bf(x) for x in pw]
        pw = [_bdot(x, x) for x in pwb]
        tinv = [tinv[p] + _bdot(tinv[p], pw[p]) for p in pairs]
    a_kv = [_bf(jnp.concatenate([strict * sc[p][:PAIR_ROWS, PAIR_ROWS:],
                                 incl * sc[p][PAIR_ROWS:, PAIR_ROWS:]], axis=0)) for p in pairs]
    a_rb = [_bf(incl * sc[p][PAIR_ROWS:, :PAIR_ROWS]) for p in pairs]

    s_old = [s_ref[p] for p in pairs]
    ars = [_bdot_nt(ar[p], s_old[p]) for p in pairs]
    akv = [_bdot(a_kv[p], v_s[p]) for p in pairs]
    u = [head_mask * _bdot(tinv[p], ars[p][:PAIR_ROWS] + akv[p][:PAIR_ROWS]) for p in pairs]
    for p in pairs:
        s_ref[p] = s_old[p] * g_all[:, sls[p]] + _bdot_tn(
            jnp.concatenate([u[p], v_s[p]], axis=0), jnp.concatenate([b_h[p], k_h[p]], axis=0))
    o_s = [ars[p][PAIR_ROWS:] + _bdot(a_rb[p], u[p]) + akv[p][PAIR_ROWS:] for p in pairs]

    for p in pairs:
        sl = sls[p]
        o = o_s[p] * head_mask
        mean = jnp.sum(o, axis=-1, keepdims=True) * (1.0 / RWKV_HEAD)
        d = (o - mean) * head_mask
        var = jnp.sum(d * d, axis=-1, keepdims=True) * (1.0 / RWKV_HEAD)
        on = (d * lax.rsqrt(var + GN_EPS) * gnw_ref[:, sl] + gnb_ref[:, sl]) * head_mask
        bonus = jnp.sum(r_s[p] * k_s[p] * rk_ref[:, sl], axis=-1, keepdims=True)
        out_s = on + bonus * v_s[p]
        out = out_s[:CHUNK, :] + out_s[CHUNK:, :]
        o_ref[0, :, sl] = out * _silu(z[:, sl])


def _rwkv_layer(proj_r, params, v_first, vmix):
    bsz, seq, _ = proj_r.shape
    has_vmix = vmix is not None
    nc = seq // CHUNK
    row_spec = lambda n: pl.BlockSpec((1, n), lambda b, c: (0, 0))
    full_spec = lambda s: pl.BlockSpec(s, lambda b, c: (0, 0))
    seq_spec = lambda n: pl.BlockSpec((1, CHUNK, n), lambda b, c: (b, c, 0))
    mu, w0, wup, a0, aup, kk, ka, rk, gnw, gnb = params
    args = [proj_r, mu, w0, wup, a0, aup, kk, ka, rk, gnw, gnb]
    in_specs = [seq_spec(RWKV_COLS), row_spec(RWKV_COLS), row_spec(D_RWKV),
                full_spec((DECAY_RANK, D_RWKV)), row_spec(D_RWKV), full_spec((A_RANK, D_RWKV)),
                row_spec(D_RWKV), row_spec(D_RWKV), row_spec(D_RWKV), row_spec(D_RWKV),
                row_spec(D_RWKV)]
    out_sds = jax.ShapeDtypeStruct((bsz, seq, D_RWKV), F32)
    if has_vmix:
        v0, vdn, vup = vmix
        args += [v_first, v0, vdn, vup]
        in_specs += [seq_spec(D_RWKV), row_spec(D_RWKV), full_spec((D_RWKV, VRES_RANK)),
                     full_spec((VRES_RANK, D_RWKV))]
        out_shape = out_sds
        out_specs = seq_spec(D_RWKV)
    else:
        out_shape = (out_sds, out_sds)
        out_specs = (seq_spec(D_RWKV), seq_spec(D_RWKV))
    res = pl.pallas_call(
        functools.partial(_rwkv_kernel, has_vmix),
        out_shape=out_shape,
        grid=(bsz, nc),
        in_specs=in_specs,
        out_specs=out_specs,
        scratch_shapes=[pltpu.VMEM((RWKV_PAIRS, PAIR_ROWS, LANES), F32),
                        pltpu.VMEM((1, RWKV_COLS), F32)],
        compiler_params=pltpu.CompilerParams(
            dimension_semantics=("arbitrary", "arbitrary"), vmem_limit_bytes=VMEM_LIMIT),
        name="rwkv7_chunk",
    )(*args)
    if has_vmix:
        return res, v_first
    return res


_HGRN_LEVELS = tuple(CHUNK >> (i + 1) for i in range(int(math.log2(CHUNK))))


def _hgrn_kernel(layer, hg_ref, lbl_ref, gw_ref, o_ref, s_ref):
    c = pl.program_id(1)

    @pl.when(c == 0)
    def _():
        s_ref[...] = jnp.zeros_like(s_ref)

    lg = lbl_ref[...]
    e = jnp.exp(lg - jnp.max(lg, axis=0, keepdims=True))
    sm = e / jnp.sum(e, axis=0, keepdims=True)
    lb = jnp.sum(sm[0:layer + 1, :], axis=0, keepdims=True) - sm[0:1, :]

    hg = hg_ref[0]
    q = _silu(hg[:, 0:D_HGRN])
    f_raw = hg[:, D_HGRN:2 * D_HGRN]
    i_in = hg[:, 2 * D_HGRN:3 * D_HGRN]
    z = hg[:, 3 * D_HGRN:4 * D_HGRN]

    log_lb = jnp.log(jnp.maximum(lb, LB_FLOOR))
    log_sig = jnp.minimum(f_raw, 0.0) - jnp.log1p(jnp.exp(-jnp.abs(f_raw)))
    yv = jnp.log1p(-lb) + log_sig
    log_f = jnp.maximum(log_lb, yv) + jnp.log1p(jnp.exp(-jnp.abs(log_lb - yv)))
    k = (1.0 - lb) * _sigmoid(-f_raw)

    trow = _iota2((CHUNK, CHUNK), 0)
    tcol = _iota2((CHUNK, CHUNK), 1)
    mats = [tcol <= trow]
    keep = []
    for m in _HGRN_LEVELS:
        mid = (trow // (2 * m)) * (2 * m) + m
        after = trow >= mid
        mats.append((after & (tcol >= mid) & (tcol <= trow))
                    | (~after & (tcol > trow) & (tcol < mid)))
        keep.append((after & (tcol < mid)
                     & (tcol // (2 * m) == trow // (2 * m))).astype(F32))
    cums = _split_dot(jnp.concatenate(mats, axis=0).astype(F32), log_f, 2)
    b = cums[0:CHUNK, :]
    b_last = b[CHUNK - 1:CHUNK, :]
    eye = (trow == tcol).astype(F32)

    q_in = q * jnp.exp(b)
    k_dec = k * jnp.exp(b_last - b)
    g_all = jnp.exp(b_last)

    heads = range(HGRN_HEADS)
    sls = [slice(h * LANES, (h + 1) * LANES) for h in heads]
    att = [eye * jnp.sum(q[:, sl] * k[:, sl], axis=-1, keepdims=True) for sl in sls]
    for li in range(len(_HGRN_LEVELS)):
        e = jnp.exp(cums[(li + 1) * CHUNK:(li + 2) * CHUNK, :])
        qe = _bf(q * e)
        ke = _bf(k * e)
        att = [att[h] + keep[li] * _bdot_nt(qe[:, sls[h]], ke[:, sls[h]]) for h in heads]
    s_old = [s_ref[h] for h in heads]
    i_b = _bf(i_in)
    o_h = [_bdot(att[h], i_b[:, sls[h]]) + _bdot_nt(q_in[:, sls[h]], s_old[h]) for h in heads]
    for h in heads:
        sl = sls[h]
        s_ref[h] = s_old[h] * g_all[:, sl] + _bdot_tn(i_b[:, sl], k_dec[:, sl])
    for h in heads:
        sl = sls[h]
        o = o_h[h]
        o = o * lax.rsqrt(jnp.mean(o * o, axis=-1, keepdims=True) + RMS_EPS)
        o_ref[0, :, sl] = o * gw_ref[:, sl] * _silu(z[:, sl])


def _hgrn_layer(proj_h, lb_logits, g_norm_w, layer):
    bsz, seq, _ = proj_h.shape
    nc = seq // CHUNK
    return pl.pallas_call(
        functools.partial(_hgrn_kernel, layer),
        out_shape=jax.ShapeDtypeStruct((bsz, seq, D_HGRN), F32),
        grid=(bsz, nc),
        in_specs=[pl.BlockSpec((1, CHUNK, HGRN_COLS), lambda b, c: (b, c, 0)),
                  pl.BlockSpec((DEPTH, D_HGRN), lambda b, c: (0, 0)),
                  pl.BlockSpec((1, D_HGRN), lambda b, c: (0, 0))],
        out_specs=pl.BlockSpec((1, CHUNK, D_HGRN), lambda b, c: (b, c, 0)),
        scratch_shapes=[pltpu.VMEM((HGRN_HEADS, HGRN_EXPAND, LANES), F32)],
        compiler_params=pltpu.CompilerParams(
            dimension_semantics=("arbitrary", "arbitrary"), vmem_limit_bytes=VMEM_LIMIT),
        name="hgrn2_chunk",
    )(proj_h, lb_logits, g_norm_w)


def kernel(x, w_in, shift_mu, w_decay0, w_decay_up, a0, a_up, k_k, k_a, r_k, ln_x_w, ln_x_b,
           v_mix0, v_mix_down, v_mix_up, lb_logits, g_norm_w, w_out, ln_w, ln_b):
    out_dtype = x.dtype
    bsz, seq, _ = x.shape
    h = x.astype(F32).reshape(bsz * seq, D_MODEL)
    lb_logits = lb_logits.astype(F32)
    row = lambda t: t.reshape(1, -1)
    v_first = None
    for l in range(DEPTH):
        proj_r = _in_proj(h, w_in, l, 0, RWKV_COLS, 512, RWKV_COLS // 3)
        proj_h = _in_proj(h, w_in, l, RWKV_COLS, HGRN_COLS, 512, HGRN_COLS // 4)
        proj_r = proj_r.reshape(bsz, seq, RWKV_COLS)
        proj_h = proj_h.reshape(bsz, seq, HGRN_COLS)
        params = (row(shift_mu[l]), row(w_decay0[l]), w_decay_up[l], row(a0[l]), a_up[l],
                  row(k_k[l]), row(k_a[l]), row(r_k[l]), row(ln_x_w[l]), row(ln_x_b[l]))
        vmix = None if l == 0 else (row(v_mix0[l - 1]), v_mix_down[l - 1], v_mix_up[l - 1])
        o_r, v_first = _rwkv_layer(proj_r, params, v_first, vmix)
        o_h = _hgrn_layer(proj_h, lb_logits, row(g_norm_w[l]), l)
        h = _out_proj_ln(o_r.reshape(bsz * seq, D_RWKV), o_h.reshape(bsz * seq, D_HGRN), h,
                         w_out, l, row(ln_w[l]), row(ln_b[l]), 256)
    return h.reshape(bsz, seq, D_MODEL).astype(out_dtype)
```

```python
import functools
import math

import jax
import jax.numpy as jnp
from jax import lax
from jax.experimental import pallas as pl
from jax.experimental.pallas import tpu as pltpu

D_MODEL = 2048
DEPTH = 2
D_RWKV = D_MODEL // 2
D_HGRN = D_MODEL - D_RWKV
RWKV_HEAD = 64
DECAY_RANK = 64
A_RANK = 64
VRES_RANK = 32
HGRN_EXPAND = 128
HGRN_HEADS = D_HGRN // HGRN_EXPAND
RWKV_COLS = 4 * D_RWKV + DECAY_RANK + A_RANK
HGRN_COLS = 4 * D_HGRN
IN_COLS = RWKV_COLS + HGRN_COLS
ALPHA = (2 * DEPTH) ** 0.25
LN_EPS = 1e-5
GN_EPS = 64e-5
RMS_EPS = 1e-5
LB_FLOOR = 1e-30

LANES = 128
CHUNK = 64
PAIR_ROWS = 2 * CHUNK
RWKV_PAIRS = D_RWKV // LANES
VMEM_LIMIT = 56 * 1024 * 1024
IN_TM, IN_TN = 512, IN_COLS // 5
OUT_TM = 256

F32 = jnp.float32


def _bf(x):
    return x.astype(jnp.bfloat16)


def _bdot(a, b):
    return jnp.dot(_bf(a), _bf(b), preferred_element_type=F32)


def _bdot_nt(a, b):
    return lax.dot_general(_bf(a), _bf(b), (((1,), (1,)), ((), ())), preferred_element_type=F32)


def _bdot_tn(a, b):
    return lax.dot_general(_bf(a), _bf(b), (((0,), (0,)), ((), ())), preferred_element_type=F32)


def _split_dot(m01, x, terms):
    mb = _bf(m01)
    acc = None
    rem = x
    for t in range(terms):
        piece = _bf(rem)
        part = jnp.dot(mb, piece, preferred_element_type=F32)
        acc = part if acc is None else acc + part
        if t + 1 < terms:
            rem = rem - piece.astype(F32)
    return acc


def _sigmoid(x):
    return 1.0 / (1.0 + jnp.exp(-x))


def _silu(x):
    return x * _sigmoid(x)


def _iota2(shape, dim):
    return lax.broadcasted_iota(jnp.int32, shape, dim)


def _mm_kernel(x_ref, w_ref, o_ref, wb_ref):
    @pl.when(pl.program_id(1) == 0)
    def _():
        wb_ref[...] = w_ref[...].astype(jnp.bfloat16)

    o_ref[...] = jnp.dot(x_ref[...].astype(jnp.bfloat16), wb_ref[...],
                         preferred_element_type=F32)


def _in_proj(x, w_in, layer):
    m, k = x.shape
    n = w_in.shape[-1]
    tm, tn = IN_TM, IN_TN
    return pl.pallas_call(
        _mm_kernel,
        out_shape=jax.ShapeDtypeStruct((m, n), F32),
        grid=(n // tn, m // tm),
        in_specs=[pl.BlockSpec((tm, k), lambda j, i: (i, 0)),
                  pl.BlockSpec((None, k, tn), lambda j, i: (layer, 0, j))],
        out_specs=pl.BlockSpec((tm, tn), lambda j, i: (i, j)),
        scratch_shapes=[pltpu.VMEM((k, tn), jnp.bfloat16)],
        compiler_params=pltpu.CompilerParams(
            dimension_semantics=("arbitrary", "arbitrary"), vmem_limit_bytes=VMEM_LIMIT),
        name="in_proj",
    )(x, w_in)


def _out_kernel(mix_ref, h_ref, w_ref, lnw_ref, lnb_ref, o_ref, wb_ref):
    @pl.when(pl.program_id(0) == 0)
    def _():
        wb_ref[...] = w_ref[...].astype(jnp.bfloat16)

    y = jnp.dot(mix_ref[...].astype(jnp.bfloat16), wb_ref[...], preferred_element_type=F32)
    u = ALPHA * h_ref[...] + y
    mu = jnp.mean(u, axis=-1, keepdims=True)
    d = u - mu
    var = jnp.mean(d * d, axis=-1, keepdims=True)
    o_ref[...] = d * lax.rsqrt(var + LN_EPS) * lnw_ref[...] + lnb_ref[...]


def _out_proj_ln(o_mix, h, w_out, layer, lnw, lnb):
    m = h.shape[0]
    tm = OUT_TM
    return pl.pallas_call(
        _out_kernel,
        out_shape=jax.ShapeDtypeStruct((m, D_MODEL), F32),
        grid=(m // tm,),
        in_specs=[pl.BlockSpec((tm, D_MODEL), lambda i: (i, 0)),
                  pl.BlockSpec((tm, D_MODEL), lambda i: (i, 0)),
                  pl.BlockSpec((None, D_MODEL, D_MODEL), lambda i: (layer, 0, 0),
                               pipeline_mode=pl.Buffered(1)),
                  pl.BlockSpec((1, D_MODEL), lambda i: (0, 0)),
                  pl.BlockSpec((1, D_MODEL), lambda i: (0, 0))],
        out_specs=pl.BlockSpec((tm, D_MODEL), lambda i: (i, 0)),
        scratch_shapes=[pltpu.VMEM((D_MODEL, D_MODEL), jnp.bfloat16)],
        compiler_params=pltpu.CompilerParams(
            dimension_semantics=("arbitrary",), vmem_limit_bytes=VMEM_LIMIT),
        name="out_proj_ln",
    )(o_mix, h, w_out, lnw, lnb)


def _rwkv_stages(has_vmix, y, prm, vf_ref, vf_out_ref, o_ref, s_ref, prev_ref):
    (mu_ref, w0_ref, wup_ref, a0_ref, aup_ref, kk_ref, ka_ref, rk_ref, gnw_ref, gnb_ref,
     v0_ref, vdn_ref, vup_ref) = prm
    row = _iota2(y.shape, 0)
    y_prev = jnp.where(row == 0, prev_ref[...], pltpu.roll(y, shift=1, axis=0))
    prev_ref[...] = y[CHUNK - 1:CHUNK, :]
    xs = y + mu_ref[...] * (y_prev - y)

    r = xs[:, 0:D_RWKV]
    k = xs[:, D_RWKV:2 * D_RWKV]
    v = xs[:, 2 * D_RWKV:3 * D_RWKV]
    z = xs[:, 3 * D_RWKV:4 * D_RWKV]
    wd = xs[:, 4 * D_RWKV:4 * D_RWKV + DECAY_RANK]
    ad = xs[:, 4 * D_RWKV + DECAY_RANK:RWKV_COLS]

    w_raw = w0_ref[...] + _bdot(jnp.tanh(wd), wup_ref[...])
    logw = (-math.exp(-0.5)) * _sigmoid(w_raw)
    a = _sigmoid(a0_ref[...] + _bdot(ad, aup_ref[...]))
    if has_vmix:
        gate = _sigmoid(v0_ref[...] + _bdot(_bdot(v, vdn_ref[...]), vup_ref[...]))
        v = v + (vf_ref[0] - v) * gate
    else:
        vf_out_ref[0] = v
    kk_raw = k * kk_ref[...]
    k = k * (1.0 + (a - 1.0) * ka_ref[...])

    tri = (_iota2((CHUNK, CHUNK), 1) <= _iota2((CHUNK, CHUNK), 0)).astype(F32)
    cl = _split_dot(tri, logw, 3)
    cl_last = cl[CHUNK - 1:CHUNK, :]
    g_inc = jnp.exp(cl)
    g_exc = jnp.exp(cl - logw)
    g_inv = jnp.exp(-cl)
    g_end = jnp.exp(cl_last - cl)
    g_all = jnp.exp(cl_last)

    lane = _iota2((CHUNK, LANES), 1)
    m_lo = (lane < RWKV_HEAD).astype(F32)
    m_hi = 1.0 - m_lo
    head_mask = jnp.concatenate([m_lo, m_hi], axis=0)
    prow = _iota2((PAIR_ROWS, PAIR_ROWS), 0)
    pcol = _iota2((PAIR_ROWS, PAIR_ROWS), 1)
    same_head = (prow // CHUNK) == (pcol // CHUNK)
    t_idx = prow % CHUNK
    s_idx = pcol % CHUNK
    strict = (same_head & (s_idx < t_idx)).astype(F32)
    incl = (same_head & (s_idx <= t_idx)).astype(F32)
    eye = (prow == pcol).astype(F32)

    pairs = range(RWKV_PAIRS)
    sls = [slice(p * LANES, (p + 1) * LANES) for p in pairs]

    def tile(x, p):
        xp = x[:, sls[p]]
        return jnp.concatenate([xp, xp], axis=0)

    def st(x, p):
        return tile(x, p) * head_mask

    r_s, k_s, v_s, b_h, k_h, ar, bk = [], [], [], [], [], [], []
    for p in pairs:
        kk_s = st(kk_raw, p)
        nrm = jnp.sqrt(jnp.sum(kk_s * kk_s, axis=-1, keepdims=True))
        kk_s = kk_s / jnp.maximum(nrm, 1e-12)
        b_s = kk_s * tile(a, p)
        k_s.append(st(k, p))
        v_s.append(st(v, p))
        r_s.append(tile(r, p))
        ginv = tile(g_inv, p)
        gend = tile(g_end, p)
        b_h.append(b_s * gend)
        k_h.append(k_s[p] * gend)
        a_t = -kk_s * tile(g_exc, p)
        r_t = r_s[p] * tile(g_inc, p)
        ar.append(_bf(jnp.concatenate([a_t, r_t], axis=0)))
        bk.append(_bf(jnp.concatenate([b_s * ginv, k_s[p] * ginv], axis=0)))

    sc = [_bdot_nt(ar[p], bk[p]) for p in pairs]
    pw = [strict * sc[p][:PAIR_ROWS, :PAIR_ROWS] for p in pairs]
    tinv = [eye + pw[p] for p in pairs]
    yield
    for _ in range(int(math.log2(CHUNK)) - 1):
        pwb = [_bf(x) for x in pw]
        pw = [_bdot(x, x) for x in pwb]
        tinv = [tinv[p] + _bdot(tinv[p], pw[p]) for p in pairs]
        yield
    a_kv = [_bf(jnp.concatenate([strict * sc[p][:PAIR_ROWS, PAIR_ROWS:],
                                 incl * sc[p][PAIR_ROWS:, PAIR_ROWS:]], axis=0)) for p in pairs]
    a_rb = [_bf(incl * sc[p][PAIR_ROWS:, :PAIR_ROWS]) for p in pairs]

    s_old = [s_ref[p] for p in pairs]
    ars = [_bdot_nt(ar[p], s_old[p]) for p in pairs]
    akv = [_bdot(a_kv[p], v_s[p]) for p in pairs]
    u = [head_mask * _bdot(tinv[p], ars[p][:PAIR_ROWS] + akv[p][:PAIR_ROWS]) for p in pairs]
    for p in pairs:
        s_ref[p] = s_old[p] * g_all[:, sls[p]] + _bdot_tn(
            jnp.concatenate([u[p], v_s[p]], axis=0), jnp.concatenate([b_h[p], k_h[p]], axis=0))
    o_s = [ars[p][PAIR_ROWS:] + _bdot(a_rb[p], u[p]) + akv[p][PAIR_ROWS:] for p in pairs]
    yield

    for p in pairs:
        sl = sls[p]
        o = o_s[p] * head_mask
        mean = jnp.sum(o, axis=-1, keepdims=True) * (1.0 / RWKV_HEAD)
        d = (o - mean) * head_mask
        var = jnp.sum(d * d, axis=-1, keepdims=True) * (1.0 / RWKV_HEAD)
        on = (d * lax.rsqrt(var + GN_EPS) * gnw_ref[:, sl] + gnb_ref[:, sl]) * head_mask
        bonus = jnp.sum(r_s[p] * k_s[p] * rk_ref[:, sl], axis=-1, keepdims=True)
        out_s = on + bonus * v_s[p]
        out = out_s[:CHUNK, :] + out_s[CHUNK:, :]
        o_ref[0, :, sl] = out * _silu(z[:, sl])
    yield


_HGRN_LEVELS = tuple(CHUNK >> (i + 1) for i in range(int(math.log2(CHUNK))))


def _hgrn_stages(layer, hg, lbl_ref, gw_ref, o_ref, s_ref):
    lg = lbl_ref[...]
    e = jnp.exp(lg - jnp.max(lg, axis=0, keepdims=True))
    sm = e / jnp.sum(e, axis=0, keepdims=True)
    lb = jnp.sum(sm[0:layer + 1, :], axis=0, keepdims=True) - sm[0:1, :]

    q = _silu(hg[:, 0:D_HGRN])
    f_raw = hg[:, D_HGRN:2 * D_HGRN]
    i_in = hg[:, 2 * D_HGRN:3 * D_HGRN]
    z = hg[:, 3 * D_HGRN:4 * D_HGRN]

    log_lb = jnp.log(jnp.maximum(lb, LB_FLOOR))
    log_sig = jnp.minimum(f_raw, 0.0) - jnp.log1p(jnp.exp(-jnp.abs(f_raw)))
    yv = jnp.log1p(-lb) + log_sig
    log_f = jnp.maximum(log_lb, yv) + jnp.log1p(jnp.exp(-jnp.abs(log_lb - yv)))
    k = (1.0 - lb) * _sigmoid(-f_raw)

    trow = _iota2((CHUNK, CHUNK), 0)
    tcol = _iota2((CHUNK, CHUNK), 1)
    mats = [tcol <= trow]
    keep = []
    for m in _HGRN_LEVELS:
        mid = (trow // (2 * m)) * (2 * m) + m
        after = trow >= mid
        mats.append((after & (tcol >= mid) & (tcol <= trow))
                    | (~after & (tcol > trow) & (tcol < mid)))
        keep.append((after & (tcol < mid)
                     & (tcol // (2 * m) == trow // (2 * m))).astype(F32))
    cums = _split_dot(jnp.concatenate(mats, axis=0).astype(F32), log_f, 2)
    b = cums[0:CHUNK, :]
    b_last = b[CHUNK - 1:CHUNK, :]
    eye = (trow == tcol).astype(F32)

    q_in = q * jnp.exp(b)
    k_dec = k * jnp.exp(b_last - b)
    g_all = jnp.exp(b_last)

    heads = range(HGRN_HEADS)
    sls = [slice(h * LANES, (h + 1) * LANES) for h in heads]
    att = [eye * jnp.sum(q[:, sl] * k[:, sl], axis=-1, keepdims=True) for sl in sls]
    yield
    for li in range(len(_HGRN_LEVELS)):
        e = jnp.exp(cums[(li + 1) * CHUNK:(li + 2) * CHUNK, :])
        qe = _bf(q * e)
        ke = _bf(k * e)
        att = [att[h] + keep[li] * _bdot_nt(qe[:, sls[h]], ke[:, sls[h]]) for h in heads]
        yield
    s_old = [s_ref[h] for h in heads]
    i_b = _bf(i_in)
    o_h = [_bdot(att[h], i_b[:, sls[h]]) + _bdot_nt(q_in[:, sls[h]], s_old[h]) for h in heads]
    for h in heads:
        sl = sls[h]
        s_ref[h] = s_old[h] * g_all[:, sl] + _bdot_tn(i_b[:, sl], k_dec[:, sl])
    yield
    for h in heads:
        sl = sls[h]
        o = o_h[h]
        o = o * lax.rsqrt(jnp.mean(o * o, axis=-1, keepdims=True) + RMS_EPS)
        o_ref[0, :, D_RWKV + h * LANES:D_RWKV + (h + 1) * LANES] = (
            o * gw_ref[:, sl] * _silu(z[:, sl]))
    yield


_STAGE_ORDER = (0, 1, 0, 1, 0, 1, 0, 1, 0, 1, 0, 1, 1, 0, 1, 0, 1)


def _mixer_kernel(layer, *refs):
    has_vmix = layer > 0
    n_prm = 13 if has_vmix else 10
    proj_ref = refs[0]
    prm = list(refs[1:1 + n_prm]) + [None] * (13 - n_prm)
    pos = 1 + n_prm
    vf_ref = vf_out_ref = None
    if has_vmix:
        vf_ref = refs[pos]
        pos += 1
    lbl_ref, gw_ref, o_ref = refs[pos:pos + 3]
    pos += 3
    if not has_vmix:
        vf_out_ref = refs[pos]
        pos += 1
    s_r_ref, prev_ref, s_h_ref = refs[pos:pos + 3]

    @pl.when(pl.program_id(1) == 0)
    def _():
        s_r_ref[...] = jnp.zeros_like(s_r_ref)
        prev_ref[...] = jnp.zeros_like(prev_ref)
        s_h_ref[...] = jnp.zeros_like(s_h_ref)

    gens = (_rwkv_stages(has_vmix, proj_ref[0, :, 0:RWKV_COLS], prm, vf_ref, vf_out_ref, o_ref,
                         s_r_ref, prev_ref),
            _hgrn_stages(layer, proj_ref[0, :, RWKV_COLS:IN_COLS], lbl_ref, gw_ref, o_ref,
                         s_h_ref))
    for g in _STAGE_ORDER:
        next(gens[g])


def _mixer_layer(proj, layer, rwkv_params, vmix, v_first, lb_logits, g_norm_w):
    bsz, seq, _ = proj.shape
    has_vmix = layer > 0
    row_spec = lambda n: pl.BlockSpec((1, n), lambda b, c: (0, 0))
    full_spec = lambda s: pl.BlockSpec(s, lambda b, c: (0, 0))
    seq_spec = lambda n: pl.BlockSpec((1, CHUNK, n), lambda b, c: (b, c, 0))
    args = [proj] + list(rwkv_params)
    in_specs = [seq_spec(IN_COLS), row_spec(RWKV_COLS), row_spec(D_RWKV),
                full_spec((DECAY_RANK, D_RWKV)), row_spec(D_RWKV), full_spec((A_RANK, D_RWKV)),
                row_spec(D_RWKV), row_spec(D_RWKV), row_spec(D_RWKV), row_spec(D_RWKV),
                row_spec(D_RWKV)]
    out_sds = jax.ShapeDtypeStruct((bsz, seq, D_MODEL), F32)
    if has_vmix:
        args += list(vmix) + [v_first]
        in_specs += [row_spec(D_RWKV), full_spec((D_RWKV, VRES_RANK)),
                     full_spec((VRES_RANK, D_RWKV)), seq_spec(D_RWKV)]
        out_shape, out_specs = out_sds, seq_spec(D_MODEL)
    else:
        out_shape = (out_sds, jax.ShapeDtypeStruct((bsz, seq, D_RWKV), F32))
        out_specs = (seq_spec(D_MODEL), seq_spec(D_RWKV))
    args += [lb_logits, g_norm_w]
    in_specs += [full_spec((DEPTH, D_HGRN)), row_spec(D_HGRN)]
    res = pl.pallas_call(
        functools.partial(_mixer_kernel, layer),
        out_shape=out_shape,
        grid=(bsz, seq // CHUNK),
        in_specs=in_specs,
        out_specs=out_specs,
        scratch_shapes=[pltpu.VMEM((RWKV_PAIRS, PAIR_ROWS, LANES), F32),
                        pltpu.VMEM((1, RWKV_COLS), F32),
                        pltpu.VMEM((HGRN_HEADS, HGRN_EXPAND, LANES), F32)],
        compiler_params=pltpu.CompilerParams(
            dimension_semantics=("arbitrary", "arbitrary"), vmem_limit_bytes=VMEM_LIMIT),
        name="mixer",
    )(*args)
    if has_vmix:
        return res, v_first
    return res


def kernel(x, w_in, shift_mu, w_decay0, w_decay_up, a0, a_up, k_k, k_a, r_k, ln_x_w, ln_x_b,
           v_mix0, v_mix_down, v_mix_up, lb_logits, g_norm_w, w_out, ln_w, ln_b):
    out_dtype = x.dtype
    bsz, seq, _ = x.shape
    h = x.astype(F32).reshape(bsz * seq, D_MODEL)
    lb_logits = lb_logits.astype(F32)
    row = lambda t: t.reshape(1, -1)
    v_first = None
    for l in range(DEPTH):
        proj = _in_proj(h, w_in, l).reshape(bsz, seq, IN_COLS)
        params = (row(shift_mu[l]), row(w_decay0[l]), w_decay_up[l], row(a0[l]), a_up[l],
                  row(k_k[l]), row(k_a[l]), row(r_k[l]), row(ln_x_w[l]), row(ln_x_b[l]))
        vmix = None if l == 0 else (row(v_mix0[l - 1]), v_mix_down[l - 1], v_mix_up[l - 1])
        o_mix, v_first = _mixer_layer(proj, l, params, vmix, v_first, lb_logits, row(g_norm_w[l]))
        h = _out_proj_ln(o_mix.reshape(bsz * seq, D_MODEL), h, w_out, l, row(ln_w[l]),
                         row(ln_b[l]))
    return h.reshape(bsz, seq, D_MODEL).astype(out_dtype)
```

```python
import functools
import math

import jax
import jax.numpy as jnp
from jax import lax
from jax.experimental import pallas as pl
from jax.experimental.pallas import tpu as pltpu

D_MODEL = 2048
DEPTH = 2
D_RWKV = D_MODEL // 2
D_HGRN = D_MODEL - D_RWKV
RWKV_HEAD = 64
DECAY_RANK = 64
A_RANK = 64
VRES_RANK = 32
HGRN_EXPAND = 128
HGRN_HEADS = D_HGRN // HGRN_EXPAND
RWKV_COLS = 4 * D_RWKV + DECAY_RANK + A_RANK
HGRN_COLS = 4 * D_HGRN
IN_COLS = RWKV_COLS + HGRN_COLS
ALPHA = (2 * DEPTH) ** 0.25
LN_EPS = 1e-5
GN_EPS = 64e-5
RMS_EPS = 1e-5
LB_FLOOR = 1e-30

LANES = 128
CHUNK = 64
PAIR_ROWS = 2 * CHUNK
RWKV_PAIRS = D_RWKV // LANES
VMEM_LIMIT = 56 * 1024 * 1024
IN_TM, IN_TN = 512, IN_COLS // 5
OUT_TM = 256

F32 = jnp.float32


def _bf(x):
    return x.astype(jnp.bfloat16)


def _bdot(a, b):
    return jnp.dot(_bf(a), _bf(b), preferred_element_type=F32)


def _bdot_nt(a, b):
    return lax.dot_general(_bf(a), _bf(b), (((1,), (1,)), ((), ())), preferred_element_type=F32)


def _bdot_tn(a, b):
    return lax.dot_general(_bf(a), _bf(b), (((0,), (0,)), ((), ())), preferred_element_type=F32)


def _split_dot(m01, x, terms):
    mb = _bf(m01)
    acc = None
    rem = x
    for t in range(terms):
        piece = _bf(rem)
        part = jnp.dot(mb, piece, preferred_element_type=F32)
        acc = part if acc is None else acc + part
        if t + 1 < terms:
            rem = rem - piece.astype(F32)
    return acc


def _sigmoid(x):
    return 1.0 / (1.0 + jnp.exp(-x))


def _silu(x):
    return x * _sigmoid(x)


def _iota2(shape, dim):
    return lax.broadcasted_iota(jnp.int32, shape, dim)


def _mm_kernel(x_ref, w_ref, o_ref, wb_ref):
    @pl.when(pl.program_id(1) == 0)
    def _():
        wb_ref[...] = w_ref[...].astype(jnp.bfloat16)

    o_ref[...] = jnp.dot(x_ref[...].astype(jnp.bfloat16), wb_ref[...],
                         preferred_element_type=F32)


def _in_proj(x, w_in, layer):
    m, k = x.shape
    n = w_in.shape[-1]
    tm, tn = IN_TM, IN_TN
    return pl.pallas_call(
        _mm_kernel,
        out_shape=jax.ShapeDtypeStruct((m, n), F32),
        grid=(n // tn, m // tm),
        in_specs=[pl.BlockSpec((tm, k), lambda j, i: (i, 0)),
                  pl.BlockSpec((None, k, tn), lambda j, i: (layer, 0, j))],
        out_specs=pl.BlockSpec((tm, tn), lambda j, i: (i, j)),
        scratch_shapes=[pltpu.VMEM((k, tn), jnp.bfloat16)],
        compiler_params=pltpu.CompilerParams(
            dimension_semantics=("arbitrary", "arbitrary"), vmem_limit_bytes=VMEM_LIMIT),
        name="in_proj",
    )(x, w_in)


def _out_kernel(mix_ref, h_ref, w_ref, lnw_ref, lnb_ref, o_ref, wb_ref):
    @pl.when(pl.program_id(0) == 0)
    def _():
        wb_ref[...] = w_ref[...].astype(jnp.bfloat16)

    y = jnp.dot(mix_ref[...].astype(jnp.bfloat16), wb_ref[...], preferred_element_type=F32)
    u = ALPHA * h_ref[...] + y
    mu = jnp.mean(u, axis=-1, keepdims=True)
    d = u - mu
    var = jnp.mean(d * d, axis=-1, keepdims=True)
    o_ref[...] = d * lax.rsqrt(var + LN_EPS) * lnw_ref[...] + lnb_ref[...]


def _out_proj_ln(o_mix, h, w_out, layer, lnw, lnb):
    m = h.shape[0]
    tm = OUT_TM
    return pl.pallas_call(
        _out_kernel,
        out_shape=jax.ShapeDtypeStruct((m, D_MODEL), F32),
        grid=(m // tm,),
        in_specs=[pl.BlockSpec((tm, D_MODEL), lambda i: (i, 0)),
                  pl.BlockSpec((tm, D_MODEL), lambda i: (i, 0)),
                  pl.BlockSpec((None, D_MODEL, D_MODEL), lambda i: (layer, 0, 0),
                               pipeline_mode=pl.Buffered(1)),
                  pl.BlockSpec((1, D_MODEL), lambda i: (0, 0)),
                  pl.BlockSpec((1, D_MODEL), lambda i: (0, 0))],
        out_specs=pl.BlockSpec((tm, D_MODEL), lambda i: (i, 0)),
        scratch_shapes=[pltpu.VMEM((D_MODEL, D_MODEL), jnp.bfloat16)],
        compiler_params=pltpu.CompilerParams(
            dimension_semantics=("arbitrary",), vmem_limit_bytes=VMEM_LIMIT),
        name="out_proj_ln",
    )(o_mix, h, w_out, lnw, lnb)


_HGRN_LEVELS = tuple(CHUNK >> (i + 1) for i in range(int(math.log2(CHUNK))))
_N_LEVELS = len(_HGRN_LEVELS)
_PAIR_SL = [slice(p * LANES, (p + 1) * LANES) for p in range(RWKV_PAIRS)]
_HEAD_SL = [slice(h * LANES, (h + 1) * LANES) for h in range(HGRN_HEADS)]

_BUF_SHAPES = (
    ("ar", (RWKV_PAIRS, 2 * PAIR_ROWS, LANES), jnp.bfloat16),
    ("bk", (RWKV_PAIRS, 2 * PAIR_ROWS, LANES), jnp.bfloat16),
    ("vs", (RWKV_PAIRS, PAIR_ROWS, LANES), jnp.bfloat16),
    ("bhkh", (RWKV_PAIRS, 2 * PAIR_ROWS, LANES), jnp.bfloat16),
    ("gall_r", (1, D_RWKV), F32),
    ("bonus", (CHUNK, D_RWKV), F32),
    ("gate_r", (CHUNK, D_RWKV), F32),
    ("vnat", (CHUNK, D_RWKV), F32),
    ("qe", (_N_LEVELS, CHUNK, D_HGRN), jnp.bfloat16),
    ("ke", (_N_LEVELS, CHUNK, D_HGRN), jnp.bfloat16),
    ("qin", (CHUNK, D_HGRN), jnp.bfloat16),
    ("kdec", (CHUNK, D_HGRN), jnp.bfloat16),
    ("ib", (CHUNK, D_HGRN), jnp.bfloat16),
    ("odiag", (CHUNK, D_HGRN), F32),
    ("gate_h", (CHUNK, D_HGRN), F32),
    ("gall_h", (1, D_HGRN), F32),
)


def _head_mask():
    lane = _iota2((PAIR_ROWS, LANES), 1)
    row = _iota2((PAIR_ROWS, LANES), 0)
    return ((lane < RWKV_HEAD) == (row < CHUNK)).astype(F32)


def _rwkv_prepare(first, has_vmix, y, vf, prm, prev_ref, buf):
    (mu_ref, w0_ref, wup_ref, a0_ref, aup_ref, kk_ref, ka_ref, rk_ref, v0_ref, vdn_ref,
     vup_ref) = prm
    row = _iota2(y.shape, 0)
    prev = jnp.zeros_like(y[0:1, :]) if first else prev_ref[...]
    y_prev = jnp.where(row == 0, prev, pltpu.roll(y, shift=1, axis=0))
    prev_ref[...] = y[CHUNK - 1:CHUNK, :]
    xs = y + mu_ref[...] * (y_prev - y)

    r = xs[:, 0:D_RWKV]
    k = xs[:, D_RWKV:2 * D_RWKV]
    v = xs[:, 2 * D_RWKV:3 * D_RWKV]
    z = xs[:, 3 * D_RWKV:4 * D_RWKV]
    wd = xs[:, 4 * D_RWKV:4 * D_RWKV + DECAY_RANK]
    ad = xs[:, 4 * D_RWKV + DECAY_RANK:RWKV_COLS]

    w_raw = w0_ref[...] + _bdot(jnp.tanh(wd), wup_ref[...])
    logw = (-math.exp(-0.5)) * _sigmoid(w_raw)
    a = _sigmoid(a0_ref[...] + _bdot(ad, aup_ref[...]))
    if has_vmix:
        gate = _sigmoid(v0_ref[...] + _bdot(_bdot(v, vdn_ref[...]), vup_ref[...]))
        v = v + (vf - v) * gate
    else:
        buf["vnat"][...] = v
    kk_raw = k * kk_ref[...]
    k = k * (1.0 + (a - 1.0) * ka_ref[...])
    buf["gate_r"][...] = _silu(z)
    yield

    tri = (_iota2((CHUNK, CHUNK), 1) <= _iota2((CHUNK, CHUNK), 0)).astype(F32)
    cl = _split_dot(tri, logw, 3)
    cl_last = cl[CHUNK - 1:CHUNK, :]
    g_inc = jnp.exp(cl)
    g_exc = jnp.exp(cl - logw)
    g_inv = jnp.exp(-cl)
    g_end = jnp.exp(cl_last - cl)
    buf["gall_r"][...] = jnp.exp(cl_last)
    yield

    head_mask = _head_mask()

    def tile(x, p):
        xp = x[:, _PAIR_SL[p]]
        return jnp.concatenate([xp, xp], axis=0)

    for p in range(RWKV_PAIRS):
        sl = _PAIR_SL[p]
        kk_s = tile(kk_raw, p) * head_mask
        nrm = jnp.sqrt(jnp.sum(kk_s * kk_s, axis=-1, keepdims=True))
        kk_s = kk_s / jnp.maximum(nrm, 1e-12)
        b_s = kk_s * tile(a, p)
        k_s = tile(k, p) * head_mask
        v_s = tile(v, p) * head_mask
        r_s = tile(r, p)
        ginv = tile(g_inv, p)
        gend = tile(g_end, p)
        a_t = -kk_s * tile(g_exc, p)
        r_t = r_s * tile(g_inc, p)
        buf["ar"][p] = _bf(jnp.concatenate([a_t, r_t], axis=0))
        buf["bk"][p] = _bf(jnp.concatenate([b_s * ginv, k_s * ginv], axis=0))
        buf["bhkh"][p] = _bf(jnp.concatenate([b_s * gend, k_s * gend], axis=0))
        buf["vs"][p] = _bf(v_s)
        bonus = jnp.sum(r_s * k_s * rk_ref[:, sl], axis=-1, keepdims=True) * v_s
        buf["bonus"][:, sl] = bonus[:CHUNK, :] + bonus[CHUNK:, :]
        yield


def _rwkv_chain(half, emit_v, buf, gnw_ref, gnb_ref, o_ref, vf_out_ref, s_ref):
    head_mask = _head_mask()
    prow = _iota2((PAIR_ROWS, PAIR_ROWS), 0)
    pcol = _iota2((PAIR_ROWS, PAIR_ROWS), 1)
    same_head = (prow // CHUNK) == (pcol // CHUNK)
    t_idx = prow % CHUNK
    s_idx = pcol % CHUNK
    strict = (same_head & (s_idx < t_idx)).astype(F32)
    incl = (same_head & (s_idx <= t_idx)).astype(F32)
    eye = (prow == pcol).astype(F32)
    rows = slice(half * CHUNK, (half + 1) * CHUNK)

    pairs = range(RWKV_PAIRS)
    ar = [buf["ar"][p] for p in pairs]
    sc = [_bdot_nt(ar[p], buf["bk"][p]) for p in pairs]
    pw = [strict * sc[p][:PAIR_ROWS, :PAIR_ROWS] for p in pairs]
    tinv = [eye + pw[p] for p in pairs]
    yield
    for _ in range(int(math.log2(CHUNK)) - 1):
        pwb = [_bf(x) for x in pw]
        pw = [_bdot(x, x) for x in pwb]
        tinv = [tinv[p] + _bdot(tinv[p], pw[p]) for p in pairs]
        yield
    a_kv = [_bf(jnp.concatenate([strict * sc[p][:PAIR_ROWS, PAIR_ROWS:],
                                 incl * sc[p][PAIR_ROWS:, PAIR_ROWS:]], axis=0)) for p in pairs]
    a_rb = [_bf(incl * sc[p][PAIR_ROWS:, :PAIR_ROWS]) for p in pairs]

    s_old = [s_ref[p] for p in pairs]
    v_s = [buf["vs"][p] for p in pairs]
    ars = [_bdot_nt(ar[p], s_old[p]) for p in pairs]
    akv = [_bdot(a_kv[p], v_s[p]) for p in pairs]
    u = [_bf(head_mask * _bdot(tinv[p], ars[p][:PAIR_ROWS] + akv[p][:PAIR_ROWS])) for p in pairs]
    for p in pairs:
        s_ref[p] = s_old[p] * buf["gall_r"][:, _PAIR_SL[p]] + _bdot_tn(
            jnp.concatenate([u[p], v_s[p]], axis=0), buf["bhkh"][p])
    o_s = [ars[p][PAIR_ROWS:] + _bdot(a_rb[p], u[p]) + akv[p][PAIR_ROWS:] for p in pairs]
    yield

    for p in pairs:
        sl = _PAIR_SL[p]
        o = o_s[p] * head_mask
        mean = jnp.sum(o, axis=-1, keepdims=True) * (1.0 / RWKV_HEAD)
        d = (o - mean) * head_mask
        var = jnp.sum(d * d, axis=-1, keepdims=True) * (1.0 / RWKV_HEAD)
        on = (d * lax.rsqrt(var + GN_EPS) * gnw_ref[:, sl] + gnb_ref[:, sl]) * head_mask
        out = on[:CHUNK, :] + on[CHUNK:, :] + buf["bonus"][:, sl]
        o_ref[0, rows, sl] = out * buf["gate_r"][:, sl]
    if emit_v:
        vf_out_ref[0, rows, :] = buf["vnat"][...]
    yield


def _hgrn_prepare(layer, hg, lbl_ref, gw_ref, buf):
    lg = lbl_ref[...]
    e = jnp.exp(lg - jnp.max(lg, axis=0, keepdims=True))
    sm = e / jnp.sum(e, axis=0, keepdims=True)
    lb = jnp.sum(sm[0:layer + 1, :], axis=0, keepdims=True) - sm[0:1, :]

    q = _silu(hg[:, 0:D_HGRN])
    f_raw = hg[:, D_HGRN:2 * D_HGRN]
    i_in = hg[:, 2 * D_HGRN:3 * D_HGRN]
    z = hg[:, 3 * D_HGRN:4 * D_HGRN]
    buf["gate_h"][...] = gw_ref[...] * _silu(z)
    buf["ib"][...] = _bf(i_in)

    log_lb = jnp.log(jnp.maximum(lb, LB_FLOOR))
    log_sig = jnp.minimum(f_raw, 0.0) - jnp.log1p(jnp.exp(-jnp.abs(f_raw)))
    yv = jnp.log1p(-lb) + log_sig
    log_f = jnp.maximum(log_lb, yv) + jnp.log1p(jnp.exp(-jnp.abs(log_lb - yv)))
    k = (1.0 - lb) * _sigmoid(-f_raw)
    yield

    trow = _iota2((CHUNK, CHUNK), 0)
    tcol = _iota2((CHUNK, CHUNK), 1)
    mats = [tcol <= trow]
    for m in _HGRN_LEVELS:
        mid = (trow // (2 * m)) * (2 * m) + m
        after = trow >= mid
        mats.append((after & (tcol >= mid) & (tcol <= trow))
                    | (~after & (tcol > trow) & (tcol < mid)))
    cums = _split_dot(jnp.concatenate(mats, axis=0).astype(F32), log_f, 2)
    b = cums[0:CHUNK, :]
    b_last = b[CHUNK - 1:CHUNK, :]
    buf["qin"][...] = _bf(q * jnp.exp(b))
    buf["kdec"][...] = _bf(k * jnp.exp(b_last - b))
    buf["gall_h"][...] = jnp.exp(b_last)
    yield
    for li in range(_N_LEVELS):
        e = jnp.exp(cums[(li + 1) * CHUNK:(li + 2) * CHUNK, :])
        buf["qe"][li] = _bf(q * e)
        buf["ke"][li] = _bf(k * e)
        yield
    qk = q * k
    for h in range(HGRN_HEADS):
        sl = _HEAD_SL[h]
        buf["odiag"][:, sl] = jnp.sum(qk[:, sl], axis=-1, keepdims=True) * i_in[:, sl]
    yield


def _hgrn_chain(half, buf, o_ref, s_ref):
    trow = _iota2((CHUNK, CHUNK), 0)
    tcol = _iota2((CHUNK, CHUNK), 1)
    rows = slice(half * CHUNK, (half + 1) * CHUNK)
    heads = range(HGRN_HEADS)
    att = [None] * HGRN_HEADS
    yield
    for li, m in enumerate(_HGRN_LEVELS):
        mid = (trow // (2 * m)) * (2 * m) + m
        keep = ((trow >= mid) & (tcol < mid) & (tcol // (2 * m) == trow // (2 * m))).astype(F32)
        qe = buf["qe"][li]
        ke = buf["ke"][li]
        for h in heads:
            part = keep * _bdot_nt(qe[:, _HEAD_SL[h]], ke[:, _HEAD_SL[h]])
            att[h] = part if att[h] is None else att[h] + part
        yield
    s_old = [s_ref[h] for h in heads]
    i_b = buf["ib"][...]
    q_in = buf["qin"][...]
    k_dec = buf["kdec"][...]
    o_h = [_bdot(att[h], i_b[:, _HEAD_SL[h]]) + _bdot_nt(q_in[:, _HEAD_SL[h]], s_old[h])
           for h in heads]
    for h in heads:
        sl = _HEAD_SL[h]
        s_ref[h] = s_old[h] * buf["gall_h"][:, sl] + _bdot_tn(i_b[:, sl], k_dec[:, sl])
    yield
    for h in heads:
        sl = _HEAD_SL[h]
        o = o_h[h] + buf["odiag"][:, sl]
        o = o * lax.rsqrt(jnp.mean(o * o, axis=-1, keepdims=True) + RMS_EPS)
        o_ref[0, rows, D_RWKV + h * LANES:D_RWKV + (h + 1) * LANES] = o * buf["gate_h"][:, sl]
    yield


_STAGE_ORDER = (0, 1, 0, 1, 0, 1, 0, 1, 0, 1, 0, 1, 1, 0, 1, 0, 1)
_DONE = object()


def _mixer_kernel(layer, *refs):
    has_vmix = layer > 0
    n_in = 3 * (2 if has_vmix else 1)
    n_prm = 13 if has_vmix else 10
    proj_refs = refs[0:3]
    vf_refs = refs[3:6] if has_vmix else (None, None, None)
    prm = list(refs[n_in:n_in + n_prm])
    (mu_ref, w0_ref, wup_ref, a0_ref, aup_ref, kk_ref, ka_ref, rk_ref, gnw_ref, gnb_ref) = prm[:10]
    vm = prm[10:13] if has_vmix else [None, None, None]
    prep_prm = (mu_ref, w0_ref, wup_ref, a0_ref, aup_ref, kk_ref, ka_ref, rk_ref, *vm)
    pos = n_in + n_prm
    lbl_ref, gw_ref, o_ref = refs[pos:pos + 3]
    pos += 3
    vf_out_ref = None
    if not has_vmix:
        vf_out_ref = refs[pos]
        pos += 1
    s_r_ref, prev_ref, s_h_ref = refs[pos:pos + 3]
    pos += 3
    nb = len(_BUF_SHAPES)
    bufs = [dict(zip([n for n, _, _ in _BUF_SHAPES], refs[pos + i * nb:pos + (i + 1) * nb]))
            for i in range(2)]

    def prepare(first, proj_ref, vf_ref, buf):
        vf = vf_ref[0] if has_vmix else None
        return [_rwkv_prepare(first, has_vmix, proj_ref[0, :, 0:RWKV_COLS], vf, prep_prm,
                              prev_ref, buf),
                _hgrn_prepare(layer, proj_ref[0, :, RWKV_COLS:IN_COLS], lbl_ref, gw_ref, buf)]

    def chain(half, buf):
        gens = (_rwkv_chain(half, not has_vmix, buf, gnw_ref, gnb_ref, o_ref, vf_out_ref, s_r_ref),
                _hgrn_chain(half, buf, o_ref, s_h_ref))
        for g in _STAGE_ORDER:
            next(gens[g])
            yield

    def run(chain_gen, prep_gens):
        live = [chain_gen] + list(prep_gens)
        while live:
            for g in list(live):
                if next(g, _DONE) is _DONE:
                    live.remove(g)

    @pl.when(pl.program_id(1) == 0)
    def _():
        s_r_ref[...] = jnp.zeros_like(s_r_ref)
        s_h_ref[...] = jnp.zeros_like(s_h_ref)
        run(iter(()), prepare(True, proj_refs[0], vf_refs[0], bufs[0]))

    run(chain(0, bufs[0]), prepare(False, proj_refs[1], vf_refs[1], bufs[1]))
    run(chain(1, bufs[1]), prepare(False, proj_refs[2], vf_refs[2], bufs[0]))


def _mixer_layer(proj, layer, rwkv_params, vmix, v_first, lb_logits, g_norm_w):
    bsz, seq, _ = proj.shape
    has_vmix = layer > 0
    nc = seq // CHUNK
    row_spec = lambda n: pl.BlockSpec((1, n), lambda b, j: (0, 0))
    full_spec = lambda s: pl.BlockSpec(s, lambda b, j: (0, 0))
    chunk_specs = lambda n: [
        pl.BlockSpec((1, CHUNK, n), lambda b, j: (b, 0, 0)),
        pl.BlockSpec((1, CHUNK, n), lambda b, j: (b, 2 * j + 1, 0)),
        pl.BlockSpec((1, CHUNK, n), lambda b, j: (b, jnp.minimum(2 * j + 2, nc - 1), 0))]
    step_spec = lambda n: pl.BlockSpec((1, 2 * CHUNK, n), lambda b, j: (b, j, 0))
    args = [proj] * 3
    in_specs = chunk_specs(IN_COLS)
    if has_vmix:
        args += [v_first] * 3
        in_specs += chunk_specs(D_RWKV)
    args += list(rwkv_params)
    in_specs += [row_spec(RWKV_COLS), row_spec(D_RWKV), full_spec((DECAY_RANK, D_RWKV)),
                 row_spec(D_RWKV), full_spec((A_RANK, D_RWKV)), row_spec(D_RWKV),
                 row_spec(D_RWKV), row_spec(D_RWKV), row_spec(D_RWKV), row_spec(D_RWKV)]
    out_sds = jax.ShapeDtypeStruct((bsz, seq, D_MODEL), F32)
    if has_vmix:
        args += list(vmix)
        in_specs += [row_spec(D_RWKV), full_spec((D_RWKV, VRES_RANK)),
                     full_spec((VRES_RANK, D_RWKV))]
        out_shape, out_specs = out_sds, step_spec(D_MODEL)
    else:
        out_shape = (out_sds, jax.ShapeDtypeStruct((bsz, seq, D_RWKV), F32))
        out_specs = (step_spec(D_MODEL), step_spec(D_RWKV))
    args += [lb_logits, g_norm_w]
    in_specs += [full_spec((DEPTH, D_HGRN)), row_spec(D_HGRN)]
    scratch = [pltpu.VMEM((RWKV_PAIRS, PAIR_ROWS, LANES), F32),
               pltpu.VMEM((1, RWKV_COLS), F32),
               pltpu.VMEM((HGRN_HEADS, HGRN_EXPAND, LANES), F32)]
    scratch += [pltpu.VMEM(shape, dtype) for _ in range(2) for _, shape, dtype in _BUF_SHAPES]
    res = pl.pallas_call(
        functools.partial(_mixer_kernel, layer),
        out_shape=out_shape,
        grid=(bsz, nc // 2),
        in_specs=in_specs,
        out_specs=out_specs,
        scratch_shapes=scratch,
        compiler_params=pltpu.CompilerParams(
            dimension_semantics=("arbitrary", "arbitrary"), vmem_limit_bytes=VMEM_LIMIT),
        name="mixer",
    )(*args)
    if has_vmix:
        return res, v_first
    return res


def kernel(x, w_in, shift_mu, w_decay0, w_decay_up, a0, a_up, k_k, k_a, r_k, ln_x_w, ln_x_b,
           v_mix0, v_mix_down, v_mix_up, lb_logits, g_norm_w, w_out, ln_w, ln_b):
    out_dtype = x.dtype
    bsz, seq, _ = x.shape
    h = x.astype(F32).reshape(bsz * seq, D_MODEL)
    lb_logits = lb_logits.astype(F32)
    row = lambda t: t.reshape(1, -1)
    v_first = None
    for l in range(DEPTH):
        proj = _in_proj(h, w_in, l).reshape(bsz, seq, IN_COLS)
        params = (row(shift_mu[l]), row(w_decay0[l]), w_decay_up[l], row(a0[l]), a_up[l],
                  row(k_k[l]), row(k_a[l]), row(r_k[l]), row(ln_x_w[l]), row(ln_x_b[l]))
        vmix = None if l == 0 else (row(v_mix0[l - 1]), v_mix_down[l - 1], v_mix_up[l - 1])
        o_mix, v_first = _mixer_layer(proj, l, params, vmix, v_first, lb_logits, row(g_norm_w[l]))
        h = _out_proj_ln(o_mix.reshape(bsz * seq, D_MODEL), h, w_out, l, row(ln_w[l]),
                         row(ln_b[l]))
    return h.reshape(bsz, seq, D_MODEL).astype(out_dtype)
```

```python
import functools
import math

import jax
import jax.numpy as jnp
from jax import lax
from jax.experimental import pallas as pl
from jax.experimental.pallas import tpu as pltpu

D_MODEL = 2048
DEPTH = 2
D_RWKV = D_MODEL // 2
D_HGRN = D_MODEL - D_RWKV
RWKV_HEAD = 64
DECAY_RANK = 64
A_RANK = 64
VRES_RANK = 32
HGRN_EXPAND = 128
HGRN_HEADS = D_HGRN // HGRN_EXPAND
RWKV_COLS = 4 * D_RWKV + DECAY_RANK + A_RANK
HGRN_COLS = 4 * D_HGRN
IN_COLS = RWKV_COLS + HGRN_COLS
ALPHA = (2 * DEPTH) ** 0.25
LN_EPS = 1e-5
GN_EPS = 64e-5
RMS_EPS = 1e-5
LB_FLOOR = 1e-30
LOG2E = math.log2(math.e)

LANES = 128
CHUNK = 64
PAIR_ROWS = 2 * CHUNK
RWKV_PAIRS = D_RWKV // LANES
VMEM_LIMIT = 56 * 1024 * 1024
IN_TM, IN_TN = 512, IN_COLS // 5
OUT_TM = 256

F32 = jnp.float32


def _bf(x):
    return x.astype(jnp.bfloat16)


def _bdot(a, b):
    return jnp.dot(_bf(a), _bf(b), preferred_element_type=F32)


def _bdot_nt(a, b):
    return lax.dot_general(_bf(a), _bf(b), (((1,), (1,)), ((), ())), preferred_element_type=F32)


def _bdot_tn(a, b):
    return lax.dot_general(_bf(a), _bf(b), (((0,), (0,)), ((), ())), preferred_element_type=F32)


def _split_dot(m01, x, terms):
    mb = _bf(m01)
    acc = None
    rem = x
    for t in range(terms):
        piece = _bf(rem)
        part = jnp.dot(mb, piece, preferred_element_type=F32)
        acc = part if acc is None else acc + part
        if t + 1 < terms:
            rem = rem - piece.astype(F32)
    return acc


def _sigmoid(x):
    return 0.5 * jnp.tanh(0.5 * x) + 0.5


def _head_sums(x, scale=1.0):
    rows, cols = x.shape
    ones = ((_iota2((LANES, LANES), 0) // RWKV_HEAD) == (_iota2((LANES, LANES), 1) // RWKV_HEAD))
    ones = jnp.where(ones, scale, 0.0).astype(jnp.bfloat16)
    tiles = [x[:, t * LANES:(t + 1) * LANES] for t in range(cols // LANES)]
    sums = jnp.dot(_bf(jnp.concatenate(tiles, axis=0)), ones, preferred_element_type=F32)
    return jnp.concatenate([sums[t * rows:(t + 1) * rows, :] for t in range(cols // LANES)],
                           axis=1)


def _silu(x):
    return x * _sigmoid(x)


def _iota2(shape, dim):
    return lax.broadcasted_iota(jnp.int32, shape, dim)


def _mm_kernel(x_ref, w_ref, o_ref, wb_ref):
    @pl.when(pl.program_id(1) == 0)
    def _():
        wb_ref[...] = w_ref[...].astype(jnp.bfloat16)

    o_ref[...] = jnp.dot(x_ref[...].astype(jnp.bfloat16), wb_ref[...],
                         preferred_element_type=F32)


def _in_proj(x, w_in, layer):
    m, k = x.shape
    n = w_in.shape[-1]
    tm, tn = IN_TM, IN_TN
    return pl.pallas_call(
        _mm_kernel,
        out_shape=jax.ShapeDtypeStruct((m, n), F32),
        grid=(n // tn, m // tm),
        in_specs=[pl.BlockSpec((tm, k), lambda j, i: (i, 0)),
                  pl.BlockSpec((None, k, tn), lambda j, i: (layer, 0, j))],
        out_specs=pl.BlockSpec((tm, tn), lambda j, i: (i, j)),
        scratch_shapes=[pltpu.VMEM((k, tn), jnp.bfloat16)],
        compiler_params=pltpu.CompilerParams(
            dimension_semantics=("arbitrary", "arbitrary"), vmem_limit_bytes=VMEM_LIMIT),
        name="in_proj",
    )(x, w_in)


def _out_kernel(mix_ref, h_ref, w_ref, lnw_ref, lnb_ref, o_ref, wb_ref):
    @pl.when(pl.program_id(0) == 0)
    def _():
        wb_ref[...] = w_ref[...].astype(jnp.bfloat16)

    y = jnp.dot(mix_ref[...].astype(jnp.bfloat16), wb_ref[...], preferred_element_type=F32)
    u = ALPHA * h_ref[...] + y
    mu = jnp.mean(u, axis=-1, keepdims=True)
    d = u - mu
    var = jnp.mean(d * d, axis=-1, keepdims=True)
    o_ref[...] = d * lax.rsqrt(var + LN_EPS) * lnw_ref[...] + lnb_ref[...]


def _out_proj_ln(o_mix, h, w_out, layer, lnw, lnb):
    m = h.shape[0]
    tm = OUT_TM
    return pl.pallas_call(
        _out_kernel,
        out_shape=jax.ShapeDtypeStruct((m, D_MODEL), F32),
        grid=(m // tm,),
        in_specs=[pl.BlockSpec((tm, D_MODEL), lambda i: (i, 0)),
                  pl.BlockSpec((tm, D_MODEL), lambda i: (i, 0)),
                  pl.BlockSpec((None, D_MODEL, D_MODEL), lambda i: (layer, 0, 0),
                               pipeline_mode=pl.Buffered(1)),
                  pl.BlockSpec((1, D_MODEL), lambda i: (0, 0)),
                  pl.BlockSpec((1, D_MODEL), lambda i: (0, 0))],
        out_specs=pl.BlockSpec((tm, D_MODEL), lambda i: (i, 0)),
        scratch_shapes=[pltpu.VMEM((D_MODEL, D_MODEL), jnp.bfloat16)],
        compiler_params=pltpu.CompilerParams(
            dimension_semantics=("arbitrary",), vmem_limit_bytes=VMEM_LIMIT),
        name="out_proj_ln",
    )(o_mix, h, w_out, lnw, lnb)


_HGRN_LEVELS = tuple(CHUNK >> (i + 1) for i in range(int(math.log2(CHUNK))))
_N_LEVELS = len(_HGRN_LEVELS)
_PAIR_SL = [slice(p * LANES, (p + 1) * LANES) for p in range(RWKV_PAIRS)]
_HEAD_SL = [slice(h * LANES, (h + 1) * LANES) for h in range(HGRN_HEADS)]

_BUF_SHAPES = (
    ("ar", (RWKV_PAIRS, 2 * PAIR_ROWS, LANES), jnp.bfloat16),
    ("bk", (RWKV_PAIRS, 2 * PAIR_ROWS, LANES), jnp.bfloat16),
    ("vs", (RWKV_PAIRS, PAIR_ROWS, LANES), jnp.bfloat16),
    ("bhkh", (RWKV_PAIRS, 2 * PAIR_ROWS, LANES), jnp.bfloat16),
    ("gall_r", (1, D_RWKV), F32),
    ("bonus", (CHUNK, D_RWKV), F32),
    ("gate_r", (CHUNK, D_RWKV), F32),
    ("vnat", (CHUNK, D_RWKV), F32),
    ("qe", (_N_LEVELS, CHUNK, D_HGRN), jnp.bfloat16),
    ("ke", (_N_LEVELS, CHUNK, D_HGRN), jnp.bfloat16),
    ("qin", (CHUNK, D_HGRN), jnp.bfloat16),
    ("kdec", (CHUNK, D_HGRN), jnp.bfloat16),
    ("ib", (CHUNK, D_HGRN), jnp.bfloat16),
    ("odiag", (CHUNK, D_HGRN), F32),
    ("gate_h", (CHUNK, D_HGRN), F32),
    ("gall_h", (1, D_HGRN), F32),
)


def _head_mask():
    lane = _iota2((PAIR_ROWS, LANES), 1)
    row = _iota2((PAIR_ROWS, LANES), 0)
    return ((lane < RWKV_HEAD) == (row < CHUNK)).astype(F32)


def _rwkv_prepare(first, has_vmix, y, vf, prm, prev_ref, buf):
    (mu_ref, w0_ref, wup_ref, a0_ref, aup_ref, kk_ref, ka_ref, rk_ref, v0_ref, vdn_ref,
     vup_ref) = prm
    rolled = pltpu.roll(y, shift=1, axis=0)
    top = rolled[0:8, :]
    prev = jnp.zeros_like(y[0:1, :]) if first else prev_ref[...]
    top = jnp.where(_iota2(top.shape, 0) == 0, prev, top)
    y_prev = jnp.concatenate([top, rolled[8:, :]], axis=0)
    prev_ref[...] = y[CHUNK - 1:CHUNK, :]
    xs = y + mu_ref[...] * (y_prev - y)

    r = xs[:, 0:D_RWKV]
    k = xs[:, D_RWKV:2 * D_RWKV]
    v = xs[:, 2 * D_RWKV:3 * D_RWKV]
    z = xs[:, 3 * D_RWKV:4 * D_RWKV]
    wd = xs[:, 4 * D_RWKV:4 * D_RWKV + DECAY_RANK]
    ad = xs[:, 4 * D_RWKV + DECAY_RANK:RWKV_COLS]

    w_raw = w0_ref[...] + _bdot(jnp.tanh(wd), wup_ref[...])
    logw = (-math.exp(-0.5) * LOG2E) * _sigmoid(w_raw)
    a = _sigmoid(a0_ref[...] + _bdot(ad, aup_ref[...]))
    if has_vmix:
        gate = _sigmoid(v0_ref[...] + _bdot(_bdot(v, vdn_ref[...]), vup_ref[...]))
        v = v + (vf - v) * gate
    else:
        buf["vnat"][...] = v
    buf["gate_r"][...] = _silu(z)
    yield

    kk = k * kk_ref[...]
    kk = kk * lax.rsqrt(jnp.maximum(_head_sums(kk * kk), 1e-24))
    k = k * (1.0 + (a - 1.0) * ka_ref[...])
    b = kk * a
    buf["bonus"][...] = _head_sums(r * k * rk_ref[...]) * v
    yield

    tri = (_iota2((CHUNK, CHUNK), 1) <= _iota2((CHUNK, CHUNK), 0)).astype(F32)
    cl = _split_dot(tri, logw, 3)
    cl_last = cl[CHUNK - 1:CHUNK, :]
    g_inv = jnp.exp2(-cl)
    g_end = jnp.exp2(cl_last - cl)
    buf["gall_r"][...] = jnp.exp2(cl_last)
    a_t = _bf(kk * -jnp.exp2(cl - logw))
    r_t = _bf(r * jnp.exp2(cl))
    b_t = _bf(b * g_inv)
    k_t = _bf(k * g_inv)
    b_h = _bf(b * g_end)
    k_h = _bf(k * g_end)
    v_b = _bf(v)
    yield

    mask = _head_mask().astype(jnp.bfloat16)

    def tile(x, p):
        xp = x[:, _PAIR_SL[p]]
        return jnp.concatenate([xp, xp], axis=0)

    for p in range(RWKV_PAIRS):
        buf["ar"][p] = jnp.concatenate([tile(a_t, p), tile(r_t, p)], axis=0)
        buf["bk"][p] = jnp.concatenate([tile(b_t, p) * mask, tile(k_t, p) * mask], axis=0)
        buf["bhkh"][p] = jnp.concatenate([tile(b_h, p) * mask, tile(k_h, p) * mask], axis=0)
        buf["vs"][p] = tile(v_b, p) * mask
        if p % 2 == 1:
            yield


def _rwkv_chain(half, emit_v, buf, gnw_ref, gnb_ref, o_ref, vf_out_ref, s_ref):
    mask_b = _head_mask().astype(jnp.bfloat16)
    prow = _iota2((PAIR_ROWS, PAIR_ROWS), 0)
    pcol = _iota2((PAIR_ROWS, PAIR_ROWS), 1)
    same_head = (prow // CHUNK) == (pcol // CHUNK)
    t_idx = prow % CHUNK
    s_idx = pcol % CHUNK
    strict = (same_head & (s_idx < t_idx)).astype(F32)
    strict_b = strict.astype(jnp.bfloat16)
    incl_b = (same_head & (s_idx <= t_idx)).astype(jnp.bfloat16)
    eye = (prow == pcol).astype(F32)
    rows = slice(half * CHUNK, (half + 1) * CHUNK)

    pairs = range(RWKV_PAIRS)
    ar = [buf["ar"][p] for p in pairs]
    sc = [_bdot_nt(ar[p], buf["bk"][p]) for p in pairs]
    pw = [strict * sc[p][:PAIR_ROWS, :PAIR_ROWS] for p in pairs]
    tinv = [eye + pw[p] for p in pairs]
    yield
    for _ in range(int(math.log2(CHUNK)) - 1):
        pwb = [_bf(x) for x in pw]
        pw = [_bdot(x, x) for x in pwb]
        tinv = [tinv[p] + _bdot(tinv[p], pw[p]) for p in pairs]
        yield
    a_kv = [jnp.concatenate([_bf(sc[p][:PAIR_ROWS, PAIR_ROWS:]) * strict_b,
                             _bf(sc[p][PAIR_ROWS:, PAIR_ROWS:]) * incl_b], axis=0) for p in pairs]
    a_rb = [_bf(sc[p][PAIR_ROWS:, :PAIR_ROWS]) * incl_b for p in pairs]

    s_old = [s_ref[p] for p in pairs]
    v_s = [buf["vs"][p] for p in pairs]
    ars = [_bdot_nt(ar[p], s_old[p]) for p in pairs]
    akv = [_bdot(a_kv[p], v_s[p]) for p in pairs]
    u = [_bf(_bdot(tinv[p], ars[p][:PAIR_ROWS] + akv[p][:PAIR_ROWS])) * mask_b for p in pairs]
    for p in pairs:
        s_ref[p] = s_old[p] * buf["gall_r"][:, _PAIR_SL[p]] + _bdot_tn(
            jnp.concatenate([u[p], v_s[p]], axis=0), buf["bhkh"][p])
    o_s = [ars[p][PAIR_ROWS:] + _bdot(a_rb[p], u[p]) + akv[p][PAIR_ROWS:] for p in pairs]
    yield

    low = _iota2((CHUNK, LANES), 1) < RWKV_HEAD
    o = jnp.concatenate([jnp.where(low, o_s[p][:CHUNK, :], o_s[p][CHUNK:, :]) for p in pairs],
                        axis=1)
    d = o - _head_sums(o, 1.0 / RWKV_HEAD)
    yield
    var = _head_sums(d * d, 1.0 / RWKV_HEAD)
    on = d * lax.rsqrt(var + GN_EPS) * gnw_ref[...] + gnb_ref[...]
    o_ref[0, rows, 0:D_RWKV] = (on + buf["bonus"][...]) * buf["gate_r"][...]
    if emit_v:
        vf_out_ref[0, rows, :] = buf["vnat"][...]
    yield


def _hgrn_prepare(layer, hg, lbl_ref, gw_ref, buf):
    lg = lbl_ref[...]
    e = jnp.exp(lg - jnp.max(lg, axis=0, keepdims=True))
    sm = e / jnp.sum(e, axis=0, keepdims=True)
    lb = jnp.sum(sm[0:layer + 1, :], axis=0, keepdims=True) - sm[0:1, :]

    q = _silu(hg[:, 0:D_HGRN])
    f_raw = hg[:, D_HGRN:2 * D_HGRN]
    i_in = hg[:, 2 * D_HGRN:3 * D_HGRN]
    z = hg[:, 3 * D_HGRN:4 * D_HGRN]
    buf["gate_h"][...] = gw_ref[...] * _silu(z)
    buf["ib"][...] = _bf(i_in)

    sig = _sigmoid(f_raw)
    log_f = jnp.log2(jnp.maximum(lb, LB_FLOOR) + (1.0 - lb) * sig)
    k = (1.0 - lb) * (1.0 - sig)
    yield

    trow = _iota2((CHUNK, CHUNK), 0)
    tcol = _iota2((CHUNK, CHUNK), 1)
    mats = [tcol <= trow]
    for m in _HGRN_LEVELS:
        mid = (trow // (2 * m)) * (2 * m) + m
        after = trow >= mid
        mats.append((after & (tcol >= mid) & (tcol <= trow))
                    | (~after & (tcol > trow) & (tcol < mid)))
    cums = _split_dot(jnp.concatenate(mats, axis=0).astype(F32), log_f, 2)
    b = cums[0:CHUNK, :]
    b_last = b[CHUNK - 1:CHUNK, :]
    buf["qin"][...] = _bf(q * jnp.exp2(b))
    buf["kdec"][...] = _bf(k * jnp.exp2(b_last - b))
    buf["gall_h"][...] = jnp.exp2(b_last)
    q_b = _bf(q)
    k_b = _bf(k)
    yield
    for li in range(_N_LEVELS):
        e = _bf(jnp.exp2(cums[(li + 1) * CHUNK:(li + 2) * CHUNK, :]))
        buf["qe"][li] = q_b * e
        buf["ke"][li] = k_b * e
        yield
    qk = q * k
    for h in range(HGRN_HEADS):
        sl = _HEAD_SL[h]
        buf["odiag"][:, sl] = jnp.sum(qk[:, sl], axis=-1, keepdims=True) * i_in[:, sl]
    yield


def _hgrn_chain(half, buf, o_ref, s_ref):
    trow = _iota2((CHUNK, CHUNK), 0)
    tcol = _iota2((CHUNK, CHUNK), 1)
    rows = slice(half * CHUNK, (half + 1) * CHUNK)
    heads = range(HGRN_HEADS)
    att = [None] * HGRN_HEADS
    yield
    for li, m in enumerate(_HGRN_LEVELS):
        mid = (trow // (2 * m)) * (2 * m) + m
        keep = ((trow >= mid) & (tcol < mid) & (tcol // (2 * m) == trow // (2 * m))).astype(F32)
        qe = buf["qe"][li]
        ke = buf["ke"][li]
        for h in heads:
            part = keep * _bdot_nt(qe[:, _HEAD_SL[h]], ke[:, _HEAD_SL[h]])
            att[h] = part if att[h] is None else att[h] + part
        yield
    s_old = [s_ref[h] for h in heads]
    i_b = buf["ib"][...]
    q_in = buf["qin"][...]
    k_dec = buf["kdec"][...]
    o_h = [_bdot(att[h], i_b[:, _HEAD_SL[h]]) + _bdot_nt(q_in[:, _HEAD_SL[h]], s_old[h])
           for h in heads]
    for h in heads:
        sl = _HEAD_SL[h]
        s_ref[h] = s_old[h] * buf["gall_h"][:, sl] + _bdot_tn(i_b[:, sl], k_dec[:, sl])
    yield
    for h in heads:
        sl = _HEAD_SL[h]
        o = o_h[h] + buf["odiag"][:, sl]
        o = o * lax.rsqrt(jnp.mean(o * o, axis=-1, keepdims=True) + RMS_EPS)
        o_ref[0, rows, D_RWKV + h * LANES:D_RWKV + (h + 1) * LANES] = o * buf["gate_h"][:, sl]
    yield


_STAGE_ORDER = (0, 1, 0, 1, 0, 1, 0, 1, 0, 1, 0, 1, 1, 0, 1, 0, 1, 0)
_DONE = object()


def _mixer_kernel(layer, *refs):
    has_vmix = layer > 0
    n_in = 3 * (2 if has_vmix else 1)
    n_prm = 13 if has_vmix else 10
    proj_refs = refs[0:3]
    vf_refs = refs[3:6] if has_vmix else (None, None, None)
    prm = list(refs[n_in:n_in + n_prm])
    (mu_ref, w0_ref, wup_ref, a0_ref, aup_ref, kk_ref, ka_ref, rk_ref, gnw_ref, gnb_ref) = prm[:10]
    vm = prm[10:13] if has_vmix else [None, None, None]
    prep_prm = (mu_ref, w0_ref, wup_ref, a0_ref, aup_ref, kk_ref, ka_ref, rk_ref, *vm)
    pos = n_in + n_prm
    lbl_ref, gw_ref, o_ref = refs[pos:pos + 3]
    pos += 3
    vf_out_ref = None
    if not has_vmix:
        vf_out_ref = refs[pos]
        pos += 1
    s_r_ref, prev_ref, s_h_ref = refs[pos:pos + 3]
    pos += 3
    nb = len(_BUF_SHAPES)
    bufs = [dict(zip([n for n, _, _ in _BUF_SHAPES], refs[pos + i * nb:pos + (i + 1) * nb]))
            for i in range(2)]

    def prepare(first, proj_ref, vf_ref, buf):
        vf = vf_ref[0] if has_vmix else None
        return [_rwkv_prepare(first, has_vmix, proj_ref[0, :, 0:RWKV_COLS], vf, prep_prm,
                              prev_ref, buf),
                _hgrn_prepare(layer, proj_ref[0, :, RWKV_COLS:IN_COLS], lbl_ref, gw_ref, buf)]

    def chain(half, buf):
        gens = (_rwkv_chain(half, not has_vmix, buf, gnw_ref, gnb_ref, o_ref, vf_out_ref, s_r_ref),
                _hgrn_chain(half, buf, o_ref, s_h_ref))
        for g in _STAGE_ORDER:
            next(gens[g])
            yield

    def run(chain_gen, prep_gens):
        live = [chain_gen] + list(prep_gens)
        while live:
            for g in list(live):
                if next(g, _DONE) is _DONE:
                    live.remove(g)

    @pl.when(pl.program_id(1) == 0)
    def _():
        s_r_ref[...] = jnp.zeros_like(s_r_ref)
        s_h_ref[...] = jnp.zeros_like(s_h_ref)
        run(iter(()), prepare(True, proj_refs[0], vf_refs[0], bufs[0]))

    run(chain(0, bufs[0]), prepare(False, proj_refs[1], vf_refs[1], bufs[1]))
    run(chain(1, bufs[1]), prepare(False, proj_refs[2], vf_refs[2], bufs[0]))


def _mixer_layer(proj, layer, rwkv_params, vmix, v_first, lb_logits, g_norm_w):
    bsz, seq, _ = proj.shape
    has_vmix = layer > 0
    nc = seq // CHUNK
    row_spec = lambda n: pl.BlockSpec((1, n), lambda b, j: (0, 0))
    full_spec = lambda s: pl.BlockSpec(s, lambda b, j: (0, 0))
    chunk_specs = lambda n: [
        pl.BlockSpec((1, CHUNK, n), lambda b, j: (b, 0, 0)),
        pl.BlockSpec((1, CHUNK, n), lambda b, j: (b, 2 * j + 1, 0)),
        pl.BlockSpec((1, CHUNK, n), lambda b, j: (b, jnp.minimum(2 * j + 2, nc - 1), 0))]
    step_spec = lambda n: pl.BlockSpec((1, 2 * CHUNK, n), lambda b, j: (b, j, 0))
    args = [proj] * 3
    in_specs = chunk_specs(IN_COLS)
    if has_vmix:
        args += [v_first] * 3
        in_specs += chunk_specs(D_RWKV)
    args += list(rwkv_params)
    in_specs += [row_spec(RWKV_COLS), row_spec(D_RWKV), full_spec((DECAY_RANK, D_RWKV)),
                 row_spec(D_RWKV), full_spec((A_RANK, D_RWKV)), row_spec(D_RWKV),
                 row_spec(D_RWKV), row_spec(D_RWKV), row_spec(D_RWKV), row_spec(D_RWKV)]
    out_sds = jax.ShapeDtypeStruct((bsz, seq, D_MODEL), F32)
    if has_vmix:
        args += list(vmix)
        in_specs += [row_spec(D_RWKV), full_spec((D_RWKV, VRES_RANK)),
                     full_spec((VRES_RANK, D_RWKV))]
        out_shape, out_specs = out_sds, step_spec(D_MODEL)
    else:
        out_shape = (out_sds, jax.ShapeDtypeStruct((bsz, seq, D_RWKV), F32))
        out_specs = (step_spec(D_MODEL), step_spec(D_RWKV))
    args += [lb_logits, g_norm_w]
    in_specs += [full_spec((DEPTH, D_HGRN)), row_spec(D_HGRN)]
    scratch = [pltpu.VMEM((RWKV_PAIRS, PAIR_ROWS, LANES), F32),
               pltpu.VMEM((1, RWKV_COLS), F32),
               pltpu.VMEM((HGRN_HEADS, HGRN_EXPAND, LANES), F32)]
    scratch += [pltpu.VMEM(shape, dtype) for _ in range(2) for _, shape, dtype in _BUF_SHAPES]
    res = pl.pallas_call(
        functools.partial(_mixer_kernel, layer),
        out_shape=out_shape,
        grid=(bsz, nc // 2),
        in_specs=in_specs,
        out_specs=out_specs,
        scratch_shapes=scratch,
        compiler_params=pltpu.CompilerParams(
            dimension_semantics=("arbitrary", "arbitrary"), vmem_limit_bytes=VMEM_LIMIT),
        name="mixer",
    )(*args)
    if has_vmix:
        return res, v_first
    return res


def kernel(x, w_in, shift_mu, w_decay0, w_decay_up, a0, a_up, k_k, k_a, r_k, ln_x_w, ln_x_b,
           v_mix0, v_mix_down, v_mix_up, lb_logits, g_norm_w, w_out, ln_w, ln_b):
    out_dtype = x.dtype
    bsz, seq, _ = x.shape
    h = x.astype(F32).reshape(bsz * seq, D_MODEL)
    lb_logits = lb_logits.astype(F32)
    row = lambda t: t.reshape(1, -1)
    v_first = None
    for l in range(DEPTH):
        proj = _in_proj(h, w_in, l).reshape(bsz, seq, IN_COLS)
        params = (row(shift_mu[l]), row(w_decay0[l]), w_decay_up[l], row(a0[l]), a_up[l],
                  row(k_k[l]), row(k_a[l]), row(r_k[l]), row(ln_x_w[l]), row(ln_x_b[l]))
        vmix = None if l == 0 else (row(v_mix0[l - 1]), v_mix_down[l - 1], v_mix_up[l - 1])
        o_mix, v_first = _mixer_layer(proj, l, params, vmix, v_first, lb_logits, row(g_norm_w[l]))
        h = _out_proj_ln(o_mix.reshape(bsz * seq, D_MODEL), h, w_out, l, row(ln_w[l]),
                         row(ln_b[l]))
    return h.reshape(bsz, seq, D_MODEL).astype(out_dtype)
```

```python
import functools
import math

import jax
import jax.numpy as jnp
from jax import lax
from jax.experimental import pallas as pl
from jax.experimental.pallas import tpu as pltpu

D_MODEL = 2048
DEPTH = 2
D_RWKV = D_MODEL // 2
D_HGRN = D_MODEL - D_RWKV
RWKV_HEAD = 64
DECAY_RANK = 64
A_RANK = 64
VRES_RANK = 32
HGRN_EXPAND = 128
HGRN_HEADS = D_HGRN // HGRN_EXPAND
RWKV_COLS = 4 * D_RWKV + DECAY_RANK + A_RANK
HGRN_COLS = 4 * D_HGRN
IN_COLS = RWKV_COLS + HGRN_COLS
ALPHA = (2 * DEPTH) ** 0.25
LN_EPS = 1e-5
GN_EPS = 64e-5
RMS_EPS = 1e-5
LB_FLOOR = 1e-30
LOG2E = math.log2(math.e)

LANES = 128
CHUNK = 64
PAIR_ROWS = 2 * CHUNK
RWKV_PAIRS = D_RWKV // LANES
VMEM_LIMIT = 56 * 1024 * 1024
IN_TM, IN_TN = 512, IN_COLS // 5
OUT_TM = 256

F32 = jnp.float32


def _bf(x):
    return x.astype(jnp.bfloat16)


def _bdot(a, b):
    return jnp.dot(_bf(a), _bf(b), preferred_element_type=F32)


def _bdot_nt(a, b):
    return lax.dot_general(_bf(a), _bf(b), (((1,), (1,)), ((), ())), preferred_element_type=F32)


def _bdot_tn(a, b):
    return lax.dot_general(_bf(a), _bf(b), (((0,), (0,)), ((), ())), preferred_element_type=F32)


def _split_dot(m01, x, terms):
    pieces = []
    rem = x
    for t in range(terms):
        pieces.append(_bf(rem))
        if t + 1 < terms:
            rem = rem - pieces[-1].astype(F32)
    return jnp.dot(_bf(m01), jnp.concatenate(pieces, axis=0), preferred_element_type=F32)


def _sigmoid(x):
    return 0.5 * jnp.tanh(0.5 * x) + 0.5


def _head_sums(xs, scale=1.0):
    rows, cols = xs[0].shape
    width = len(xs) * LANES
    ones = ((_iota2((width, width), 0) // RWKV_HEAD) == (_iota2((width, width), 1) // RWKV_HEAD))
    ones = jnp.where(ones, scale, 0.0).astype(jnp.bfloat16)
    tiles = range(cols // LANES)
    lhs = jnp.concatenate(
        [jnp.concatenate([x[:, t * LANES:(t + 1) * LANES] for t in tiles], axis=0) for x in xs],
        axis=1)
    sums = jnp.dot(_bf(lhs), ones, preferred_element_type=F32)
    return [jnp.concatenate([sums[t * rows:(t + 1) * rows, i * LANES:(i + 1) * LANES]
                             for t in tiles], axis=1) for i in range(len(xs))]


def _silu(x):
    return x * _sigmoid(x)


def _iota2(shape, dim):
    return lax.broadcasted_iota(jnp.int32, shape, dim)


def _mm_kernel(x_ref, w_ref, o_ref, wb_ref):
    @pl.when(pl.program_id(1) == 0)
    def _():
        wb_ref[...] = w_ref[...].astype(jnp.bfloat16)

    o_ref[...] = jnp.dot(x_ref[...].astype(jnp.bfloat16), wb_ref[...],
                         preferred_element_type=F32)


def _in_proj(x, w_in, layer):
    m, k = x.shape
    n = w_in.shape[-1]
    tm, tn = IN_TM, IN_TN
    return pl.pallas_call(
        _mm_kernel,
        out_shape=jax.ShapeDtypeStruct((m, n), F32),
        grid=(n // tn, m // tm),
        in_specs=[pl.BlockSpec((tm, k), lambda j, i: (i, 0)),
                  pl.BlockSpec((None, k, tn), lambda j, i: (layer, 0, j))],
        out_specs=pl.BlockSpec((tm, tn), lambda j, i: (i, j)),
        scratch_shapes=[pltpu.VMEM((k, tn), jnp.bfloat16)],
        compiler_params=pltpu.CompilerParams(
            dimension_semantics=("arbitrary", "arbitrary"), vmem_limit_bytes=VMEM_LIMIT),
        name="in_proj",
    )(x, w_in)


def _out_kernel(mix_ref, h_ref, w_ref, lnw_ref, lnb_ref, o_ref, wb_ref):
    @pl.when(pl.program_id(0) == 0)
    def _():
        wb_ref[...] = w_ref[...].astype(jnp.bfloat16)

    y = jnp.dot(mix_ref[...].astype(jnp.bfloat16), wb_ref[...], preferred_element_type=F32)
    u = ALPHA * h_ref[...] + y
    mu = jnp.mean(u, axis=-1, keepdims=True)
    d = u - mu
    var = jnp.mean(d * d, axis=-1, keepdims=True)
    o_ref[...] = d * lax.rsqrt(var + LN_EPS) * lnw_ref[...] + lnb_ref[...]


def _out_proj_ln(o_mix, h, w_out, layer, lnw, lnb):
    m = h.shape[0]
    tm = OUT_TM
    return pl.pallas_call(
        _out_kernel,
        out_shape=jax.ShapeDtypeStruct((m, D_MODEL), F32),
        grid=(m // tm,),
        in_specs=[pl.BlockSpec((tm, D_MODEL), lambda i: (i, 0)),
                  pl.BlockSpec((tm, D_MODEL), lambda i: (i, 0)),
                  pl.BlockSpec((None, D_MODEL, D_MODEL), lambda i: (layer, 0, 0),
                               pipeline_mode=pl.Buffered(1)),
                  pl.BlockSpec((1, D_MODEL), lambda i: (0, 0)),
                  pl.BlockSpec((1, D_MODEL), lambda i: (0, 0))],
        out_specs=pl.BlockSpec((tm, D_MODEL), lambda i: (i, 0)),
        scratch_shapes=[pltpu.VMEM((D_MODEL, D_MODEL), jnp.bfloat16)],
        compiler_params=pltpu.CompilerParams(
            dimension_semantics=("arbitrary",), vmem_limit_bytes=VMEM_LIMIT),
        name="out_proj_ln",
    )(o_mix, h, w_out, lnw, lnb)


_HGRN_LEVELS = tuple(CHUNK >> (i + 1) for i in range(int(math.log2(CHUNK))))
_N_LEVELS = len(_HGRN_LEVELS)
_PAIR_SL = [slice(p * LANES, (p + 1) * LANES) for p in range(RWKV_PAIRS)]
_HEAD_SL = [slice(h * LANES, (h + 1) * LANES) for h in range(HGRN_HEADS)]

_BUF_SHAPES = (
    ("ar", (RWKV_PAIRS, 2 * PAIR_ROWS, LANES), jnp.bfloat16),
    ("bk", (RWKV_PAIRS, 2 * PAIR_ROWS, LANES), jnp.bfloat16),
    ("vs", (RWKV_PAIRS, PAIR_ROWS, LANES), jnp.bfloat16),
    ("bhkh", (RWKV_PAIRS, 2 * PAIR_ROWS, LANES), jnp.bfloat16),
    ("gall_r", (1, D_RWKV), F32),
    ("bonus", (CHUNK, D_RWKV), F32),
    ("gate_r", (CHUNK, D_RWKV), F32),
    ("vnat", (CHUNK, D_RWKV), F32),
    ("qe", (_N_LEVELS, CHUNK, D_HGRN), jnp.bfloat16),
    ("ke", (_N_LEVELS, CHUNK, D_HGRN), jnp.bfloat16),
    ("qin", (CHUNK, D_HGRN), jnp.bfloat16),
    ("kdec", (CHUNK, D_HGRN), jnp.bfloat16),
    ("ib", (CHUNK, D_HGRN), jnp.bfloat16),
    ("odiag", (CHUNK, D_HGRN), F32),
    ("gate_h", (CHUNK, D_HGRN), F32),
    ("gall_h", (1, D_HGRN), F32),
)


def _head_mask():
    lane = _iota2((PAIR_ROWS, LANES), 1)
    row = _iota2((PAIR_ROWS, LANES), 0)
    return ((lane < RWKV_HEAD) == (row < CHUNK)).astype(F32)


def _rwkv_prepare(first, has_vmix, y, vf, prm, prev_ref, buf):
    (mu_ref, w0_ref, wup_ref, a0_ref, aup_ref, kk_ref, ka_ref, rk_ref, v0_ref, vdn_ref,
     vup_ref) = prm
    rolled = pltpu.roll(y, shift=1, axis=0)
    top = rolled[0:8, :]
    prev = jnp.zeros_like(y[0:1, :]) if first else prev_ref[...]
    top = jnp.where(_iota2(top.shape, 0) == 0, prev, top)
    y_prev = jnp.concatenate([top, rolled[8:, :]], axis=0)
    prev_ref[...] = y[CHUNK - 1:CHUNK, :]
    xs = y + mu_ref[...] * (y_prev - y)

    r = xs[:, 0:D_RWKV]
    k = xs[:, D_RWKV:2 * D_RWKV]
    v = xs[:, 2 * D_RWKV:3 * D_RWKV]
    z = xs[:, 3 * D_RWKV:4 * D_RWKV]
    wd = xs[:, 4 * D_RWKV:4 * D_RWKV + DECAY_RANK]
    ad = xs[:, 4 * D_RWKV + DECAY_RANK:RWKV_COLS]

    w_raw = w0_ref[...] + _bdot(jnp.tanh(wd), wup_ref[...])
    logw = (-math.exp(-0.5) * LOG2E) * _sigmoid(w_raw)
    a = _sigmoid(a0_ref[...] + _bdot(ad, aup_ref[...]))
    if has_vmix:
        gate = _sigmoid(v0_ref[...] + _bdot(_bdot(v, vdn_ref[...]), vup_ref[...]))
        v = v + (vf - v) * gate
    else:
        buf["vnat"][...] = v
    buf["gate_r"][...] = _silu(z)
    yield

    kk = k * kk_ref[...]
    k = k * (1.0 + (a - 1.0) * ka_ref[...])
    kk_sq, rk_sum = _head_sums([kk * kk, r * k * rk_ref[...]])
    kk = kk * lax.rsqrt(jnp.maximum(kk_sq, 1e-24))
    b = kk * a
    buf["bonus"][...] = rk_sum * v
    yield

    tri = (_iota2((CHUNK, 3 * CHUNK), 1) % CHUNK <= _iota2((CHUNK, 3 * CHUNK), 0)).astype(F32)
    cl = _split_dot(tri, logw, 3)
    cl_last = cl[CHUNK - 1:CHUNK, :]
    g_inv = jnp.exp2(-cl)
    g_end = jnp.exp2(cl_last - cl)
    buf["gall_r"][...] = jnp.exp2(cl_last)
    a_t = _bf(kk * -jnp.exp2(cl - logw))
    r_t = _bf(r * jnp.exp2(cl))
    b_t = _bf(b * g_inv)
    k_t = _bf(k * g_inv)
    b_h = _bf(b * g_end)
    k_h = _bf(k * g_end)
    v_b = _bf(v)
    yield

    mask = _head_mask().astype(jnp.bfloat16)

    def tile(x, p):
        xp = x[:, _PAIR_SL[p]]
        return jnp.concatenate([xp, xp], axis=0)

    for p in range(RWKV_PAIRS):
        buf["ar"][p] = jnp.concatenate([tile(a_t, p), tile(r_t, p)], axis=0)
        buf["bk"][p] = jnp.concatenate([tile(b_t, p) * mask, tile(k_t, p) * mask], axis=0)
        buf["bhkh"][p] = jnp.concatenate([tile(b_h, p) * mask, tile(k_h, p) * mask], axis=0)
        buf["vs"][p] = tile(v_b, p) * mask
        if p % 2 == 1:
            yield


def _rwkv_chain(half, emit_v, buf, gnw_ref, gnb_ref, o_ref, vf_out_ref, s_ref):
    mask_b = _head_mask().astype(jnp.bfloat16)
    prow = _iota2((PAIR_ROWS, PAIR_ROWS), 0)
    pcol = _iota2((PAIR_ROWS, PAIR_ROWS), 1)
    same_head = (prow // CHUNK) == (pcol // CHUNK)
    t_idx = prow % CHUNK
    s_idx = pcol % CHUNK
    strict = (same_head & (s_idx < t_idx)).astype(F32)
    strict_b = strict.astype(jnp.bfloat16)
    incl_b = (same_head & (s_idx <= t_idx)).astype(jnp.bfloat16)
    eye = (prow == pcol).astype(F32)
    rows = slice(half * CHUNK, (half + 1) * CHUNK)

    pairs = range(RWKV_PAIRS)
    ar = [buf["ar"][p] for p in pairs]
    sc = [_bdot_nt(ar[p], buf["bk"][p]) for p in pairs]
    pw = [strict * sc[p][:PAIR_ROWS, :PAIR_ROWS] for p in pairs]
    tinv = [eye + pw[p] for p in pairs]
    yield
    for _ in range(int(math.log2(CHUNK)) - 1):
        pwb = [_bf(x) for x in pw]
        pw = [_bdot(x, x) for x in pwb]
        tinv = [tinv[p] + _bdot(tinv[p], pw[p]) for p in pairs]
        yield
    a_kv = [jnp.concatenate([_bf(sc[p][:PAIR_ROWS, PAIR_ROWS:]) * strict_b,
                             _bf(sc[p][PAIR_ROWS:, PAIR_ROWS:]) * incl_b], axis=0) for p in pairs]
    a_rb = [_bf(sc[p][PAIR_ROWS:, :PAIR_ROWS]) * incl_b for p in pairs]

    s_old = [s_ref[p] for p in pairs]
    v_s = [buf["vs"][p] for p in pairs]
    ars = [_bdot_nt(ar[p], s_old[p]) for p in pairs]
    akv = [_bdot(a_kv[p], v_s[p]) for p in pairs]
    u = [_bf(_bdot(tinv[p], ars[p][:PAIR_ROWS] + akv[p][:PAIR_ROWS])) * mask_b for p in pairs]
    for p in pairs:
        s_ref[p] = s_old[p] * buf["gall_r"][:, _PAIR_SL[p]] + _bdot_tn(
            jnp.concatenate([u[p], v_s[p]], axis=0), buf["bhkh"][p])
    o_s = [ars[p][PAIR_ROWS:] + _bdot(a_rb[p], u[p]) + akv[p][PAIR_ROWS:] for p in pairs]
    yield

    low = _iota2((CHUNK, LANES), 1) < RWKV_HEAD
    o = jnp.concatenate([jnp.where(low, o_s[p][:CHUNK, :], o_s[p][CHUNK:, :]) for p in pairs],
                        axis=1)
    d = o - _head_sums([o], 1.0 / RWKV_HEAD)[0]
    yield
    var = _head_sums([d * d], 1.0 / RWKV_HEAD)[0]
    on = d * lax.rsqrt(var + GN_EPS) * gnw_ref[...] + gnb_ref[...]
    o_ref[0, rows, 0:D_RWKV] = (on + buf["bonus"][...]) * buf["gate_r"][...]
    if emit_v:
        vf_out_ref[0, rows, :] = buf["vnat"][...]
    yield


def _hgrn_prepare(layer, hg, lbl_ref, gw_ref, buf):
    lg = lbl_ref[...]
    e = jnp.exp(lg - jnp.max(lg, axis=0, keepdims=True))
    sm = e / jnp.sum(e, axis=0, keepdims=True)
    lb = jnp.sum(sm[0:layer + 1, :], axis=0, keepdims=True) - sm[0:1, :]

    q = _silu(hg[:, 0:D_HGRN])
    f_raw = hg[:, D_HGRN:2 * D_HGRN]
    i_in = hg[:, 2 * D_HGRN:3 * D_HGRN]
    z = hg[:, 3 * D_HGRN:4 * D_HGRN]
    buf["gate_h"][...] = gw_ref[...] * _silu(z)
    buf["ib"][...] = _bf(i_in)

    sig = _sigmoid(f_raw)
    log_f = jnp.log2(jnp.maximum(lb, LB_FLOOR) + (1.0 - lb) * sig)
    k = (1.0 - lb) * (1.0 - sig)
    yield

    trow = _iota2((CHUNK, 2 * CHUNK), 0)
    tcol = _iota2((CHUNK, 2 * CHUNK), 1) % CHUNK
    mats = [tcol <= trow]
    for m in _HGRN_LEVELS:
        mid = (trow // (2 * m)) * (2 * m) + m
        after = trow >= mid
        mats.append((after & (tcol >= mid) & (tcol <= trow))
                    | (~after & (tcol > trow) & (tcol < mid)))
    cums = _split_dot(jnp.concatenate(mats, axis=0).astype(F32), log_f, 2)
    b = cums[0:CHUNK, :]
    b_last = b[CHUNK - 1:CHUNK, :]
    buf["qin"][...] = _bf(q * jnp.exp2(b))
    buf["kdec"][...] = _bf(k * jnp.exp2(b_last - b))
    buf["gall_h"][...] = jnp.exp2(b_last)
    q_b = _bf(q)
    k_b = _bf(k)
    yield
    for li in range(_N_LEVELS):
        e = _bf(jnp.exp2(cums[(li + 1) * CHUNK:(li + 2) * CHUNK, :]))
        buf["qe"][li] = q_b * e
        buf["ke"][li] = k_b * e
        yield
    qk = q * k
    for h in range(HGRN_HEADS):
        sl = _HEAD_SL[h]
        buf["odiag"][:, sl] = jnp.sum(qk[:, sl], axis=-1, keepdims=True) * i_in[:, sl]
    yield


def _hgrn_chain(half, buf, o_ref, s_ref):
    trow = _iota2((CHUNK, LANES), 0)
    tcol = _iota2((CHUNK, LANES), 1) % CHUNK
    rows = slice(half * CHUNK, (half + 1) * CHUNK)
    heads = range(HGRN_HEADS)
    pairs = range(HGRN_HEADS // 2)
    pair_sl = [slice(2 * hp * LANES, 2 * (hp + 1) * LANES) for hp in pairs]

    def diag2(x0, x1):
        return jnp.concatenate([jnp.concatenate([x0, jnp.zeros_like(x1)], axis=1),
                                jnp.concatenate([jnp.zeros_like(x0), x1], axis=1)], axis=0)

    att = [None] * len(pairs)
    yield
    for li, m in enumerate(_HGRN_LEVELS):
        mid = (trow // (2 * m)) * (2 * m) + m
        keep = ((trow >= mid) & (tcol < mid) & (tcol // (2 * m) == trow // (2 * m))).astype(F32)
        qe = buf["qe"][li]
        ke = buf["ke"][li]
        for hp in pairs:
            ke2 = diag2(ke[:, _HEAD_SL[2 * hp]], ke[:, _HEAD_SL[2 * hp + 1]])
            part = keep * _bdot_nt(qe[:, pair_sl[hp]], ke2)
            att[hp] = part if att[hp] is None else att[hp] + part
        yield
    s_old = [s_ref[h] for h in heads]
    i_b = buf["ib"][...]
    q_in = buf["qin"][...]
    k_dec = buf["kdec"][...]
    o_2 = []
    for hp in pairs:
        h0, h1 = 2 * hp, 2 * hp + 1
        i2 = diag2(i_b[:, _HEAD_SL[h0]], i_b[:, _HEAD_SL[h1]])
        s2 = diag2(_bf(s_old[h0]), _bf(s_old[h1]))
        o_2.append(_bdot(att[hp], i2) + _bdot_nt(q_in[:, pair_sl[hp]], s2))
    for h in heads:
        sl = _HEAD_SL[h]
        s_ref[h] = s_old[h] * buf["gall_h"][:, sl] + _bdot_tn(i_b[:, sl], k_dec[:, sl])
    yield
    for h in heads:
        sl = _HEAD_SL[h]
        o = o_2[h // 2][:, (h % 2) * LANES:(h % 2 + 1) * LANES] + buf["odiag"][:, sl]
        o = o * lax.rsqrt(jnp.mean(o * o, axis=-1, keepdims=True) + RMS_EPS)
        o_ref[0, rows, D_RWKV + h * LANES:D_RWKV + (h + 1) * LANES] = o * buf["gate_h"][:, sl]
    yield


_STAGE_ORDER = (0, 1, 0, 1, 0, 1, 0, 1, 0, 1, 0, 1, 1, 0, 1, 0, 1, 0)
_DONE = object()


def _mixer_kernel(layer, *refs):
    has_vmix = layer > 0
    n_in = 3 * (2 if has_vmix else 1)
    n_prm = 13 if has_vmix else 10
    proj_refs = refs[0:3]
    vf_refs = refs[3:6] if has_vmix else (None, None, None)
    prm = list(refs[n_in:n_in + n_prm])
    (mu_ref, w0_ref, wup_ref, a0_ref, aup_ref, kk_ref, ka_ref, rk_ref, gnw_ref, gnb_ref) = prm[:10]
    vm = prm[10:13] if has_vmix else [None, None, None]
    prep_prm = (mu_ref, w0_ref, wup_ref, a0_ref, aup_ref, kk_ref, ka_ref, rk_ref, *vm)
    pos = n_in + n_prm
    lbl_ref, gw_ref, o_ref = refs[pos:pos + 3]
    pos += 3
    vf_out_ref = None
    if not has_vmix:
        vf_out_ref = refs[pos]
        pos += 1
    s_r_ref, prev_ref, s_h_ref = refs[pos:pos + 3]
    pos += 3
    nb = len(_BUF_SHAPES)
    bufs = [dict(zip([n for n, _, _ in _BUF_SHAPES], refs[pos + i * nb:pos + (i + 1) * nb]))
            for i in range(2)]

    def prepare(first, proj_ref, vf_ref, buf):
        vf = vf_ref[0] if has_vmix else None
        return [_rwkv_prepare(first, has_vmix, proj_ref[0, :, 0:RWKV_COLS], vf, prep_prm,
                              prev_ref, buf),
                _hgrn_prepare(layer, proj_ref[0, :, RWKV_COLS:IN_COLS], lbl_ref, gw_ref, buf)]

    def chain(half, buf):
        gens = (_rwkv_chain(half, not has_vmix, buf, gnw_ref, gnb_ref, o_ref, vf_out_ref, s_r_ref),
                _hgrn_chain(half, buf, o_ref, s_h_ref))
        for g in _STAGE_ORDER:
            next(gens[g])
            yield

    def run(chain_gen, prep_gens):
        live = [chain_gen] + list(prep_gens)
        while live:
            for g in list(live):
                if next(g, _DONE) is _DONE:
                    live.remove(g)

    @pl.when(pl.program_id(1) == 0)
    def _():
        s_r_ref[...] = jnp.zeros_like(s_r_ref)
        s_h_ref[...] = jnp.zeros_like(s_h_ref)
        run(iter(()), prepare(True, proj_refs[0], vf_refs[0], bufs[0]))

    run(chain(0, bufs[0]), prepare(False, proj_refs[1], vf_refs[1], bufs[1]))
    run(chain(1, bufs[1]), prepare(False, proj_refs[2], vf_refs[2], bufs[0]))


def _mixer_layer(proj, layer, rwkv_params, vmix, v_first, lb_logits, g_norm_w):
    bsz, seq, _ = proj.shape
    has_vmix = layer > 0
    nc = seq // CHUNK
    row_spec = lambda n: pl.BlockSpec((1, n), lambda b, j: (0, 0))
    full_spec = lambda s: pl.BlockSpec(s, lambda b, j: (0, 0))
    chunk_specs = lambda n: [
        pl.BlockSpec((1, CHUNK, n), lambda b, j: (b, 0, 0)),
        pl.BlockSpec((1, CHUNK, n), lambda b, j: (b, 2 * j + 1, 0)),
        pl.BlockSpec((1, CHUNK, n), lambda b, j: (b, jnp.minimum(2 * j + 2, nc - 1), 0))]
    step_spec = lambda n: pl.BlockSpec((1, 2 * CHUNK, n), lambda b, j: (b, j, 0))
    args = [proj] * 3
    in_specs = chunk_specs(IN_COLS)
    if has_vmix:
        args += [v_first] * 3
        in_specs += chunk_specs(D_RWKV)
    args += list(rwkv_params)
    in_specs += [row_spec(RWKV_COLS), row_spec(D_RWKV), full_spec((DECAY_RANK, D_RWKV)),
                 row_spec(D_RWKV), full_spec((A_RANK, D_RWKV)), row_spec(D_RWKV),
                 row_spec(D_RWKV), row_spec(D_RWKV), row_spec(D_RWKV), row_spec(D_RWKV)]
    out_sds = jax.ShapeDtypeStruct((bsz, seq, D_MODEL), F32)
    if has_vmix:
        args += list(vmix)
        in_specs += [row_spec(D_RWKV), full_spec((D_RWKV, VRES_RANK)),
                     full_spec((VRES_RANK, D_RWKV))]
        out_shape, out_specs = out_sds, step_spec(D_MODEL)
    else:
        out_shape = (out_sds, jax.ShapeDtypeStruct((bsz, seq, D_RWKV), F32))
        out_specs = (step_spec(D_MODEL), step_spec(D_RWKV))
    args += [lb_logits, g_norm_w]
    in_specs += [full_spec((DEPTH, D_HGRN)), row_spec(D_HGRN)]
    scratch = [pltpu.VMEM((RWKV_PAIRS, PAIR_ROWS, LANES), F32),
               pltpu.VMEM((1, RWKV_COLS), F32),
               pltpu.VMEM((HGRN_HEADS, HGRN_EXPAND, LANES), F32)]
    scratch += [pltpu.VMEM(shape, dtype) for _ in range(2) for _, shape, dtype in _BUF_SHAPES]
    res = pl.pallas_call(
        functools.partial(_mixer_kernel, layer),
        out_shape=out_shape,
        grid=(bsz, nc // 2),
        in_specs=in_specs,
        out_specs=out_specs,
        scratch_shapes=scratch,
        compiler_params=pltpu.CompilerParams(
            dimension_semantics=("arbitrary", "arbitrary"), vmem_limit_bytes=VMEM_LIMIT),
        name="mixer",
    )(*args)
    if has_vmix:
        return res, v_first
    return res


def kernel(x, w_in, shift_mu, w_decay0, w_decay_up, a0, a_up, k_k, k_a, r_k, ln_x_w, ln_x_b,
           v_mix0, v_mix_down, v_mix_up, lb_logits, g_norm_w, w_out, ln_w, ln_b):
    out_dtype = x.dtype
    bsz, seq, _ = x.shape
    h = x.astype(F32).reshape(bsz * seq, D_MODEL)
    lb_logits = lb_logits.astype(F32)
    row = lambda t: t.reshape(1, -1)
    v_first = None
    for l in range(DEPTH):
        proj = _in_proj(h, w_in, l).reshape(bsz, seq, IN_COLS)
        params = (row(shift_mu[l]), row(w_decay0[l]), w_decay_up[l], row(a0[l]), a_up[l],
                  row(k_k[l]), row(k_a[l]), row(r_k[l]), row(ln_x_w[l]), row(ln_x_b[l]))
        vmix = None if l == 0 else (row(v_mix0[l - 1]), v_mix_down[l - 1], v_mix_up[l - 1])
        o_mix, v_first = _mixer_layer(proj, l, params, vmix, v_first, lb_logits, row(g_norm_w[l]))
        h = _out_proj_ln(o_mix.reshape(bsz * seq, D_MODEL), h, w_out, l, row(ln_w[l]),
                         row(ln_b[l]))
    return h.reshape(bsz, seq, D_MODEL).astype(out_dtype)
```

```python
import functools
import math

import jax
import jax.numpy as jnp
from jax import lax
from jax.experimental import pallas as pl
from jax.experimental.pallas import tpu as pltpu

D_MODEL = 2048
DEPTH = 2
D_RWKV = D_MODEL // 2
D_HGRN = D_MODEL - D_RWKV
RWKV_HEAD = 64
DECAY_RANK = 64
A_RANK = 64
VRES_RANK = 32
HGRN_EXPAND = 128
HGRN_HEADS = D_HGRN // HGRN_EXPAND
RWKV_COLS = 4 * D_RWKV + DECAY_RANK + A_RANK
HGRN_COLS = 4 * D_HGRN
IN_COLS = RWKV_COLS + HGRN_COLS
ALPHA = (2 * DEPTH) ** 0.25
LN_EPS = 1e-5
GN_EPS = 64e-5
RMS_EPS = 1e-5
LB_FLOOR = 1e-30
LOG2E = math.log2(math.e)

LANES = 128
CHUNK = 64
PAIR_ROWS = 2 * CHUNK
RWKV_PAIRS = D_RWKV // LANES
VMEM_LIMIT = 56 * 1024 * 1024
IN_TM, IN_TN = 512, IN_COLS // 5
OUT_TM = 256

F32 = jnp.float32


def _bf(x):
    return x.astype(jnp.bfloat16)


def _bdot(a, b):
    return jnp.dot(_bf(a), _bf(b), preferred_element_type=F32)


def _bdot_nt(a, b):
    return lax.dot_general(_bf(a), _bf(b), (((1,), (1,)), ((), ())), preferred_element_type=F32)


def _bdot_tn(a, b):
    return lax.dot_general(_bf(a), _bf(b), (((0,), (0,)), ((), ())), preferred_element_type=F32)


def _split_dot(m01, x, terms):
    pieces = []
    rem = x
    for t in range(terms):
        pieces.append(_bf(rem))
        if t + 1 < terms:
            rem = rem - pieces[-1].astype(F32)
    return jnp.dot(_bf(m01), jnp.concatenate(pieces, axis=0), preferred_element_type=F32)


def _sigmoid(x):
    return 0.5 * jnp.tanh(0.5 * x) + 0.5


def _head_sums(xs, scale=1.0):
    rows, cols = xs[0].shape
    width = len(xs) * LANES
    ones = ((_iota2((width, width), 0) // RWKV_HEAD) == (_iota2((width, width), 1) // RWKV_HEAD))
    ones = jnp.where(ones, scale, 0.0).astype(jnp.bfloat16)
    tiles = range(cols // LANES)
    lhs = jnp.concatenate(
        [jnp.concatenate([x[:, t * LANES:(t + 1) * LANES] for t in tiles], axis=0) for x in xs],
        axis=1)
    sums = jnp.dot(_bf(lhs), ones, preferred_element_type=F32)
    return [jnp.concatenate([sums[t * rows:(t + 1) * rows, i * LANES:(i + 1) * LANES]
                             for t in tiles], axis=1) for i in range(len(xs))]


def _silu(x):
    return x * _sigmoid(x)


def _iota2(shape, dim):
    return lax.broadcasted_iota(jnp.int32, shape, dim)


def _mm_kernel(x_ref, w_ref, o_ref, wb_ref):
    @pl.when(pl.program_id(1) == 0)
    def _():
        wb_ref[...] = w_ref[...].astype(jnp.bfloat16)

    o_ref[...] = jnp.dot(x_ref[...], wb_ref[...], preferred_element_type=F32)


def _in_proj(x, w_in, layer):
    m, k = x.shape
    n = w_in.shape[-1]
    tm, tn = IN_TM, IN_TN
    return pl.pallas_call(
        _mm_kernel,
        out_shape=jax.ShapeDtypeStruct((m, n), F32),
        grid=(n // tn, m // tm),
        in_specs=[pl.BlockSpec((tm, k), lambda j, i: (i, 0)),
                  pl.BlockSpec((None, k, tn), lambda j, i: (layer, 0, j))],
        out_specs=pl.BlockSpec((tm, tn), lambda j, i: (i, j)),
        scratch_shapes=[pltpu.VMEM((k, tn), jnp.bfloat16)],
        compiler_params=pltpu.CompilerParams(
            dimension_semantics=("arbitrary", "arbitrary"), vmem_limit_bytes=VMEM_LIMIT),
        name="in_proj",
    )(x, w_in)


def _out_kernel(emit_bf16, mix_ref, h_ref, w_ref, lnw_ref, lnb_ref, o_ref, *rest):
    wb_ref = rest[-1]

    @pl.when(pl.program_id(0) == 0)
    def _():
        wb_ref[...] = w_ref[...].astype(jnp.bfloat16)

    y = jnp.dot(mix_ref[...].astype(jnp.bfloat16), wb_ref[...], preferred_element_type=F32)
    u = ALPHA * h_ref[...] + y
    mu = jnp.mean(u, axis=-1, keepdims=True)
    d = u - mu
    var = jnp.mean(d * d, axis=-1, keepdims=True)
    out = d * lax.rsqrt(var + LN_EPS) * lnw_ref[...] + lnb_ref[...]
    o_ref[...] = out
    if emit_bf16:
        rest[0][...] = out.astype(jnp.bfloat16)


def _out_proj_ln(o_mix, h, w_out, layer, lnw, lnb, emit_bf16):
    m = h.shape[0]
    tm = OUT_TM
    out_spec = pl.BlockSpec((tm, D_MODEL), lambda i: (i, 0))
    out_shape = jax.ShapeDtypeStruct((m, D_MODEL), F32)
    if emit_bf16:
        out_shape = (out_shape, jax.ShapeDtypeStruct((m, D_MODEL), jnp.bfloat16))
        out_spec = (out_spec, pl.BlockSpec((tm, D_MODEL), lambda i: (i, 0)))
    return pl.pallas_call(
        functools.partial(_out_kernel, emit_bf16),
        out_shape=out_shape,
        grid=(m // tm,),
        in_specs=[pl.BlockSpec((tm, D_MODEL), lambda i: (i, 0)),
                  pl.BlockSpec((tm, D_MODEL), lambda i: (i, 0)),
                  pl.BlockSpec((None, D_MODEL, D_MODEL), lambda i: (layer, 0, 0),
                               pipeline_mode=pl.Buffered(1)),
                  pl.BlockSpec((1, D_MODEL), lambda i: (0, 0)),
                  pl.BlockSpec((1, D_MODEL), lambda i: (0, 0))],
        out_specs=out_spec,
        scratch_shapes=[pltpu.VMEM((D_MODEL, D_MODEL), jnp.bfloat16)],
        compiler_params=pltpu.CompilerParams(
            dimension_semantics=("arbitrary",), vmem_limit_bytes=VMEM_LIMIT),
        name="out_proj_ln",
    )(o_mix, h, w_out, lnw, lnb)


_HGRN_LEVELS = tuple(CHUNK >> (i + 1) for i in range(int(math.log2(CHUNK))))
_N_LEVELS = len(_HGRN_LEVELS)
_PAIR_SL = [slice(p * LANES, (p + 1) * LANES) for p in range(RWKV_PAIRS)]
_HEAD_SL = [slice(h * LANES, (h + 1) * LANES) for h in range(HGRN_HEADS)]

_BUF_SHAPES = (
    ("ar", (RWKV_PAIRS, 2 * PAIR_ROWS, LANES), jnp.bfloat16),
    ("bk", (RWKV_PAIRS, 2 * PAIR_ROWS, LANES), jnp.bfloat16),
    ("vs", (RWKV_PAIRS, PAIR_ROWS, LANES), jnp.bfloat16),
    ("bhkh", (RWKV_PAIRS, 2 * PAIR_ROWS, LANES), jnp.bfloat16),
    ("gall_r", (1, D_RWKV), F32),
    ("bonus", (CHUNK, D_RWKV), F32),
    ("gate_r", (CHUNK, D_RWKV), F32),
    ("vnat", (CHUNK, D_RWKV), F32),
    ("qe", (_N_LEVELS, CHUNK, D_HGRN), jnp.bfloat16),
    ("ke", (_N_LEVELS, CHUNK, D_HGRN), jnp.bfloat16),
    ("qin", (CHUNK, D_HGRN), jnp.bfloat16),
    ("kdec", (CHUNK, D_HGRN), jnp.bfloat16),
    ("ib", (CHUNK, D_HGRN), jnp.bfloat16),
    ("odiag", (CHUNK, D_HGRN), F32),
    ("gate_h", (CHUNK, D_HGRN), F32),
    ("gall_h", (1, D_HGRN), F32),
)


def _head_mask():
    lane = _iota2((PAIR_ROWS, LANES), 1)
    row = _iota2((PAIR_ROWS, LANES), 0)
    return ((lane < RWKV_HEAD) == (row < CHUNK)).astype(F32)


def _rwkv_prepare(first, has_vmix, y, vf, prm, prev_ref, buf):
    (mu_ref, w0_ref, wup_ref, a0_ref, aup_ref, kk_ref, ka_ref, rk_ref, v0_ref, vdn_ref,
     vup_ref) = prm
    rolled = pltpu.roll(y, shift=1, axis=0)
    top = rolled[0:8, :]
    prev = jnp.zeros_like(y[0:1, :]) if first else prev_ref[...]
    top = jnp.where(_iota2(top.shape, 0) == 0, prev, top)
    y_prev = jnp.concatenate([top, rolled[8:, :]], axis=0)
    prev_ref[...] = y[CHUNK - 1:CHUNK, :]
    xs = y + mu_ref[...] * (y_prev - y)

    r = xs[:, 0:D_RWKV]
    k = xs[:, D_RWKV:2 * D_RWKV]
    v = xs[:, 2 * D_RWKV:3 * D_RWKV]
    z = xs[:, 3 * D_RWKV:4 * D_RWKV]
    wd = xs[:, 4 * D_RWKV:4 * D_RWKV + DECAY_RANK]
    ad = xs[:, 4 * D_RWKV + DECAY_RANK:RWKV_COLS]

    w_raw = w0_ref[...] + _bdot(jnp.tanh(wd), wup_ref[...])
    logw = (-math.exp(-0.5) * LOG2E) * _sigmoid(w_raw)
    a = _sigmoid(a0_ref[...] + _bdot(ad, aup_ref[...]))
    if has_vmix:
        gate = _sigmoid(v0_ref[...] + _bdot(_bdot(v, vdn_ref[...]), vup_ref[...]))
        v = v + (vf - v) * gate
    else:
        buf["vnat"][...] = v
    buf["gate_r"][...] = _silu(z)
    yield

    kk = k * kk_ref[...]
    k = k * (1.0 + (a - 1.0) * ka_ref[...])
    kk_sq, rk_sum = _head_sums([kk * kk, r * k * rk_ref[...]])
    kk = kk * lax.rsqrt(jnp.maximum(kk_sq, 1e-24))
    b = kk * a
    buf["bonus"][...] = rk_sum * v
    yield

    tri = (_iota2((CHUNK, 3 * CHUNK), 1) % CHUNK <= _iota2((CHUNK, 3 * CHUNK), 0)).astype(F32)
    cl = _split_dot(tri, logw, 3)
    cl_last = cl[CHUNK - 1:CHUNK, :]
    g_inv = jnp.exp2(-cl)
    g_end = jnp.exp2(cl_last - cl)
    buf["gall_r"][...] = jnp.exp2(cl_last)
    a_t = _bf(kk * -jnp.exp2(cl - logw))
    r_t = _bf(r * jnp.exp2(cl))
    b_t = _bf(b * g_inv)
    k_t = _bf(k * g_inv)
    b_h = _bf(b * g_end)
    k_h = _bf(k * g_end)
    v_b = _bf(v)
    yield

    mask = _head_mask().astype(jnp.bfloat16)

    def tile(x, p):
        xp = x[:, _PAIR_SL[p]]
        return jnp.concatenate([xp, xp], axis=0)

    for p in range(RWKV_PAIRS):
        buf["ar"][p] = jnp.concatenate([tile(a_t, p), tile(r_t, p)], axis=0)
        buf["bk"][p] = jnp.concatenate([tile(b_t, p) * mask, tile(k_t, p) * mask], axis=0)
        buf["bhkh"][p] = jnp.concatenate([tile(b_h, p) * mask, tile(k_h, p) * mask], axis=0)
        buf["vs"][p] = tile(v_b, p) * mask
        if p % 2 == 1:
            yield


def _rwkv_chain(half, emit_v, buf, gnw_ref, gnb_ref, o_ref, vf_out_ref, s_ref):
    mask_b = _head_mask().astype(jnp.bfloat16)
    prow = _iota2((PAIR_ROWS, PAIR_ROWS), 0)
    pcol = _iota2((PAIR_ROWS, PAIR_ROWS), 1)
    same_head = (prow // CHUNK) == (pcol // CHUNK)
    t_idx = prow % CHUNK
    s_idx = pcol % CHUNK
    strict = (same_head & (s_idx < t_idx)).astype(F32)
    strict_b = strict.astype(jnp.bfloat16)
    incl_b = (same_head & (s_idx <= t_idx)).astype(jnp.bfloat16)
    eye = (prow == pcol).astype(F32)
    rows = slice(half * CHUNK, (half + 1) * CHUNK)

    pairs = range(RWKV_PAIRS)
    ar = [buf["ar"][p] for p in pairs]
    sc = [_bdot_nt(ar[p], buf["bk"][p]) for p in pairs]
    pw = [strict * sc[p][:PAIR_ROWS, :PAIR_ROWS] for p in pairs]
    tinv = [eye + pw[p] for p in pairs]
    a_kv = [jnp.concatenate([_bf(sc[p][:PAIR_ROWS, PAIR_ROWS:]) * strict_b,
                             _bf(sc[p][PAIR_ROWS:, PAIR_ROWS:]) * incl_b], axis=0) for p in pairs]
    a_rb = [_bf(sc[p][PAIR_ROWS:, :PAIR_ROWS]) * incl_b for p in pairs]
    yield
    for _ in range(int(math.log2(CHUNK)) - 1):
        pwb = [_bf(x) for x in pw]
        pw = [_bdot(x, x) for x in pwb]
        tinv = [tinv[p] + _bdot(tinv[p], pw[p]) for p in pairs]
        yield

    s_old = [s_ref[p] for p in pairs]
    v_s = [buf["vs"][p] for p in pairs]
    ars = [_bdot_nt(ar[p], s_old[p]) for p in pairs]
    akv = [_bdot(a_kv[p], v_s[p]) for p in pairs]
    u = [_bf(_bdot(tinv[p], ars[p][:PAIR_ROWS] + akv[p][:PAIR_ROWS])) * mask_b for p in pairs]
    for p in pairs:
        s_ref[p] = s_old[p] * buf["gall_r"][:, _PAIR_SL[p]] + _bdot_tn(
            jnp.concatenate([u[p], v_s[p]], axis=0), buf["bhkh"][p])
    o_s = [ars[p][PAIR_ROWS:] + _bdot(a_rb[p], u[p]) + akv[p][PAIR_ROWS:] for p in pairs]
    yield

    low = _iota2((CHUNK, LANES), 1) < RWKV_HEAD
    o = jnp.concatenate([jnp.where(low, o_s[p][:CHUNK, :], o_s[p][CHUNK:, :]) for p in pairs],
                        axis=1)
    d = o - _head_sums([o], 1.0 / RWKV_HEAD)[0]
    yield
    var = _head_sums([d * d], 1.0 / RWKV_HEAD)[0]
    on = d * lax.rsqrt(var + GN_EPS) * gnw_ref[...] + gnb_ref[...]
    o_ref[0, rows, 0:D_RWKV] = (on + buf["bonus"][...]) * buf["gate_r"][...]
    if emit_v:
        vf_out_ref[0, rows, :] = buf["vnat"][...]
    yield


def _hgrn_prepare(layer, hg, lbl_ref, gw_ref, buf):
    lg = lbl_ref[...]
    e = jnp.exp(lg - jnp.max(lg, axis=0, keepdims=True))
    sm = e / jnp.sum(e, axis=0, keepdims=True)
    lb = jnp.sum(sm[0:layer + 1, :], axis=0, keepdims=True) - sm[0:1, :]

    q = _silu(hg[:, 0:D_HGRN])
    f_raw = hg[:, D_HGRN:2 * D_HGRN]
    i_in = hg[:, 2 * D_HGRN:3 * D_HGRN]
    z = hg[:, 3 * D_HGRN:4 * D_HGRN]
    buf["gate_h"][...] = gw_ref[...] * _silu(z)
    buf["ib"][...] = _bf(i_in)

    sig = _sigmoid(f_raw)
    log_f = jnp.log2(jnp.maximum(lb, LB_FLOOR) + (1.0 - lb) * sig)
    k = (1.0 - lb) * (1.0 - sig)
    yield

    trow = _iota2((CHUNK, 2 * CHUNK), 0)
    tcol = _iota2((CHUNK, 2 * CHUNK), 1) % CHUNK
    mats = [tcol <= trow]
    for m in _HGRN_LEVELS:
        mid = (trow // (2 * m)) * (2 * m) + m
        after = trow >= mid
        mats.append((after & (tcol >= mid) & (tcol <= trow))
                    | (~after & (tcol > trow) & (tcol < mid)))
    cums = _split_dot(jnp.concatenate(mats, axis=0).astype(F32), log_f, 2)
    b = cums[0:CHUNK, :]
    b_last = b[CHUNK - 1:CHUNK, :]
    buf["qin"][...] = _bf(q * jnp.exp2(b))
    buf["kdec"][...] = _bf(k * jnp.exp2(b_last - b))
    buf["gall_h"][...] = jnp.exp2(b_last)
    q_b = _bf(q)
    k_b = _bf(k)
    yield
    for li in range(_N_LEVELS):
        e = _bf(jnp.exp2(cums[(li + 1) * CHUNK:(li + 2) * CHUNK, :]))
        buf["qe"][li] = q_b * e
        buf["ke"][li] = k_b * e
        yield
    qk = q * k
    for h in range(HGRN_HEADS):
        sl = _HEAD_SL[h]
        buf["odiag"][:, sl] = jnp.sum(qk[:, sl], axis=-1, keepdims=True) * i_in[:, sl]
    yield


def _hgrn_chain(half, buf, o_ref, s_ref):
    trow = _iota2((CHUNK, LANES), 0)
    tcol = _iota2((CHUNK, LANES), 1) % CHUNK
    rows = slice(half * CHUNK, (half + 1) * CHUNK)
    heads = range(HGRN_HEADS)
    pairs = range(HGRN_HEADS // 2)
    pair_sl = [slice(2 * hp * LANES, 2 * (hp + 1) * LANES) for hp in pairs]

    def diag2(x0, x1):
        return jnp.concatenate([jnp.concatenate([x0, jnp.zeros_like(x1)], axis=1),
                                jnp.concatenate([jnp.zeros_like(x0), x1], axis=1)], axis=0)

    att = [None] * len(pairs)
    yield
    for li, m in enumerate(_HGRN_LEVELS):
        mid = (trow // (2 * m)) * (2 * m) + m
        keep = ((trow >= mid) & (tcol < mid) & (tcol // (2 * m) == trow // (2 * m))).astype(F32)
        qe = buf["qe"][li]
        ke = buf["ke"][li]
        for hp in pairs:
            ke2 = diag2(ke[:, _HEAD_SL[2 * hp]], ke[:, _HEAD_SL[2 * hp + 1]])
            part = keep * _bdot_nt(qe[:, pair_sl[hp]], ke2)
            att[hp] = part if att[hp] is None else att[hp] + part
        yield
    s_old = [s_ref[h] for h in heads]
    i_b = buf["ib"][...]
    q_in = buf["qin"][...]
    k_dec = buf["kdec"][...]
    o_2 = []
    for hp in pairs:
        h0, h1 = 2 * hp, 2 * hp + 1
        i2 = diag2(i_b[:, _HEAD_SL[h0]], i_b[:, _HEAD_SL[h1]])
        s2 = diag2(_bf(s_old[h0]), _bf(s_old[h1]))
        o_2.append(_bdot(att[hp], i2) + _bdot_nt(q_in[:, pair_sl[hp]], s2))
    for h in heads:
        sl = _HEAD_SL[h]
        s_ref[h] = s_old[h] * buf["gall_h"][:, sl] + _bdot_tn(i_b[:, sl], k_dec[:, sl])
    yield
    for h in heads:
        sl = _HEAD_SL[h]
        o = o_2[h // 2][:, (h % 2) * LANES:(h % 2 + 1) * LANES] + buf["odiag"][:, sl]
        o = o * lax.rsqrt(jnp.mean(o * o, axis=-1, keepdims=True) + RMS_EPS)
        o_ref[0, rows, D_RWKV + h * LANES:D_RWKV + (h + 1) * LANES] = o * buf["gate_h"][:, sl]
    yield


_STAGE_ORDER = (0, 1, 0, 1, 0, 1, 0, 1, 0, 1, 0, 1, 1, 0, 1, 0, 1, 0)
_DONE = object()


def _mixer_kernel(layer, *refs):
    has_vmix = layer > 0
    n_in = 3 * (2 if has_vmix else 1)
    n_prm = 13 if has_vmix else 10
    proj_refs = refs[0:3]
    vf_refs = refs[3:6] if has_vmix else (None, None, None)
    prm = list(refs[n_in:n_in + n_prm])
    (mu_ref, w0_ref, wup_ref, a0_ref, aup_ref, kk_ref, ka_ref, rk_ref, gnw_ref, gnb_ref) = prm[:10]
    vm = prm[10:13] if has_vmix else [None, None, None]
    prep_prm = (mu_ref, w0_ref, wup_ref, a0_ref, aup_ref, kk_ref, ka_ref, rk_ref, *vm)
    pos = n_in + n_prm
    lbl_ref, gw_ref, o_ref = refs[pos:pos + 3]
    pos += 3
    vf_out_ref = None
    if not has_vmix:
        vf_out_ref = refs[pos]
        pos += 1
    s_r_ref, prev_ref, s_h_ref = refs[pos:pos + 3]
    pos += 3
    nb = len(_BUF_SHAPES)
    bufs = [dict(zip([n for n, _, _ in _BUF_SHAPES], refs[pos + i * nb:pos + (i + 1) * nb]))
            for i in range(2)]

    def prepare(first, proj_ref, vf_ref, buf):
        vf = vf_ref[0] if has_vmix else None
        return [_rwkv_prepare(first, has_vmix, proj_ref[0, :, 0:RWKV_COLS], vf, prep_prm,
                              prev_ref, buf),
                _hgrn_prepare(layer, proj_ref[0, :, RWKV_COLS:IN_COLS], lbl_ref, gw_ref, buf)]

    def chain(half, buf):
        gens = (_rwkv_chain(half, not has_vmix, buf, gnw_ref, gnb_ref, o_ref, vf_out_ref, s_r_ref),
                _hgrn_chain(half, buf, o_ref, s_h_ref))
        for g in _STAGE_ORDER:
            next(gens[g])
            yield

    def run(chain_gen, prep_gens):
        live = [chain_gen] + list(prep_gens)
        while live:
            for g in list(live):
                if next(g, _DONE) is _DONE:
                    live.remove(g)

    @pl.when(pl.program_id(1) == 0)
    def _():
        s_r_ref[...] = jnp.zeros_like(s_r_ref)
        s_h_ref[...] = jnp.zeros_like(s_h_ref)
        run(iter(()), prepare(True, proj_refs[0], vf_refs[0], bufs[0]))

    run(chain(0, bufs[0]), prepare(False, proj_refs[1], vf_refs[1], bufs[1]))
    run(chain(1, bufs[1]), prepare(False, proj_refs[2], vf_refs[2], bufs[0]))


def _mixer_layer(proj, layer, rwkv_params, vmix, v_first, lb_logits, g_norm_w):
    bsz, seq, _ = proj.shape
    has_vmix = layer > 0
    nc = seq // CHUNK
    row_spec = lambda n: pl.BlockSpec((1, n), lambda b, j: (0, 0))
    full_spec = lambda s: pl.BlockSpec(s, lambda b, j: (0, 0))
    chunk_specs = lambda n: [
        pl.BlockSpec((1, CHUNK, n), lambda b, j: (b, 0, 0)),
        pl.BlockSpec((1, CHUNK, n), lambda b, j: (b, 2 * j + 1, 0)),
        pl.BlockSpec((1, CHUNK, n), lambda b, j: (b, jnp.minimum(2 * j + 2, nc - 1), 0))]
    step_spec = lambda n: pl.BlockSpec((1, 2 * CHUNK, n), lambda b, j: (b, j, 0))
    args = [proj] * 3
    in_specs = chunk_specs(IN_COLS)
    if has_vmix:
        args += [v_first] * 3
        in_specs += chunk_specs(D_RWKV)
    args += list(rwkv_params)
    in_specs += [row_spec(RWKV_COLS), row_spec(D_RWKV), full_spec((DECAY_RANK, D_RWKV)),
                 row_spec(D_RWKV), full_spec((A_RANK, D_RWKV)), row_spec(D_RWKV),
                 row_spec(D_RWKV), row_spec(D_RWKV), row_spec(D_RWKV), row_spec(D_RWKV)]
    out_sds = jax.ShapeDtypeStruct((bsz, seq, D_MODEL), F32)
    if has_vmix:
        args += list(vmix)
        in_specs += [row_spec(D_RWKV), full_spec((D_RWKV, VRES_RANK)),
                     full_spec((VRES_RANK, D_RWKV))]
        out_shape, out_specs = out_sds, step_spec(D_MODEL)
    else:
        out_shape = (out_sds, jax.ShapeDtypeStruct((bsz, seq, D_RWKV), F32))
        out_specs = (step_spec(D_MODEL), step_spec(D_RWKV))
    args += [lb_logits, g_norm_w]
    in_specs += [full_spec((DEPTH, D_HGRN)), row_spec(D_HGRN)]
    scratch = [pltpu.VMEM((RWKV_PAIRS, PAIR_ROWS, LANES), F32),
               pltpu.VMEM((1, RWKV_COLS), F32),
               pltpu.VMEM((HGRN_HEADS, HGRN_EXPAND, LANES), F32)]
    scratch += [pltpu.VMEM(shape, dtype) for _ in range(2) for _, shape, dtype in _BUF_SHAPES]
    res = pl.pallas_call(
        functools.partial(_mixer_kernel, layer),
        out_shape=out_shape,
        grid=(bsz, nc // 2),
        in_specs=in_specs,
        out_specs=out_specs,
        scratch_shapes=scratch,
        compiler_params=pltpu.CompilerParams(
            dimension_semantics=("arbitrary", "arbitrary"), vmem_limit_bytes=VMEM_LIMIT),
        name="mixer",
    )(*args)
    if has_vmix:
        return res, v_first
    return res


def kernel(x, w_in, shift_mu, w_decay0, w_decay_up, a0, a_up, k_k, k_a, r_k, ln_x_w, ln_x_b,
           v_mix0, v_mix_down, v_mix_up, lb_logits, g_norm_w, w_out, ln_w, ln_b):
    out_dtype = x.dtype
    bsz, seq, _ = x.shape
    h = x.astype(F32).reshape(bsz * seq, D_MODEL)
    lb_logits = lb_logits.astype(F32)
    row = lambda t: t.reshape(1, -1)
    h_b = h.astype(jnp.bfloat16)
    v_first = None
    for l in range(DEPTH):
        proj = _in_proj(h_b, w_in, l).reshape(bsz, seq, IN_COLS)
        params = (row(shift_mu[l]), row(w_decay0[l]), w_decay_up[l], row(a0[l]), a_up[l],
                  row(k_k[l]), row(k_a[l]), row(r_k[l]), row(ln_x_w[l]), row(ln_x_b[l]))
        vmix = None if l == 0 else (row(v_mix0[l - 1]), v_mix_down[l - 1], v_mix_up[l - 1])
        o_mix, v_first = _mixer_layer(proj, l, params, vmix, v_first, lb_logits, row(g_norm_w[l]))
        last = l == DEPTH - 1
        res = _out_proj_ln(o_mix.reshape(bsz * seq, D_MODEL), h, w_out, l, row(ln_w[l]),
                           row(ln_b[l]), not last)
        h, h_b = (res, None) if last else res
    return h.reshape(bsz, seq, D_MODEL).astype(out_dtype)
```

```python
import functools
import math

import jax
import jax.numpy as jnp
from jax import lax
from jax.experimental import pallas as pl
from jax.experimental.pallas import tpu as pltpu

D_MODEL = 2048
DEPTH = 2
D_RWKV = D_MODEL // 2
D_HGRN = D_MODEL - D_RWKV
RWKV_HEAD = 64
DECAY_RANK = 64
A_RANK = 64
VRES_RANK = 32
HGRN_EXPAND = 128
HGRN_HEADS = D_HGRN // HGRN_EXPAND
RWKV_COLS = 4 * D_RWKV + DECAY_RANK + A_RANK
HGRN_COLS = 4 * D_HGRN
IN_COLS = RWKV_COLS + HGRN_COLS
ALPHA = (2 * DEPTH) ** 0.25
LN_EPS = 1e-5
GN_EPS = 64e-5
RMS_EPS = 1e-5
LB_FLOOR = 1e-30
LOG2E = math.log2(math.e)

LANES = 128
CHUNK = 64
PAIR_ROWS = 2 * CHUNK
RWKV_PAIRS = D_RWKV // LANES
VMEM_LIMIT = 56 * 1024 * 1024
IN_TM, IN_TN = 512, IN_COLS // 5
OUT_TM = 256

F32 = jnp.float32


def _bf(x):
    return x.astype(jnp.bfloat16)


def _bdot(a, b):
    return jnp.dot(_bf(a), _bf(b), preferred_element_type=F32)


def _bdot_nt(a, b):
    return lax.dot_general(_bf(a), _bf(b), (((1,), (1,)), ((), ())), preferred_element_type=F32)


def _bdot_tn(a, b):
    return lax.dot_general(_bf(a), _bf(b), (((0,), (0,)), ((), ())), preferred_element_type=F32)


def _split_dot(m01, x, terms):
    pieces = []
    rem = x
    for t in range(terms):
        pieces.append(_bf(rem))
        if t + 1 < terms:
            rem = rem - pieces[-1].astype(F32)
    return jnp.dot(_bf(m01), jnp.concatenate(pieces, axis=0), preferred_element_type=F32)


def _sigmoid(x):
    return 0.5 * jnp.tanh(0.5 * x) + 0.5


def _head_sums(xs, scale=1.0):
    rows, cols = xs[0].shape
    width = len(xs) * LANES
    ones = ((_iota2((width, width), 0) // RWKV_HEAD) == (_iota2((width, width), 1) // RWKV_HEAD))
    ones = jnp.where(ones, scale, 0.0).astype(jnp.bfloat16)
    tiles = range(cols // LANES)
    lhs = jnp.concatenate(
        [jnp.concatenate([x[:, t * LANES:(t + 1) * LANES] for t in tiles], axis=0) for x in xs],
        axis=1)
    sums = jnp.dot(_bf(lhs), ones, preferred_element_type=F32)
    return [jnp.concatenate([sums[t * rows:(t + 1) * rows, i * LANES:(i + 1) * LANES]
                             for t in tiles], axis=1) for i in range(len(xs))]


def _silu(x):
    return x * _sigmoid(x)


def _iota2(shape, dim):
    return lax.broadcasted_iota(jnp.int32, shape, dim)


def _mm_kernel(x_ref, w_ref, o_ref, wb_ref):
    @pl.when(pl.program_id(1) == 0)
    def _():
        wb_ref[...] = w_ref[...].astype(jnp.bfloat16)

    o_ref[...] = jnp.dot(x_ref[...].astype(jnp.bfloat16), wb_ref[...],
                         preferred_element_type=F32)


def _in_proj(x, w_in, layer):
    m, k = x.shape
    n = w_in.shape[-1]
    tm, tn = IN_TM, IN_TN
    return pl.pallas_call(
        _mm_kernel,
        out_shape=jax.ShapeDtypeStruct((m, n), F32),
        grid=(n // tn, m // tm),
        in_specs=[pl.BlockSpec((tm, k), lambda j, i: (i, 0)),
                  pl.BlockSpec((None, k, tn), lambda j, i: (layer, 0, j))],
        out_specs=pl.BlockSpec((tm, tn), lambda j, i: (i, j)),
        scratch_shapes=[pltpu.VMEM((k, tn), jnp.bfloat16)],
        compiler_params=pltpu.CompilerParams(
            dimension_semantics=("arbitrary", "arbitrary"), vmem_limit_bytes=VMEM_LIMIT),
        name="in_proj",
    )(x, w_in)


def _out_kernel(mix_ref, h_ref, w_ref, lnw_ref, lnb_ref, o_ref, wb_ref):
    @pl.when(pl.program_id(0) == 0)
    def _():
        wb_ref[...] = w_ref[...].astype(jnp.bfloat16)

    y = jnp.dot(mix_ref[...].astype(jnp.bfloat16), wb_ref[...], preferred_element_type=F32)
    u = ALPHA * h_ref[...] + y
    mu = jnp.mean(u, axis=-1, keepdims=True)
    d = u - mu
    var = jnp.mean(d * d, axis=-1, keepdims=True)
    o_ref[...] = d * lax.rsqrt(var + LN_EPS) * lnw_ref[...] + lnb_ref[...]


def _out_proj_ln(o_mix, h, w_out, layer, lnw, lnb):
    m = h.shape[0]
    tm = OUT_TM
    return pl.pallas_call(
        _out_kernel,
        out_shape=jax.ShapeDtypeStruct((m, D_MODEL), F32),
        grid=(m // tm,),
        in_specs=[pl.BlockSpec((tm, D_MODEL), lambda i: (i, 0)),
                  pl.BlockSpec((tm, D_MODEL), lambda i: (i, 0)),
                  pl.BlockSpec((None, D_MODEL, D_MODEL), lambda i: (layer, 0, 0),
                               pipeline_mode=pl.Buffered(1)),
                  pl.BlockSpec((1, D_MODEL), lambda i: (0, 0)),
                  pl.BlockSpec((1, D_MODEL), lambda i: (0, 0))],
        out_specs=pl.BlockSpec((tm, D_MODEL), lambda i: (i, 0)),
        scratch_shapes=[pltpu.VMEM((D_MODEL, D_MODEL), jnp.bfloat16)],
        compiler_params=pltpu.CompilerParams(
            dimension_semantics=("arbitrary",), vmem_limit_bytes=VMEM_LIMIT),
        name="out_proj_ln",
    )(o_mix, h, w_out, lnw, lnb)


_HGRN_LEVELS = tuple(CHUNK >> (i + 1) for i in range(int(math.log2(CHUNK))))
_N_LEVELS = len(_HGRN_LEVELS)
_PAIR_SL = [slice(p * LANES, (p + 1) * LANES) for p in range(RWKV_PAIRS)]
_HEAD_SL = [slice(h * LANES, (h + 1) * LANES) for h in range(HGRN_HEADS)]

_BUF_SHAPES = (
    ("ar", (RWKV_PAIRS, PAIR_ROWS, LANES), jnp.bfloat16),
    ("bk", (RWKV_PAIRS, 2 * PAIR_ROWS, LANES), jnp.bfloat16),
    ("vs", (RWKV_PAIRS, PAIR_ROWS, LANES), jnp.bfloat16),
    ("uv", (RWKV_PAIRS, PAIR_ROWS, LANES), jnp.bfloat16),
    ("bhkh", (RWKV_PAIRS, PAIR_ROWS, LANES), jnp.bfloat16),
    ("gall_r", (1, D_RWKV), F32),
    ("bonus", (CHUNK, D_RWKV), F32),
    ("gate_r", (CHUNK, D_RWKV), F32),
    ("vnat", (CHUNK, D_RWKV), F32),
    ("qe", (_N_LEVELS, CHUNK, D_HGRN), jnp.bfloat16),
    ("ke", (_N_LEVELS, CHUNK, D_HGRN), jnp.bfloat16),
    ("qin", (CHUNK, D_HGRN), jnp.bfloat16),
    ("kdec", (CHUNK, D_HGRN), jnp.bfloat16),
    ("ib", (CHUNK, D_HGRN), jnp.bfloat16),
    ("odiag", (CHUNK, D_HGRN), F32),
    ("gate_h", (CHUNK, D_HGRN), F32),
    ("gall_h", (1, D_HGRN), F32),
)


def _head_mask():
    lane = _iota2((PAIR_ROWS, LANES), 1)
    row = _iota2((PAIR_ROWS, LANES), 0)
    return ((lane < RWKV_HEAD) == (row < CHUNK)).astype(F32)


def _rwkv_prepare(first, has_vmix, y, vf, prm, prev_ref, buf):
    (mu_ref, w0_ref, wup_ref, a0_ref, aup_ref, kk_ref, ka_ref, rk_ref, v0_ref, vdn_ref,
     vup_ref) = prm
    rolled = pltpu.roll(y, shift=1, axis=0)
    top = rolled[0:8, :]
    prev = jnp.zeros_like(y[0:1, :]) if first else prev_ref[...]
    top = jnp.where(_iota2(top.shape, 0) == 0, prev, top)
    y_prev = jnp.concatenate([top, rolled[8:, :]], axis=0)
    prev_ref[...] = y[CHUNK - 1:CHUNK, :]
    xs = y + mu_ref[...] * (y_prev - y)

    r = xs[:, 0:D_RWKV]
    k = xs[:, D_RWKV:2 * D_RWKV]
    v = xs[:, 2 * D_RWKV:3 * D_RWKV]
    z = xs[:, 3 * D_RWKV:4 * D_RWKV]
    wd = xs[:, 4 * D_RWKV:4 * D_RWKV + DECAY_RANK]
    ad = xs[:, 4 * D_RWKV + DECAY_RANK:RWKV_COLS]

    w_raw = w0_ref[...] + _bdot(jnp.tanh(wd), wup_ref[...])
    logw = (-math.exp(-0.5) * LOG2E) * _sigmoid(w_raw)
    a = _sigmoid(a0_ref[...] + _bdot(ad, aup_ref[...]))
    if has_vmix:
        gate = _sigmoid(v0_ref[...] + _bdot(_bdot(v, vdn_ref[...]), vup_ref[...]))
        v = v + (vf - v) * gate
    else:
        buf["vnat"][...] = v
    buf["gate_r"][...] = _silu(z)
    yield

    kk = k * kk_ref[...]
    k = k * (1.0 + (a - 1.0) * ka_ref[...])
    kk_sq, rk_sum = _head_sums([kk * kk, r * k * rk_ref[...]])
    kk = kk * lax.rsqrt(jnp.maximum(kk_sq, 1e-24))
    b = kk * a
    buf["bonus"][...] = rk_sum * v
    yield

    tri = (_iota2((CHUNK, 3 * CHUNK), 1) % CHUNK <= _iota2((CHUNK, 3 * CHUNK), 0)).astype(F32)
    cl = _split_dot(tri, logw, 3)
    cl_last = cl[CHUNK - 1:CHUNK, :]
    g_inv = jnp.exp2(-cl)
    g_end = jnp.exp2(cl_last - cl)
    buf["gall_r"][...] = jnp.exp2(cl_last)
    a_t = _bf(kk * -jnp.exp2(cl - logw))
    r_t = _bf(r * jnp.exp2(cl))
    b_t = _bf(b * g_inv)
    k_t = _bf(k * g_inv)
    b_h = _bf(b * g_end)
    k_h = _bf(k * g_end)
    v_b = _bf(v)
    yield

    mask = _head_mask().astype(jnp.bfloat16)

    def stack(x, p):
        xp = x[:, _PAIR_SL[p]]
        return jnp.concatenate([xp, xp], axis=0) * mask

    for p in range(RWKV_PAIRS):
        sl = _PAIR_SL[p]
        buf["ar"][p] = jnp.concatenate([a_t[:, sl], r_t[:, sl]], axis=0)
        buf["bk"][p] = jnp.concatenate([stack(b_t, p), stack(k_t, p)], axis=0)
        buf["bhkh"][p] = jnp.concatenate([b_h[:, sl], k_h[:, sl]], axis=0)
        buf["vs"][p] = stack(v_b, p)
        buf["uv"][p, CHUNK:, :] = v_b[:, sl]
        if p % 2 == 1:
            yield


def _rwkv_chain(half, emit_v, buf, gnw_ref, gnb_ref, o_ref, vf_out_ref, s_ref):
    mask_b = _head_mask().astype(jnp.bfloat16)
    t_idx = _iota2((CHUNK, LANES), 0)
    s_idx = _iota2((CHUNK, LANES), 1) % CHUNK
    strict = (s_idx < t_idx).astype(F32)
    strict_b = strict.astype(jnp.bfloat16)
    incl_b = (s_idx <= t_idx).astype(jnp.bfloat16)
    eye = (s_idx == t_idx).astype(F32)
    prow = _iota2((PAIR_ROWS, LANES), 0)
    pcol = _iota2((PAIR_ROWS, LANES), 1)
    same_head = (prow // RWKV_HEAD) == (pcol // RWKV_HEAD)
    rows = slice(half * CHUNK, (half + 1) * CHUNK)

    def diag(x):
        return jnp.concatenate([x, x], axis=0) * mask_b

    pairs = range(RWKV_PAIRS)
    ar = [buf["ar"][p] for p in pairs]
    sc = [_bdot_nt(ar[p], buf["bk"][p]) for p in pairs]
    pw = [strict * sc[p][:CHUNK, :LANES] for p in pairs]
    tinv = [eye + pw[p] for p in pairs]
    a_kv = [jnp.concatenate([_bf(sc[p][:CHUNK, LANES:]) * strict_b,
                             _bf(sc[p][CHUNK:, LANES:]) * incl_b], axis=0) for p in pairs]
    a_rb = [_bf(sc[p][CHUNK:, :LANES]) * incl_b for p in pairs]
    pwb = [_bf(x) for x in pw]
    pwd = [diag(x) for x in pwb]
    yield
    for _ in range(int(math.log2(CHUNK)) - 1):
        pw = [_bdot(pwb[p], pwd[p]) for p in pairs]
        pwb = [_bf(x) for x in pw]
        pwd = [diag(x) for x in pwb]
        tinv = [tinv[p] + _bdot(tinv[p], pwd[p]) for p in pairs]
        yield

    s_old = [s_ref[p] for p in pairs]
    ars = [_bdot_nt(ar[p], s_old[p]) for p in pairs]
    akv = [_bdot(a_kv[p], buf["vs"][p]) for p in pairs]
    u = [_bf(_bdot(tinv[p], diag(_bf(ars[p][:CHUNK] + akv[p][:CHUNK])))) for p in pairs]
    for p in pairs:
        buf["uv"][p, :CHUNK, :] = u[p]
        upd = _bdot_tn(buf["uv"][p], buf["bhkh"][p])
        s_ref[p] = jnp.where(same_head, s_old[p] * buf["gall_r"][:, _PAIR_SL[p]] + upd, 0.0)
    o = jnp.concatenate([ars[p][CHUNK:] + _bdot(a_rb[p], diag(u[p])) + akv[p][CHUNK:]
                         for p in pairs], axis=1)
    yield

    d = o - _head_sums([o], 1.0 / RWKV_HEAD)[0]
    yield
    var = _head_sums([d * d], 1.0 / RWKV_HEAD)[0]
    on = d * lax.rsqrt(var + GN_EPS) * gnw_ref[...] + gnb_ref[...]
    o_ref[0, rows, 0:D_RWKV] = (on + buf["bonus"][...]) * buf["gate_r"][...]
    if emit_v:
        vf_out_ref[0, rows, :] = buf["vnat"][...]
    yield


def _hgrn_prepare(layer, hg, lbl_ref, gw_ref, buf):
    lg = lbl_ref[...]
    e = jnp.exp(lg - jnp.max(lg, axis=0, keepdims=True))
    sm = e / jnp.sum(e, axis=0, keepdims=True)
    lb = jnp.sum(sm[0:layer + 1, :], axis=0, keepdims=True) - sm[0:1, :]

    q = _silu(hg[:, 0:D_HGRN])
    f_raw = hg[:, D_HGRN:2 * D_HGRN]
    i_in = hg[:, 2 * D_HGRN:3 * D_HGRN]
    z = hg[:, 3 * D_HGRN:4 * D_HGRN]
    buf["gate_h"][...] = gw_ref[...] * _silu(z)
    buf["ib"][...] = _bf(i_in)

    sig = _sigmoid(f_raw)
    log_f = jnp.log2(jnp.maximum(lb, LB_FLOOR) + (1.0 - lb) * sig)
    k = (1.0 - lb) * (1.0 - sig)
    yield

    trow = _iota2((CHUNK, 2 * CHUNK), 0)
    tcol = _iota2((CHUNK, 2 * CHUNK), 1) % CHUNK
    mats = [tcol <= trow]
    for m in _HGRN_LEVELS:
        mid = (trow // (2 * m)) * (2 * m) + m
        after = trow >= mid
        mats.append((after & (tcol >= mid) & (tcol <= trow))
                    | (~after & (tcol > trow) & (tcol < mid)))
    cums = _split_dot(jnp.concatenate(mats, axis=0).astype(F32), log_f, 2)
    b = cums[0:CHUNK, :]
    b_last = b[CHUNK - 1:CHUNK, :]
    buf["qin"][...] = _bf(q * jnp.exp2(b))
    buf["kdec"][...] = _bf(k * jnp.exp2(b_last - b))
    buf["gall_h"][...] = jnp.exp2(b_last)
    q_b = _bf(q)
    k_b = _bf(k)
    yield
    for li in range(_N_LEVELS):
        e = _bf(jnp.exp2(cums[(li + 1) * CHUNK:(li + 2) * CHUNK, :]))
        buf["qe"][li] = q_b * e
        buf["ke"][li] = k_b * e
        yield
    qk = q * k
    for h in range(HGRN_HEADS):
        sl = _HEAD_SL[h]
        buf["odiag"][:, sl] = jnp.sum(qk[:, sl], axis=-1, keepdims=True) * i_in[:, sl]
    yield


def _hgrn_chain(half, buf, o_ref, s_ref):
    trow = _iota2((CHUNK, LANES), 0)
    tcol = _iota2((CHUNK, LANES), 1) % CHUNK
    rows = slice(half * CHUNK, (half + 1) * CHUNK)
    heads = range(HGRN_HEADS)
    pairs = range(HGRN_HEADS // 2)
    pair_sl = [slice(2 * hp * LANES, 2 * (hp + 1) * LANES) for hp in pairs]

    def diag2(x0, x1):
        return jnp.concatenate([jnp.concatenate([x0, jnp.zeros_like(x1)], axis=1),
                                jnp.concatenate([jnp.zeros_like(x0), x1], axis=1)], axis=0)

    att = [None] * len(pairs)
    yield
    for li, m in enumerate(_HGRN_LEVELS):
        mid = (trow // (2 * m)) * (2 * m) + m
        keep = ((trow >= mid) & (tcol < mid) & (tcol // (2 * m) == trow // (2 * m))).astype(F32)
        qe = buf["qe"][li]
        ke = buf["ke"][li]
        for hp in pairs:
            ke2 = diag2(ke[:, _HEAD_SL[2 * hp]], ke[:, _HEAD_SL[2 * hp + 1]])
            part = keep * _bdot_nt(qe[:, pair_sl[hp]], ke2)
            att[hp] = part if att[hp] is None else att[hp] + part
        yield
    s_old = [s_ref[h] for h in heads]
    i_b = buf["ib"][...]
    q_in = buf["qin"][...]
    k_dec = buf["kdec"][...]
    o_2 = []
    for hp in pairs:
        h0, h1 = 2 * hp, 2 * hp + 1
        i2 = diag2(i_b[:, _HEAD_SL[h0]], i_b[:, _HEAD_SL[h1]])
        s2 = diag2(_bf(s_old[h0]), _bf(s_old[h1]))
        o_2.append(_bdot(att[hp], i2) + _bdot_nt(q_in[:, pair_sl[hp]], s2))
    for h in heads:
        sl = _HEAD_SL[h]
        s_ref[h] = s_old[h] * buf["gall_h"][:, sl] + _bdot_tn(i_b[:, sl], k_dec[:, sl])
    yield
    for h in heads:
        sl = _HEAD_SL[h]
        o = o_2[h // 2][:, (h % 2) * LANES:(h % 2 + 1) * LANES] + buf["odiag"][:, sl]
        o = o * lax.rsqrt(jnp.mean(o * o, axis=-1, keepdims=True) + RMS_EPS)
        o_ref[0, rows, D_RWKV + h * LANES:D_RWKV + (h + 1) * LANES] = o * buf["gate_h"][:, sl]
    yield


_STAGE_ORDER = (0, 1, 0, 1, 0, 1, 0, 1, 0, 1, 0, 1, 1, 0, 1, 0, 1, 0)
_DONE = object()


def _mixer_kernel(layer, *refs):
    has_vmix = layer > 0
    n_in = 3 * (2 if has_vmix else 1)
    n_prm = 13 if has_vmix else 10
    proj_refs = refs[0:3]
    vf_refs = refs[3:6] if has_vmix else (None, None, None)
    prm = list(refs[n_in:n_in + n_prm])
    (mu_ref, w0_ref, wup_ref, a0_ref, aup_ref, kk_ref, ka_ref, rk_ref, gnw_ref, gnb_ref) = prm[:10]
    vm = prm[10:13] if has_vmix else [None, None, None]
    prep_prm = (mu_ref, w0_ref, wup_ref, a0_ref, aup_ref, kk_ref, ka_ref, rk_ref, *vm)
    pos = n_in + n_prm
    lbl_ref, gw_ref, o_ref = refs[pos:pos + 3]
    pos += 3
    vf_out_ref = None
    if not has_vmix:
        vf_out_ref = refs[pos]
        pos += 1
    s_r_ref, prev_ref, s_h_ref = refs[pos:pos + 3]
    pos += 3
    nb = len(_BUF_SHAPES)
    bufs = [dict(zip([n for n, _, _ in _BUF_SHAPES], refs[pos + i * nb:pos + (i + 1) * nb]))
            for i in range(2)]

    def prepare(first, proj_ref, vf_ref, buf):
        vf = vf_ref[0] if has_vmix else None
        return [_rwkv_prepare(first, has_vmix, proj_ref[0, :, 0:RWKV_COLS], vf, prep_prm,
                              prev_ref, buf),
                _hgrn_prepare(layer, proj_ref[0, :, RWKV_COLS:IN_COLS], lbl_ref, gw_ref, buf)]

    def chain(half, buf):
        gens = (_rwkv_chain(half, not has_vmix, buf, gnw_ref, gnb_ref, o_ref, vf_out_ref, s_r_ref),
                _hgrn_chain(half, buf, o_ref, s_h_ref))
        for g in _STAGE_ORDER:
            next(gens[g])
            yield

    def run(chain_gen, prep_gens):
        live = [chain_gen] + list(prep_gens)
        while live:
            for g in list(live):
                if next(g, _DONE) is _DONE:
                    live.remove(g)

    @pl.when(pl.program_id(1) == 0)
    def _():
        s_r_ref[...] = jnp.zeros_like(s_r_ref)
        s_h_ref[...] = jnp.zeros_like(s_h_ref)
        run(iter(()), prepare(True, proj_refs[0], vf_refs[0], bufs[0]))

    run(chain(0, bufs[0]), prepare(False, proj_refs[1], vf_refs[1], bufs[1]))
    run(chain(1, bufs[1]), prepare(False, proj_refs[2], vf_refs[2], bufs[0]))


def _mixer_layer(proj, layer, rwkv_params, vmix, v_first, lb_logits, g_norm_w):
    bsz, seq, _ = proj.shape
    has_vmix = layer > 0
    nc = seq // CHUNK
    row_spec = lambda n: pl.BlockSpec((1, n), lambda b, j: (0, 0))
    full_spec = lambda s: pl.BlockSpec(s, lambda b, j: (0, 0))
    chunk_specs = lambda n: [
        pl.BlockSpec((1, CHUNK, n), lambda b, j: (b, 0, 0)),
        pl.BlockSpec((1, CHUNK, n), lambda b, j: (b, 2 * j + 1, 0)),
        pl.BlockSpec((1, CHUNK, n), lambda b, j: (b, jnp.minimum(2 * j + 2, nc - 1), 0))]
    step_spec = lambda n: pl.BlockSpec((1, 2 * CHUNK, n), lambda b, j: (b, j, 0))
    args = [proj] * 3
    in_specs = chunk_specs(IN_COLS)
    if has_vmix:
        args += [v_first] * 3
        in_specs += chunk_specs(D_RWKV)
    args += list(rwkv_params)
    in_specs += [row_spec(RWKV_COLS), row_spec(D_RWKV), full_spec((DECAY_RANK, D_RWKV)),
                 row_spec(D_RWKV), full_spec((A_RANK, D_RWKV)), row_spec(D_RWKV),
                 row_spec(D_RWKV), row_spec(D_RWKV), row_spec(D_RWKV), row_spec(D_RWKV)]
    out_sds = jax.ShapeDtypeStruct((bsz, seq, D_MODEL), F32)
    if has_vmix:
        args += list(vmix)
        in_specs += [row_spec(D_RWKV), full_spec((D_RWKV, VRES_RANK)),
                     full_spec((VRES_RANK, D_RWKV))]
        out_shape, out_specs = out_sds, step_spec(D_MODEL)
    else:
        out_shape = (out_sds, jax.ShapeDtypeStruct((bsz, seq, D_RWKV), F32))
        out_specs = (step_spec(D_MODEL), step_spec(D_RWKV))
    args += [lb_logits, g_norm_w]
    in_specs += [full_spec((DEPTH, D_HGRN)), row_spec(D_HGRN)]
    scratch = [pltpu.VMEM((RWKV_PAIRS, PAIR_ROWS, LANES), F32),
               pltpu.VMEM((1, RWKV_COLS), F32),
               pltpu.VMEM((HGRN_HEADS, HGRN_EXPAND, LANES), F32)]
    scratch += [pltpu.VMEM(shape, dtype) for _ in range(2) for _, shape, dtype in _BUF_SHAPES]
    res = pl.pallas_call(
        functools.partial(_mixer_kernel, layer),
        out_shape=out_shape,
        grid=(bsz, nc // 2),
        in_specs=in_specs,
        out_specs=out_specs,
        scratch_shapes=scratch,
        compiler_params=pltpu.CompilerParams(
            dimension_semantics=("arbitrary", "arbitrary"), vmem_limit_bytes=VMEM_LIMIT),
        name="mixer",
    )(*args)
    if has_vmix:
        return res, v_first
    return res


def kernel(x, w_in, shift_mu, w_decay0, w_decay_up, a0, a_up, k_k, k_a, r_k, ln_x_w, ln_x_b,
           v_mix0, v_mix_down, v_mix_up, lb_logits, g_norm_w, w_out, ln_w, ln_b):
    out_dtype = x.dtype
    bsz, seq, _ = x.shape
    h = x.astype(F32).reshape(bsz * seq, D_MODEL)
    lb_logits = lb_logits.astype(F32)
    row = lambda t: t.reshape(1, -1)
    v_first = None
    for l in range(DEPTH):
        proj = _in_proj(h, w_in, l).reshape(bsz, seq, IN_COLS)
        params = (row(shift_mu[l]), row(w_decay0[l]), w_decay_up[l], row(a0[l]), a_up[l],
                  row(k_k[l]), row(k_a[l]), row(r_k[l]), row(ln_x_w[l]), row(ln_x_b[l]))
        vmix = None if l == 0 else (row(v_mix0[l - 1]), v_mix_down[l - 1], v_mix_up[l - 1])
        o_mix, v_first = _mixer_layer(proj, l, params, vmix, v_first, lb_logits, row(g_norm_w[l]))
        h = _out_proj_ln(o_mix.reshape(bsz * seq, D_MODEL), h, w_out, l, row(ln_w[l]),
                         row(ln_b[l]))
    return h.reshape(bsz, seq, D_MODEL).astype(out_dtype)
```

```python
import functools
import math

import jax
import jax.numpy as jnp
from jax import lax
from jax.experimental import pallas as pl
from jax.experimental.pallas import tpu as pltpu

D_MODEL = 2048
DEPTH = 2
D_RWKV = D_MODEL // 2
D_HGRN = D_MODEL - D_RWKV
RWKV_HEAD = 64
DECAY_RANK = 64
A_RANK = 64
VRES_RANK = 32
HGRN_EXPAND = 128
HGRN_HEADS = D_HGRN // HGRN_EXPAND
RWKV_COLS = 4 * D_RWKV + DECAY_RANK + A_RANK
HGRN_COLS = 4 * D_HGRN
IN_COLS = RWKV_COLS + HGRN_COLS
ALPHA = (2 * DEPTH) ** 0.25
LN_EPS = 1e-5
GN_EPS = 64e-5
RMS_EPS = 1e-5
LB_FLOOR = 1e-30
LOG2E = math.log2(math.e)

LANES = 128
CHUNK = 64
PAIR_ROWS = 2 * CHUNK
RWKV_PAIRS = D_RWKV // LANES
VMEM_LIMIT = 56 * 1024 * 1024
IN_TM, IN_TN = 512, IN_COLS // 5
OUT_TM = 256

F32 = jnp.float32


def _bf(x):
    return x.astype(jnp.bfloat16)


def _bdot(a, b):
    return jnp.dot(_bf(a), _bf(b), preferred_element_type=F32)


def _bdot_nt(a, b):
    return lax.dot_general(_bf(a), _bf(b), (((1,), (1,)), ((), ())), preferred_element_type=F32)


def _bdot_tn(a, b):
    return lax.dot_general(_bf(a), _bf(b), (((0,), (0,)), ((), ())), preferred_element_type=F32)


def _split_dot(m01, x, terms):
    pieces = []
    rem = x
    for t in range(terms):
        pieces.append(_bf(rem))
        if t + 1 < terms:
            rem = rem - pieces[-1].astype(F32)
    return jnp.dot(_bf(m01), jnp.concatenate(pieces, axis=0), preferred_element_type=F32)


def _sigmoid(x):
    return 0.5 * jnp.tanh(0.5 * x) + 0.5


def _head_sums(xs, scale=1.0):
    rows, cols = xs[0].shape
    width = len(xs) * LANES
    ones = ((_iota2((width, width), 0) // RWKV_HEAD) == (_iota2((width, width), 1) // RWKV_HEAD))
    ones = jnp.where(ones, scale, 0.0).astype(jnp.bfloat16)
    tiles = range(cols // LANES)
    lhs = jnp.concatenate(
        [jnp.concatenate([x[:, t * LANES:(t + 1) * LANES] for t in tiles], axis=0) for x in xs],
        axis=1)
    sums = jnp.dot(_bf(lhs), ones, preferred_element_type=F32)
    return [jnp.concatenate([sums[t * rows:(t + 1) * rows, i * LANES:(i + 1) * LANES]
                             for t in tiles], axis=1) for i in range(len(xs))]


def _silu(x):
    return x * _sigmoid(x)


def _iota2(shape, dim):
    return lax.broadcasted_iota(jnp.int32, shape, dim)


def _mm_kernel(x_ref, w_ref, o_ref, wb_ref):
    @pl.when(pl.program_id(1) == 0)
    def _():
        wb_ref[...] = w_ref[...].astype(jnp.bfloat16)

    o_ref[...] = jnp.dot(x_ref[...].astype(jnp.bfloat16), wb_ref[...],
                         preferred_element_type=F32)


def _in_proj(x, w_in, layer):
    m, k = x.shape
    n = w_in.shape[-1]
    tm, tn = IN_TM, IN_TN
    return pl.pallas_call(
        _mm_kernel,
        out_shape=jax.ShapeDtypeStruct((m, n), F32),
        grid=(n // tn, m // tm),
        in_specs=[pl.BlockSpec((tm, k), lambda j, i: (i, 0)),
                  pl.BlockSpec((None, k, tn), lambda j, i: (layer, 0, j))],
        out_specs=pl.BlockSpec((tm, tn), lambda j, i: (i, j)),
        scratch_shapes=[pltpu.VMEM((k, tn), jnp.bfloat16)],
        compiler_params=pltpu.CompilerParams(
            dimension_semantics=("arbitrary", "arbitrary"), vmem_limit_bytes=VMEM_LIMIT),
        name="in_proj",
    )(x, w_in)


def _out_kernel(mix_ref, h_ref, w_ref, lnw_ref, lnb_ref, o_ref, wb_ref):
    @pl.when(pl.program_id(0) == 0)
    def _():
        wb_ref[...] = w_ref[...].astype(jnp.bfloat16)

    y = jnp.dot(mix_ref[...].astype(jnp.bfloat16), wb_ref[...], preferred_element_type=F32)
    u = ALPHA * h_ref[...] + y
    mu = jnp.mean(u, axis=-1, keepdims=True)
    d = u - mu
    var = jnp.mean(d * d, axis=-1, keepdims=True)
    o_ref[...] = d * lax.rsqrt(var + LN_EPS) * lnw_ref[...] + lnb_ref[...]


def _out_proj_ln(o_mix, h, w_out, layer, lnw, lnb):
    m = h.shape[0]
    tm = OUT_TM
    return pl.pallas_call(
        _out_kernel,
        out_shape=jax.ShapeDtypeStruct((m, D_MODEL), F32),
        grid=(m // tm,),
        in_specs=[pl.BlockSpec((tm, D_MODEL), lambda i: (i, 0)),
                  pl.BlockSpec((tm, D_MODEL), lambda i: (i, 0)),
                  pl.BlockSpec((None, D_MODEL, D_MODEL), lambda i: (layer, 0, 0),
                               pipeline_mode=pl.Buffered(1)),
                  pl.BlockSpec((1, D_MODEL), lambda i: (0, 0)),
                  pl.BlockSpec((1, D_MODEL), lambda i: (0, 0))],
        out_specs=pl.BlockSpec((tm, D_MODEL), lambda i: (i, 0)),
        scratch_shapes=[pltpu.VMEM((D_MODEL, D_MODEL), jnp.bfloat16)],
        compiler_params=pltpu.CompilerParams(
            dimension_semantics=("arbitrary",), vmem_limit_bytes=VMEM_LIMIT),
        name="out_proj_ln",
    )(o_mix, h, w_out, lnw, lnb)


_HGRN_LEVELS = tuple(CHUNK >> (i + 1) for i in range(int(math.log2(CHUNK))))
_N_LEVELS = len(_HGRN_LEVELS)
_PAIR_SL = [slice(p * LANES, (p + 1) * LANES) for p in range(RWKV_PAIRS)]
_HEAD_SL = [slice(h * LANES, (h + 1) * LANES) for h in range(HGRN_HEADS)]

_BUF_SHAPES = (
    ("ar", (RWKV_PAIRS, PAIR_ROWS, LANES), jnp.bfloat16),
    ("bkT", (RWKV_PAIRS, LANES, 2 * PAIR_ROWS), jnp.bfloat16),
    ("vs", (RWKV_PAIRS, PAIR_ROWS, LANES), jnp.bfloat16),
    ("uv", (RWKV_PAIRS, PAIR_ROWS, LANES), jnp.bfloat16),
    ("bhkh", (RWKV_PAIRS, PAIR_ROWS, LANES), jnp.bfloat16),
    ("gall_r", (1, D_RWKV), F32),
    ("bonus", (CHUNK, D_RWKV), F32),
    ("gate_r", (CHUNK, D_RWKV), F32),
    ("vnat", (CHUNK, D_RWKV), F32),
    ("qe", (_N_LEVELS, CHUNK, D_HGRN), jnp.bfloat16),
    ("keT", (_N_LEVELS, HGRN_HEADS // 2, 2 * LANES, PAIR_ROWS), jnp.bfloat16),
    ("qin", (CHUNK, D_HGRN), jnp.bfloat16),
    ("kdec", (CHUNK, D_HGRN), jnp.bfloat16),
    ("ib", (CHUNK, D_HGRN), jnp.bfloat16),
    ("odiag", (CHUNK, D_HGRN), F32),
    ("gate_h", (CHUNK, D_HGRN), F32),
    ("gall_h", (1, D_HGRN), F32),
)


def _head_mask():
    lane = _iota2((PAIR_ROWS, LANES), 1)
    row = _iota2((PAIR_ROWS, LANES), 0)
    return ((lane < RWKV_HEAD) == (row < CHUNK)).astype(F32)


def _rwkv_prepare(first, has_vmix, y, vf, prm, prev_ref, buf):
    (mu_ref, w0_ref, wup_ref, a0_ref, aup_ref, kk_ref, ka_ref, rk_ref, v0_ref, vdn_ref,
     vup_ref) = prm
    rolled = pltpu.roll(y, shift=1, axis=0)
    top = rolled[0:8, :]
    prev = jnp.zeros_like(y[0:1, :]) if first else prev_ref[...]
    top = jnp.where(_iota2(top.shape, 0) == 0, prev, top)
    y_prev = jnp.concatenate([top, rolled[8:, :]], axis=0)
    prev_ref[...] = y[CHUNK - 1:CHUNK, :]
    xs = y + mu_ref[...] * (y_prev - y)

    r = xs[:, 0:D_RWKV]
    k = xs[:, D_RWKV:2 * D_RWKV]
    v = xs[:, 2 * D_RWKV:3 * D_RWKV]
    z = xs[:, 3 * D_RWKV:4 * D_RWKV]
    wd = xs[:, 4 * D_RWKV:4 * D_RWKV + DECAY_RANK]
    ad = xs[:, 4 * D_RWKV + DECAY_RANK:RWKV_COLS]

    w_raw = w0_ref[...] + _bdot(jnp.tanh(wd), wup_ref[...])
    logw = (-math.exp(-0.5) * LOG2E) * _sigmoid(w_raw)
    a = _sigmoid(a0_ref[...] + _bdot(ad, aup_ref[...]))
    if has_vmix:
        gate = _sigmoid(v0_ref[...] + _bdot(_bdot(v, vdn_ref[...]), vup_ref[...]))
        v = v + (vf - v) * gate
    else:
        buf["vnat"][...] = v
    buf["gate_r"][...] = _silu(z)
    yield

    kk = k * kk_ref[...]
    k = k * (1.0 + (a - 1.0) * ka_ref[...])
    kk_sq, rk_sum = _head_sums([kk * kk, r * k * rk_ref[...]])
    kk = kk * lax.rsqrt(jnp.maximum(kk_sq, 1e-24))
    b = kk * a
    buf["bonus"][...] = rk_sum * v
    yield

    tri = (_iota2((CHUNK, 3 * CHUNK), 1) % CHUNK <= _iota2((CHUNK, 3 * CHUNK), 0)).astype(F32)
    cl = _split_dot(tri, logw, 3)
    cl_last = cl[CHUNK - 1:CHUNK, :]
    g_inv = jnp.exp2(-cl)
    g_end = jnp.exp2(cl_last - cl)
    buf["gall_r"][...] = jnp.exp2(cl_last)
    a_t = _bf(kk * -jnp.exp2(cl - logw))
    r_t = _bf(r * jnp.exp2(cl))
    b_t = _bf(b * g_inv)
    k_t = _bf(k * g_inv)
    b_h = _bf(b * g_end)
    k_h = _bf(k * g_end)
    v_b = _bf(v)
    yield

    mask = _head_mask().astype(jnp.bfloat16)

    def stack(x, p):
        xp = x[:, _PAIR_SL[p]]
        return jnp.concatenate([xp, xp], axis=0) * mask

    for p in range(RWKV_PAIRS):
        sl = _PAIR_SL[p]
        buf["ar"][p] = jnp.concatenate([a_t[:, sl], r_t[:, sl]], axis=0)
        buf["bkT"][p] = jnp.concatenate([stack(b_t, p), stack(k_t, p)], axis=0).T
        buf["bhkh"][p] = jnp.concatenate([b_h[:, sl], k_h[:, sl]], axis=0)
        buf["vs"][p] = stack(v_b, p)
        buf["uv"][p, CHUNK:, :] = v_b[:, sl]
        if p % 2 == 1:
            yield


def _rwkv_chain(half, emit_v, buf, gnw_ref, gnb_ref, o_ref, vf_out_ref, s_ref):
    mask_b = _head_mask().astype(jnp.bfloat16)
    t_idx = _iota2((CHUNK, LANES), 0)
    s_idx = _iota2((CHUNK, LANES), 1) % CHUNK
    strict = (s_idx < t_idx).astype(F32)
    strict_b = strict.astype(jnp.bfloat16)
    incl_b = (s_idx <= t_idx).astype(jnp.bfloat16)
    eye = (s_idx == t_idx).astype(F32)
    prow = _iota2((PAIR_ROWS, LANES), 0)
    pcol = _iota2((PAIR_ROWS, LANES), 1)
    same_head = (prow // RWKV_HEAD) == (pcol // RWKV_HEAD)
    rows = slice(half * CHUNK, (half + 1) * CHUNK)

    def diag(x):
        return jnp.concatenate([x, x], axis=0) * mask_b

    pairs = range(RWKV_PAIRS)
    ar = [buf["ar"][p] for p in pairs]
    sc = [_bdot(ar[p], buf["bkT"][p]) for p in pairs]
    pw = [strict * sc[p][:CHUNK, :LANES] for p in pairs]
    tinv = [eye + pw[p] for p in pairs]
    a_kv = [jnp.concatenate([_bf(sc[p][:CHUNK, LANES:]) * strict_b,
                             _bf(sc[p][CHUNK:, LANES:]) * incl_b], axis=0) for p in pairs]
    a_rb = [_bf(sc[p][CHUNK:, :LANES]) * incl_b for p in pairs]
    pwb = [_bf(x) for x in pw]
    pwd = [diag(x) for x in pwb]
    s_old = [s_ref[p] for p in pairs]
    ars = [_bdot_nt(ar[p], s_old[p]) for p in pairs]
    akv = [_bdot(a_kv[p], buf["vs"][p]) for p in pairs]
    yield
    for it in range(int(math.log2(CHUNK)) - 1):
        pw = [_bdot(pwb[p], pwd[p]) for p in pairs]
        if it > 0:
            tinv = [tinv[p] + _bdot(tinv[p], pwd[p]) for p in pairs]
        pwb = [_bf(x) for x in pw]
        pwd = [diag(x) for x in pwb]
        yield
    tinv = [tinv[p] + _bdot(tinv[p], pwd[p]) for p in pairs]
    yield

    u = [_bf(_bdot(tinv[p], diag(_bf(ars[p][:CHUNK] + akv[p][:CHUNK])))) for p in pairs]
    yield
    for p in pairs:
        buf["uv"][p, :CHUNK, :] = u[p]
        upd = _bdot_tn(buf["uv"][p], buf["bhkh"][p])
        s_ref[p] = jnp.where(same_head, s_old[p] * buf["gall_r"][:, _PAIR_SL[p]] + upd, 0.0)
    o = jnp.concatenate([ars[p][CHUNK:] + _bdot(a_rb[p], diag(u[p])) + akv[p][CHUNK:]
                         for p in pairs], axis=1)
    yield

    d = o - _head_sums([o], 1.0 / RWKV_HEAD)[0]
    yield
    var = _head_sums([d * d], 1.0 / RWKV_HEAD)[0]
    on = d * lax.rsqrt(var + GN_EPS) * gnw_ref[...] + gnb_ref[...]
    o_ref[0, rows, 0:D_RWKV] = (on + buf["bonus"][...]) * buf["gate_r"][...]
    if emit_v:
        vf_out_ref[0, rows, :] = buf["vnat"][...]
    yield


def _hgrn_prepare(layer, hg, lbl_ref, gw_ref, buf):
    lg = lbl_ref[...]
    e = jnp.exp(lg - jnp.max(lg, axis=0, keepdims=True))
    sm = e / jnp.sum(e, axis=0, keepdims=True)
    lb = jnp.sum(sm[0:layer + 1, :], axis=0, keepdims=True) - sm[0:1, :]

    q = _silu(hg[:, 0:D_HGRN])
    f_raw = hg[:, D_HGRN:2 * D_HGRN]
    i_in = hg[:, 2 * D_HGRN:3 * D_HGRN]
    z = hg[:, 3 * D_HGRN:4 * D_HGRN]
    buf["gate_h"][...] = gw_ref[...] * _silu(z)
    buf["ib"][...] = _bf(i_in)

    sig = _sigmoid(f_raw)
    log_f = jnp.log2(jnp.maximum(lb, LB_FLOOR) + (1.0 - lb) * sig)
    k = (1.0 - lb) * (1.0 - sig)
    yield

    trow = _iota2((CHUNK, 2 * CHUNK), 0)
    tcol = _iota2((CHUNK, 2 * CHUNK), 1) % CHUNK
    mats = [tcol <= trow]
    for m in _HGRN_LEVELS:
        mid = (trow // (2 * m)) * (2 * m) + m
        after = trow >= mid
        mats.append((after & (tcol >= mid) & (tcol <= trow))
                    | (~after & (tcol > trow) & (tcol < mid)))
    cums = _split_dot(jnp.concatenate(mats, axis=0).astype(F32), log_f, 2)
    b = cums[0:CHUNK, :]
    b_last = b[CHUNK - 1:CHUNK, :]
    buf["qin"][...] = _bf(q * jnp.exp2(b))
    buf["kdec"][...] = _bf(k * jnp.exp2(b_last - b))
    buf["gall_h"][...] = jnp.exp2(b_last)
    q_b = _bf(q)
    k_b = _bf(k)
    yield
    for li in range(_N_LEVELS):
        e = _bf(jnp.exp2(cums[(li + 1) * CHUNK:(li + 2) * CHUNK, :]))
        buf["qe"][li] = q_b * e
        ke = k_b * e
        for hp in range(HGRN_HEADS // 2):
            buf["keT"][li, hp] = _diag2(ke[:, _HEAD_SL[2 * hp]], ke[:, _HEAD_SL[2 * hp + 1]]).T
        yield
    qk = q * k
    for h in range(HGRN_HEADS):
        sl = _HEAD_SL[h]
        buf["odiag"][:, sl] = jnp.sum(qk[:, sl], axis=-1, keepdims=True) * i_in[:, sl]
    yield


def _diag2(x0, x1):
    return jnp.concatenate([jnp.concatenate([x0, jnp.zeros_like(x1)], axis=1),
                            jnp.concatenate([jnp.zeros_like(x0), x1], axis=1)], axis=0)


def _hgrn_chain(half, buf, o_ref, s_ref):
    trow = _iota2((CHUNK, LANES), 0)
    tcol = _iota2((CHUNK, LANES), 1) % CHUNK
    rows = slice(half * CHUNK, (half + 1) * CHUNK)
    heads = range(HGRN_HEADS)
    pairs = range(HGRN_HEADS // 2)
    pair_sl = [slice(2 * hp * LANES, 2 * (hp + 1) * LANES) for hp in pairs]

    att = [None] * len(pairs)
    yield
    for li, m in enumerate(_HGRN_LEVELS):
        mid = (trow // (2 * m)) * (2 * m) + m
        keep = ((trow >= mid) & (tcol < mid) & (tcol // (2 * m) == trow // (2 * m))).astype(F32)
        qe = buf["qe"][li]
        for hp in pairs:
            part = keep * _bdot(qe[:, pair_sl[hp]], buf["keT"][li, hp])
            att[hp] = part if att[hp] is None else att[hp] + part
        yield
    s_old = [s_ref[h] for h in heads]
    i_b = buf["ib"][...]
    q_in = buf["qin"][...]
    k_dec = buf["kdec"][...]
    o_2 = []
    for hp in pairs:
        h0, h1 = 2 * hp, 2 * hp + 1
        i2 = _diag2(i_b[:, _HEAD_SL[h0]], i_b[:, _HEAD_SL[h1]])
        inter = [_bdot_nt(q_in[:, _HEAD_SL[h]], s_old[h]) for h in (h0, h1)]
        o_2.append(_bdot(att[hp], i2) + jnp.concatenate(inter, axis=1))
    for h in heads:
        sl = _HEAD_SL[h]
        s_ref[h] = s_old[h] * buf["gall_h"][:, sl] + _bdot_tn(i_b[:, sl], k_dec[:, sl])
    yield
    for h in heads:
        sl = _HEAD_SL[h]
        o = o_2[h // 2][:, (h % 2) * LANES:(h % 2 + 1) * LANES] + buf["odiag"][:, sl]
        o = o * lax.rsqrt(jnp.mean(o * o, axis=-1, keepdims=True) + RMS_EPS)
        o_ref[0, rows, D_RWKV + h * LANES:D_RWKV + (h + 1) * LANES] = o * buf["gate_h"][:, sl]
    yield


_STAGE_ORDER = (0, 1, 0, 1, 0, 1, 0, 1, 0, 1, 0, 1, 0, 1, 0, 1, 0, 1, 0, 0)
_DONE = object()


def _mixer_kernel(layer, *refs):
    has_vmix = layer > 0
    n_in = 3 * (2 if has_vmix else 1)
    n_prm = 13 if has_vmix else 10
    proj_refs = refs[0:3]
    vf_refs = refs[3:6] if has_vmix else (None, None, None)
    prm = list(refs[n_in:n_in + n_prm])
    (mu_ref, w0_ref, wup_ref, a0_ref, aup_ref, kk_ref, ka_ref, rk_ref, gnw_ref, gnb_ref) = prm[:10]
    vm = prm[10:13] if has_vmix else [None, None, None]
    prep_prm = (mu_ref, w0_ref, wup_ref, a0_ref, aup_ref, kk_ref, ka_ref, rk_ref, *vm)
    pos = n_in + n_prm
    lbl_ref, gw_ref, o_ref = refs[pos:pos + 3]
    pos += 3
    vf_out_ref = None
    if not has_vmix:
        vf_out_ref = refs[pos]
        pos += 1
    s_r_ref, prev_ref, s_h_ref = refs[pos:pos + 3]
    pos += 3
    nb = len(_BUF_SHAPES)
    bufs = [dict(zip([n for n, _, _ in _BUF_SHAPES], refs[pos + i * nb:pos + (i + 1) * nb]))
            for i in range(2)]

    def prepare(first, proj_ref, vf_ref, buf):
        vf = vf_ref[0] if has_vmix else None
        return [_rwkv_prepare(first, has_vmix, proj_ref[0, :, 0:RWKV_COLS], vf, prep_prm,
                              prev_ref, buf),
                _hgrn_prepare(layer, proj_ref[0, :, RWKV_COLS:IN_COLS], lbl_ref, gw_ref, buf)]

    def chain(half, buf):
        gens = (_rwkv_chain(half, not has_vmix, buf, gnw_ref, gnb_ref, o_ref, vf_out_ref, s_r_ref),
                _hgrn_chain(half, buf, o_ref, s_h_ref))
        for g in _STAGE_ORDER:
            next(gens[g])
            yield

    def run(chain_gen, prep_gens):
        live = [chain_gen] + list(prep_gens)
        while live:
            for g in list(live):
                if next(g, _DONE) is _DONE:
                    live.remove(g)

    @pl.when(pl.program_id(1) == 0)
    def _():
        s_r_ref[...] = jnp.zeros_like(s_r_ref)
        s_h_ref[...] = jnp.zeros_like(s_h_ref)
        run(iter(()), prepare(True, proj_refs[0], vf_refs[0], bufs[0]))

    run(chain(0, bufs[0]), prepare(False, proj_refs[1], vf_refs[1], bufs[1]))
    run(chain(1, bufs[1]), prepare(False, proj_refs[2], vf_refs[2], bufs[0]))


def _mixer_layer(proj, layer, rwkv_params, vmix, v_first, lb_logits, g_norm_w):
    bsz, seq, _ = proj.shape
    has_vmix = layer > 0
    nc = seq // CHUNK
    row_spec = lambda n: pl.BlockSpec((1, n), lambda b, j: (0, 0))
    full_spec = lambda s: pl.BlockSpec(s, lambda b, j: (0, 0))
    chunk_specs = lambda n: [
        pl.BlockSpec((1, CHUNK, n), lambda b, j: (b, 0, 0)),
        pl.BlockSpec((1, CHUNK, n), lambda b, j: (b, 2 * j + 1, 0)),
        pl.BlockSpec((1, CHUNK, n), lambda b, j: (b, jnp.minimum(2 * j + 2, nc - 1), 0))]
    step_spec = lambda n: pl.BlockSpec((1, 2 * CHUNK, n), lambda b, j: (b, j, 0))
    args = [proj] * 3
    in_specs = chunk_specs(IN_COLS)
    if has_vmix:
        args += [v_first] * 3
        in_specs += chunk_specs(D_RWKV)
    args += list(rwkv_params)
    in_specs += [row_spec(RWKV_COLS), row_spec(D_RWKV), full_spec((DECAY_RANK, D_RWKV)),
                 row_spec(D_RWKV), full_spec((A_RANK, D_RWKV)), row_spec(D_RWKV),
                 row_spec(D_RWKV), row_spec(D_RWKV), row_spec(D_RWKV), row_spec(D_RWKV)]
    out_sds = jax.ShapeDtypeStruct((bsz, seq, D_MODEL), F32)
    if has_vmix:
        args += list(vmix)
        in_specs += [row_spec(D_RWKV), full_spec((D_RWKV, VRES_RANK)),
                     full_spec((VRES_RANK, D_RWKV))]
        out_shape, out_specs = out_sds, step_spec(D_MODEL)
    else:
        out_shape = (out_sds, jax.ShapeDtypeStruct((bsz, seq, D_RWKV), F32))
        out_specs = (step_spec(D_MODEL), step_spec(D_RWKV))
    args += [lb_logits, g_norm_w]
    in_specs += [full_spec((DEPTH, D_HGRN)), row_spec(D_HGRN)]
    scratch = [pltpu.VMEM((RWKV_PAIRS, PAIR_ROWS, LANES), F32),
               pltpu.VMEM((1, RWKV_COLS), F32),
               pltpu.VMEM((HGRN_HEADS, HGRN_EXPAND, LANES), F32)]
    scratch += [pltpu.VMEM(shape, dtype) for _ in range(2) for _, shape, dtype in _BUF_SHAPES]
    res = pl.pallas_call(
        functools.partial(_mixer_kernel, layer),
        out_shape=out_shape,
        grid=(bsz, nc // 2),
        in_specs=in_specs,
        out_specs=out_specs,
        scratch_shapes=scratch,
        compiler_params=pltpu.CompilerParams(
            dimension_semantics=("arbitrary", "arbitrary"), vmem_limit_bytes=VMEM_LIMIT),
        name="mixer",
    )(*args)
    if has_vmix:
        return res, v_first
    return res


def kernel(x, w_in, shift_mu, w_decay0, w_decay_up, a0, a_up, k_k, k_a, r_k, ln_x_w, ln_x_b,
           v_mix0, v_mix_down, v_mix_up, lb_logits, g_norm_w, w_out, ln_w, ln_b):
    out_dtype = x.dtype
    bsz, seq, _ = x.shape
    h = x.astype(F32).reshape(bsz * seq, D_MODEL)
    lb_logits = lb_logits.astype(F32)
    row = lambda t: t.reshape(1, -1)
    v_first = None
    for l in range(DEPTH):
        proj = _in_proj(h, w_in, l).reshape(bsz, seq, IN_COLS)
        params = (row(shift_mu[l]), row(w_decay0[l]), w_decay_up[l], row(a0[l]), a_up[l],
                  row(k_k[l]), row(k_a[l]), row(r_k[l]), row(ln_x_w[l]), row(ln_x_b[l]))
        vmix = None if l == 0 else (row(v_mix0[l - 1]), v_mix_down[l - 1], v_mix_up[l - 1])
        o_mix, v_first = _mixer_layer(proj, l, params, vmix, v_first, lb_logits, row(g_norm_w[l]))
        h = _out_proj_ln(o_mix.reshape(bsz * seq, D_MODEL), h, w_out, l, row(ln_w[l]),
                         row(ln_b[l]))
    return h.reshape(bsz, seq, D_MODEL).astype(out_dtype)
```

```python
import functools
import math

import jax
import jax.numpy as jnp
from jax import lax
from jax.experimental import pallas as pl
from jax.experimental.pallas import tpu as pltpu

D_MODEL = 2048
DEPTH = 2
D_RWKV = D_MODEL // 2
D_HGRN = D_MODEL - D_RWKV
RWKV_HEAD = 64
DECAY_RANK = 64
A_RANK = 64
VRES_RANK = 32
HGRN_EXPAND = 128
HGRN_HEADS = D_HGRN // HGRN_EXPAND
RWKV_COLS = 4 * D_RWKV + DECAY_RANK + A_RANK
HGRN_COLS = 4 * D_HGRN
IN_COLS = RWKV_COLS + HGRN_COLS
ALPHA = (2 * DEPTH) ** 0.25
LN_EPS = 1e-5
GN_EPS = 64e-5
RMS_EPS = 1e-5
LB_FLOOR = 1e-30
LOG2E = math.log2(math.e)

LANES = 128
CHUNK = 64
PAIR_ROWS = 2 * CHUNK
RWKV_PAIRS = D_RWKV // LANES
VMEM_LIMIT = 56 * 1024 * 1024
MXU_WIDTH = 256
IN_TM = 512
IN_TN = 6 * MXU_WIDTH
IN_MAIN = (IN_COLS // IN_TN) * IN_TN
OUT_TM, OUT_SLABS = 512, 2

F32 = jnp.float32


def _bf(x):
    return x.astype(jnp.bfloat16)


def _bdot(a, b):
    return jnp.dot(_bf(a), _bf(b), preferred_element_type=F32)


def _bdot_nt(a, b):
    return lax.dot_general(_bf(a), _bf(b), (((1,), (1,)), ((), ())), preferred_element_type=F32)


def _bdot_tn(a, b):
    return lax.dot_general(_bf(a), _bf(b), (((0,), (0,)), ((), ())), preferred_element_type=F32)


def _split_dot(m01, x, terms):
    pieces = []
    rem = x
    for t in range(terms):
        pieces.append(_bf(rem))
        if t + 1 < terms:
            rem = rem - pieces[-1].astype(F32)
    return jnp.dot(_bf(m01), jnp.concatenate(pieces, axis=0), preferred_element_type=F32)


def _sigmoid(x):
    return 0.5 * jnp.tanh(0.5 * x) + 0.5


def _head_sums(xs, scale=1.0):
    rows, cols = xs[0].shape
    width = len(xs) * LANES
    ones = ((_iota2((width, width), 0) // RWKV_HEAD) == (_iota2((width, width), 1) // RWKV_HEAD))
    ones = jnp.where(ones, scale, 0.0).astype(jnp.bfloat16)
    tiles = range(cols // LANES)
    lhs = jnp.concatenate(
        [jnp.concatenate([x[:, t * LANES:(t + 1) * LANES] for t in tiles], axis=0) for x in xs],
        axis=1)
    sums = jnp.dot(_bf(lhs), ones, preferred_element_type=F32)
    return [jnp.concatenate([sums[t * rows:(t + 1) * rows, i * LANES:(i + 1) * LANES]
                             for t in tiles], axis=1) for i in range(len(xs))]


def _silu(x):
    return x * _sigmoid(x)


def _iota2(shape, dim):
    return lax.broadcasted_iota(jnp.int32, shape, dim)


def _mm_kernel(x_ref, w_ref, o_ref, wb_ref):
    @pl.when(pl.program_id(1) == 0)
    def _():
        wb_ref[...] = w_ref[0].astype(jnp.bfloat16)

    o_ref[...] = jnp.dot(x_ref[...].astype(jnp.bfloat16), wb_ref[...],
                         preferred_element_type=F32)


def _in_proj(x, w_in, layer, col0, n, tn):
    m, k = x.shape
    tm = IN_TM
    return pl.pallas_call(
        _mm_kernel,
        out_shape=jax.ShapeDtypeStruct((m, n), F32),
        grid=(n // tn, m // tm),
        in_specs=[pl.BlockSpec((tm, k), lambda j, i: (i, 0)),
                  pl.BlockSpec((pl.Element(1), pl.Element(k), pl.Element(tn)),
                               lambda j, i: (layer, 0, pl.multiple_of(col0 + j * tn, LANES)))],
        out_specs=pl.BlockSpec((tm, tn), lambda j, i: (i, j)),
        scratch_shapes=[pltpu.VMEM((k, tn), jnp.bfloat16)],
        compiler_params=pltpu.CompilerParams(
            dimension_semantics=("arbitrary", "arbitrary"), vmem_limit_bytes=VMEM_LIMIT),
        name="in_proj",
    )(x, w_in)


def _out_kernel(mix_ref, h_ref, w_ref, lnw_ref, lnb_ref, o_ref, wb_ref):
    @pl.when(pl.program_id(0) == 0)
    def _():
        wb_ref[...] = w_ref[...].astype(jnp.bfloat16)

    slab = OUT_TM // OUT_SLABS
    for s in range(OUT_SLABS):
        rows = slice(s * slab, (s + 1) * slab)
        y = jnp.dot(mix_ref[rows, :].astype(jnp.bfloat16), wb_ref[...],
                    preferred_element_type=F32)
        u = ALPHA * h_ref[rows, :] + y
        mu = jnp.mean(u, axis=-1, keepdims=True)
        d = u - mu
        var = jnp.mean(d * d, axis=-1, keepdims=True)
        o_ref[rows, :] = d * lax.rsqrt(var + LN_EPS) * lnw_ref[...] + lnb_ref[...]


def _out_proj_ln(o_mix, h, w_out, layer, lnw, lnb):
    m = h.shape[0]
    tm = OUT_TM
    return pl.pallas_call(
        _out_kernel,
        out_shape=jax.ShapeDtypeStruct((m, D_MODEL), F32),
        grid=(m // tm,),
        in_specs=[pl.BlockSpec((tm, D_MODEL), lambda i: (i, 0)),
                  pl.BlockSpec((tm, D_MODEL), lambda i: (i, 0)),
                  pl.BlockSpec((None, D_MODEL, D_MODEL), lambda i: (layer, 0, 0),
                               pipeline_mode=pl.Buffered(1)),
                  pl.BlockSpec((1, D_MODEL), lambda i: (0, 0)),
                  pl.BlockSpec((1, D_MODEL), lambda i: (0, 0))],
        out_specs=pl.BlockSpec((tm, D_MODEL), lambda i: (i, 0)),
        scratch_shapes=[pltpu.VMEM((D_MODEL, D_MODEL), jnp.bfloat16)],
        compiler_params=pltpu.CompilerParams(
            dimension_semantics=("arbitrary",), vmem_limit_bytes=VMEM_LIMIT),
        name="out_proj_ln",
    )(o_mix, h, w_out, lnw, lnb)


_HGRN_LEVELS = tuple(CHUNK >> (i + 1) for i in range(int(math.log2(CHUNK))))
_N_LEVELS = len(_HGRN_LEVELS)
_PAIR_SL = [slice(p * LANES, (p + 1) * LANES) for p in range(RWKV_PAIRS)]
_HEAD_SL = [slice(h * LANES, (h + 1) * LANES) for h in range(HGRN_HEADS)]

_BUF_SHAPES = (
    ("ar", (RWKV_PAIRS, PAIR_ROWS, LANES), jnp.bfloat16),
    ("bkT", (RWKV_PAIRS, LANES, 2 * PAIR_ROWS), jnp.bfloat16),
    ("vs", (RWKV_PAIRS, PAIR_ROWS, LANES), jnp.bfloat16),
    ("uv", (RWKV_PAIRS, PAIR_ROWS, LANES), jnp.bfloat16),
    ("bhkh", (RWKV_PAIRS, PAIR_ROWS, LANES), jnp.bfloat16),
    ("gall_r", (1, D_RWKV), F32),
    ("bonus", (CHUNK, D_RWKV), F32),
    ("gate_r", (CHUNK, D_RWKV), F32),
    ("vnat", (CHUNK, D_RWKV), F32),
    ("qe", (_N_LEVELS, CHUNK, D_HGRN), jnp.bfloat16),
    ("keT", (_N_LEVELS, HGRN_HEADS // 2, 2 * LANES, PAIR_ROWS), jnp.bfloat16),
    ("qin", (CHUNK, D_HGRN), jnp.bfloat16),
    ("kdec", (CHUNK, D_HGRN), jnp.bfloat16),
    ("ib", (CHUNK, D_HGRN), jnp.bfloat16),
    ("odiag", (CHUNK, D_HGRN), F32),
    ("gate_h", (CHUNK, D_HGRN), F32),
    ("gall_h", (1, D_HGRN), F32),
)


def _head_mask():
    lane = _iota2((PAIR_ROWS, LANES), 1)
    row = _iota2((PAIR_ROWS, LANES), 0)
    return ((lane < RWKV_HEAD) == (row < CHUNK)).astype(F32)


def _rwkv_prepare(first, has_vmix, y, vf, prm, prev_ref, buf):
    (mu_ref, w0_ref, wup_ref, a0_ref, aup_ref, kk_ref, ka_ref, rk_ref, v0_ref, vdn_ref,
     vup_ref) = prm
    rolled = pltpu.roll(y, shift=1, axis=0)
    top = rolled[0:8, :]
    prev = jnp.zeros_like(y[0:1, :]) if first else prev_ref[...]
    top = jnp.where(_iota2(top.shape, 0) == 0, prev, top)
    y_prev = jnp.concatenate([top, rolled[8:, :]], axis=0)
    prev_ref[...] = y[CHUNK - 1:CHUNK, :]
    xs = y + mu_ref[...] * (y_prev - y)

    r = xs[:, 0:D_RWKV]
    k = xs[:, D_RWKV:2 * D_RWKV]
    v = xs[:, 2 * D_RWKV:3 * D_RWKV]
    z = xs[:, 3 * D_RWKV:4 * D_RWKV]
    wd = xs[:, 4 * D_RWKV:4 * D_RWKV + DECAY_RANK]
    ad = xs[:, 4 * D_RWKV + DECAY_RANK:RWKV_COLS]

    w_raw = w0_ref[...] + _bdot(jnp.tanh(wd), wup_ref[...])
    logw = (-math.exp(-0.5) * LOG2E) * _sigmoid(w_raw)
    a = _sigmoid(a0_ref[...] + _bdot(ad, aup_ref[...]))
    if has_vmix:
        gate = _sigmoid(v0_ref[...] + _bdot(_bdot(v, vdn_ref[...]), vup_ref[...]))
        v = v + (vf - v) * gate
    else:
        buf["vnat"][...] = v
    buf["gate_r"][...] = _silu(z)
    yield

    kk = k * kk_ref[...]
    k = k * (1.0 + (a - 1.0) * ka_ref[...])
    kk_sq, rk_sum = _head_sums([kk * kk, r * k * rk_ref[...]])
    kk = kk * lax.rsqrt(jnp.maximum(kk_sq, 1e-24))
    b = kk * a
    buf["bonus"][...] = rk_sum * v
    yield

    tri = (_iota2((CHUNK, 3 * CHUNK), 1) % CHUNK <= _iota2((CHUNK, 3 * CHUNK), 0)).astype(F32)
    cl = _split_dot(tri, logw, 3)
    cl_last = cl[CHUNK - 1:CHUNK, :]
    g_inv = jnp.exp2(-cl)
    g_end = jnp.exp2(cl_last - cl)
    buf["gall_r"][...] = jnp.exp2(cl_last)
    a_t = _bf(kk * -jnp.exp2(cl - logw))
    r_t = _bf(r * jnp.exp2(cl))
    b_t = _bf(b * g_inv)
    k_t = _bf(k * g_inv)
    b_h = _bf(b * g_end)
    k_h = _bf(k * g_end)
    v_b = _bf(v)
    yield

    mask = _head_mask().astype(jnp.bfloat16)

    def stack(x, p):
        xp = x[:, _PAIR_SL[p]]
        return jnp.concatenate([xp, xp], axis=0) * mask

    for p in range(RWKV_PAIRS):
        sl = _PAIR_SL[p]
        buf["ar"][p] = jnp.concatenate([a_t[:, sl], r_t[:, sl]], axis=0)
        buf["bkT"][p] = jnp.concatenate([stack(b_t, p), stack(k_t, p)], axis=0).T
        buf["bhkh"][p] = jnp.concatenate([b_h[:, sl], k_h[:, sl]], axis=0)
        buf["vs"][p] = stack(v_b, p)
        buf["uv"][p, CHUNK:, :] = v_b[:, sl]
        if p % 2 == 1:
            yield


def _rwkv_chain(half, emit_v, buf, gnw_ref, gnb_ref, o_ref, vf_out_ref, s_ref):
    mask_b = _head_mask().astype(jnp.bfloat16)
    t_idx = _iota2((CHUNK, LANES), 0)
    s_idx = _iota2((CHUNK, LANES), 1) % CHUNK
    strict = (s_idx < t_idx).astype(F32)
    strict_b = strict.astype(jnp.bfloat16)
    incl_b = (s_idx <= t_idx).astype(jnp.bfloat16)
    eye = (s_idx == t_idx).astype(F32)
    prow = _iota2((PAIR_ROWS, LANES), 0)
    pcol = _iota2((PAIR_ROWS, LANES), 1)
    same_head = (prow // RWKV_HEAD) == (pcol // RWKV_HEAD)
    rows = slice(half * CHUNK, (half + 1) * CHUNK)

    def diag(x):
        return jnp.concatenate([x, x], axis=0) * mask_b

    pairs = range(RWKV_PAIRS)
    ar = [buf["ar"][p] for p in pairs]
    sc = [_bdot(ar[p], buf["bkT"][p]) for p in pairs]
    pw = [strict * sc[p][:CHUNK, :LANES] for p in pairs]
    tinv = [eye + pw[p] for p in pairs]
    a_kv = [jnp.concatenate([_bf(sc[p][:CHUNK, LANES:]) * strict_b,
                             _bf(sc[p][CHUNK:, LANES:]) * incl_b], axis=0) for p in pairs]
    a_rb = [_bf(sc[p][CHUNK:, :LANES]) * incl_b for p in pairs]
    pwb = [_bf(x) for x in pw]
    pwd = [diag(x) for x in pwb]
    s_old = [s_ref[p] for p in pairs]
    ars = [_bdot_nt(ar[p], s_old[p]) for p in pairs]
    akv = [_bdot(a_kv[p], buf["vs"][p]) for p in pairs]
    yield
    for it in range(int(math.log2(CHUNK)) - 1):
        pw = [_bdot(pwb[p], pwd[p]) for p in pairs]
        if it > 0:
            tinv = [tinv[p] + _bdot(tinv[p], pwd[p]) for p in pairs]
        pwb = [_bf(x) for x in pw]
        pwd = [diag(x) for x in pwb]
        yield
    tinv = [tinv[p] + _bdot(tinv[p], pwd[p]) for p in pairs]
    yield

    u = [_bf(_bdot(tinv[p], diag(_bf(ars[p][:CHUNK] + akv[p][:CHUNK])))) for p in pairs]
    yield
    for p in pairs:
        buf["uv"][p, :CHUNK, :] = u[p]
        upd = _bdot_tn(buf["uv"][p], buf["bhkh"][p])
        s_ref[p] = jnp.where(same_head, s_old[p] * buf["gall_r"][:, _PAIR_SL[p]] + upd, 0.0)
    o = jnp.concatenate([ars[p][CHUNK:] + _bdot(a_rb[p], diag(u[p])) + akv[p][CHUNK:]
                         for p in pairs], axis=1)
    yield

    d = o - _head_sums([o], 1.0 / RWKV_HEAD)[0]
    yield
    var = _head_sums([d * d], 1.0 / RWKV_HEAD)[0]
    on = d * lax.rsqrt(var + GN_EPS) * gnw_ref[...] + gnb_ref[...]
    o_ref[0, rows, 0:D_RWKV] = (on + buf["bonus"][...]) * buf["gate_r"][...]
    if emit_v:
        vf_out_ref[0, rows, :] = buf["vnat"][...]
    yield


def _hgrn_prepare(layer, hg, lbl_ref, gw_ref, buf):
    lg = lbl_ref[...]
    e = jnp.exp(lg - jnp.max(lg, axis=0, keepdims=True))
    sm = e / jnp.sum(e, axis=0, keepdims=True)
    lb = jnp.sum(sm[0:layer + 1, :], axis=0, keepdims=True) - sm[0:1, :]

    q = _silu(hg[:, 0:D_HGRN])
    f_raw = hg[:, D_HGRN:2 * D_HGRN]
    i_in = hg[:, 2 * D_HGRN:3 * D_HGRN]
    z = hg[:, 3 * D_HGRN:4 * D_HGRN]
    buf["gate_h"][...] = gw_ref[...] * _silu(z)
    buf["ib"][...] = _bf(i_in)

    sig = _sigmoid(f_raw)
    log_f = jnp.log2(jnp.maximum(lb, LB_FLOOR) + (1.0 - lb) * sig)
    k = (1.0 - lb) * (1.0 - sig)
    yield

    trow = _iota2((CHUNK, 2 * CHUNK), 0)
    tcol = _iota2((CHUNK, 2 * CHUNK), 1) % CHUNK
    mats = [tcol <= trow]
    for m in _HGRN_LEVELS:
        mid = (trow // (2 * m)) * (2 * m) + m
        after = trow >= mid
        mats.append((after & (tcol >= mid) & (tcol <= trow))
                    | (~after & (tcol > trow) & (tcol < mid)))
    cums = _split_dot(jnp.concatenate(mats, axis=0).astype(F32), log_f, 2)
    b = cums[0:CHUNK, :]
    b_last = b[CHUNK - 1:CHUNK, :]
    buf["qin"][...] = _bf(q * jnp.exp2(b))
    buf["kdec"][...] = _bf(k * jnp.exp2(b_last - b))
    buf["gall_h"][...] = jnp.exp2(b_last)
    q_b = _bf(q)
    k_b = _bf(k)
    yield
    for li in range(_N_LEVELS):
        e = _bf(jnp.exp2(cums[(li + 1) * CHUNK:(li + 2) * CHUNK, :]))
        buf["qe"][li] = q_b * e
        ke = k_b * e
        for hp in range(HGRN_HEADS // 2):
            buf["keT"][li, hp] = _diag2(ke[:, _HEAD_SL[2 * hp]], ke[:, _HEAD_SL[2 * hp + 1]]).T
        yield
    qk = q * k
    for h in range(HGRN_HEADS):
        sl = _HEAD_SL[h]
        buf["odiag"][:, sl] = jnp.sum(qk[:, sl], axis=-1, keepdims=True) * i_in[:, sl]
    yield


def _diag2(x0, x1):
    return jnp.concatenate([jnp.concatenate([x0, jnp.zeros_like(x1)], axis=1),
                            jnp.concatenate([jnp.zeros_like(x0), x1], axis=1)], axis=0)


def _hgrn_chain(half, buf, o_ref, s_ref):
    trow = _iota2((CHUNK, LANES), 0)
    tcol = _iota2((CHUNK, LANES), 1) % CHUNK
    rows = slice(half * CHUNK, (half + 1) * CHUNK)
    heads = range(HGRN_HEADS)
    pairs = range(HGRN_HEADS // 2)
    pair_sl = [slice(2 * hp * LANES, 2 * (hp + 1) * LANES) for hp in pairs]

    att = [None] * len(pairs)
    yield
    for li, m in enumerate(_HGRN_LEVELS):
        mid = (trow // (2 * m)) * (2 * m) + m
        keep = ((trow >= mid) & (tcol < mid) & (tcol // (2 * m) == trow // (2 * m))).astype(F32)
        qe = buf["qe"][li]
        for hp in pairs:
            part = keep * _bdot(qe[:, pair_sl[hp]], buf["keT"][li, hp])
            att[hp] = part if att[hp] is None else att[hp] + part
        yield
    s_old = [s_ref[h] for h in heads]
    i_b = buf["ib"][...]
    q_in = buf["qin"][...]
    k_dec = buf["kdec"][...]
    o_2 = []
    for hp in pairs:
        h0, h1 = 2 * hp, 2 * hp + 1
        i2 = _diag2(i_b[:, _HEAD_SL[h0]], i_b[:, _HEAD_SL[h1]])
        inter = [_bdot_nt(q_in[:, _HEAD_SL[h]], s_old[h]) for h in (h0, h1)]
        o_2.append(_bdot(att[hp], i2) + jnp.concatenate(inter, axis=1))
    for h in heads:
        sl = _HEAD_SL[h]
        s_ref[h] = s_old[h] * buf["gall_h"][:, sl] + _bdot_tn(i_b[:, sl], k_dec[:, sl])
    yield
    for h in heads:
        sl = _HEAD_SL[h]
        o = o_2[h // 2][:, (h % 2) * LANES:(h % 2 + 1) * LANES] + buf["odiag"][:, sl]
        o = o * lax.rsqrt(jnp.mean(o * o, axis=-1, keepdims=True) + RMS_EPS)
        o_ref[0, rows, D_RWKV + h * LANES:D_RWKV + (h + 1) * LANES] = o * buf["gate_h"][:, sl]
    yield


_STAGE_ORDER = (0, 1, 0, 1, 0, 1, 0, 1, 0, 1, 0, 1, 0, 1, 0, 1, 0, 1, 0, 0)
_DONE = object()


def _mixer_kernel(layer, *refs):
    has_vmix = layer > 0
    n_in = 3 * (3 if has_vmix else 2)
    n_prm = 13 if has_vmix else 10
    proj_refs = list(zip(refs[0:3], refs[3:6]))
    vf_refs = refs[6:9] if has_vmix else (None, None, None)
    prm = list(refs[n_in:n_in + n_prm])
    (mu_ref, w0_ref, wup_ref, a0_ref, aup_ref, kk_ref, ka_ref, rk_ref, gnw_ref, gnb_ref) = prm[:10]
    vm = prm[10:13] if has_vmix else [None, None, None]
    prep_prm = (mu_ref, w0_ref, wup_ref, a0_ref, aup_ref, kk_ref, ka_ref, rk_ref, *vm)
    pos = n_in + n_prm
    lbl_ref, gw_ref, o_ref = refs[pos:pos + 3]
    pos += 3
    vf_out_ref = None
    if not has_vmix:
        vf_out_ref = refs[pos]
        pos += 1
    s_r_ref, prev_ref, s_h_ref = refs[pos:pos + 3]
    pos += 3
    nb = len(_BUF_SHAPES)
    bufs = [dict(zip([n for n, _, _ in _BUF_SHAPES], refs[pos + i * nb:pos + (i + 1) * nb]))
            for i in range(2)]

    def prepare(first, proj_ref, vf_ref, buf):
        main_ref, tail_ref = proj_ref
        vf = vf_ref[0] if has_vmix else None
        hg = jnp.concatenate([main_ref[0, :, RWKV_COLS:IN_MAIN], tail_ref[0]], axis=1)
        return [_rwkv_prepare(first, has_vmix, main_ref[0, :, 0:RWKV_COLS], vf, prep_prm,
                              prev_ref, buf),
                _hgrn_prepare(layer, hg, lbl_ref, gw_ref, buf)]

    def chain(half, buf):
        gens = (_rwkv_chain(half, not has_vmix, buf, gnw_ref, gnb_ref, o_ref, vf_out_ref, s_r_ref),
                _hgrn_chain(half, buf, o_ref, s_h_ref))
        for g in _STAGE_ORDER:
            next(gens[g])
            yield

    def run(chain_gen, prep_gens):
        live = [chain_gen] + list(prep_gens)
        while live:
            for g in list(live):
                if next(g, _DONE) is _DONE:
                    live.remove(g)

    @pl.when(pl.program_id(1) == 0)
    def _():
        s_r_ref[...] = jnp.zeros_like(s_r_ref)
        s_h_ref[...] = jnp.zeros_like(s_h_ref)
        run(iter(()), prepare(True, proj_refs[0], vf_refs[0], bufs[0]))

    run(chain(0, bufs[0]), prepare(False, proj_refs[1], vf_refs[1], bufs[1]))
    run(chain(1, bufs[1]), prepare(False, proj_refs[2], vf_refs[2], bufs[0]))


def _mixer_layer(proj, proj_tail, layer, rwkv_params, vmix, v_first, lb_logits, g_norm_w):
    bsz, seq, _ = proj.shape
    has_vmix = layer > 0
    nc = seq // CHUNK
    row_spec = lambda n: pl.BlockSpec((1, n), lambda b, j: (0, 0))
    full_spec = lambda s: pl.BlockSpec(s, lambda b, j: (0, 0))
    chunk_specs = lambda n: [
        pl.BlockSpec((1, CHUNK, n), lambda b, j: (b, 0, 0)),
        pl.BlockSpec((1, CHUNK, n), lambda b, j: (b, 2 * j + 1, 0)),
        pl.BlockSpec((1, CHUNK, n), lambda b, j: (b, jnp.minimum(2 * j + 2, nc - 1), 0))]
    step_spec = lambda n: pl.BlockSpec((1, 2 * CHUNK, n), lambda b, j: (b, j, 0))
    args = [proj] * 3 + [proj_tail] * 3
    in_specs = chunk_specs(IN_MAIN) + chunk_specs(IN_COLS - IN_MAIN)
    if has_vmix:
        args += [v_first] * 3
        in_specs += chunk_specs(D_RWKV)
    args += list(rwkv_params)
    in_specs += [row_spec(RWKV_COLS), row_spec(D_RWKV), full_spec((DECAY_RANK, D_RWKV)),
                 row_spec(D_RWKV), full_spec((A_RANK, D_RWKV)), row_spec(D_RWKV),
                 row_spec(D_RWKV), row_spec(D_RWKV), row_spec(D_RWKV), row_spec(D_RWKV)]
    out_sds = jax.ShapeDtypeStruct((bsz, seq, D_MODEL), F32)
    if has_vmix:
        args += list(vmix)
        in_specs += [row_spec(D_RWKV), full_spec((D_RWKV, VRES_RANK)),
                     full_spec((VRES_RANK, D_RWKV))]
        out_shape, out_specs = out_sds, step_spec(D_MODEL)
    else:
        out_shape = (out_sds, jax.ShapeDtypeStruct((bsz, seq, D_RWKV), F32))
        out_specs = (step_spec(D_MODEL), step_spec(D_RWKV))
    args += [lb_logits, g_norm_w]
    in_specs += [full_spec((DEPTH, D_HGRN)), row_spec(D_HGRN)]
    scratch = [pltpu.VMEM((RWKV_PAIRS, PAIR_ROWS, LANES), F32),
               pltpu.VMEM((1, RWKV_COLS), F32),
               pltpu.VMEM((HGRN_HEADS, HGRN_EXPAND, LANES), F32)]
    scratch += [pltpu.VMEM(shape, dtype) for _ in range(2) for _, shape, dtype in _BUF_SHAPES]
    res = pl.pallas_call(
        functools.partial(_mixer_kernel, layer),
        out_shape=out_shape,
        grid=(bsz, nc // 2),
        in_specs=in_specs,
        out_specs=out_specs,
        scratch_shapes=scratch,
        compiler_params=pltpu.CompilerParams(
            dimension_semantics=("arbitrary", "arbitrary"), vmem_limit_bytes=VMEM_LIMIT),
        name="mixer",
    )(*args)
    if has_vmix:
        return res, v_first
    return res


def kernel(x, w_in, shift_mu, w_decay0, w_decay_up, a0, a_up, k_k, k_a, r_k, ln_x_w, ln_x_b,
           v_mix0, v_mix_down, v_mix_up, lb_logits, g_norm_w, w_out, ln_w, ln_b):
    out_dtype = x.dtype
    bsz, seq, _ = x.shape
    h = x.astype(F32).reshape(bsz * seq, D_MODEL)
    lb_logits = lb_logits.astype(F32)
    row = lambda t: t.reshape(1, -1)
    v_first = None
    for l in range(DEPTH):
        proj = _in_proj(h, w_in, l, 0, IN_MAIN, IN_TN).reshape(bsz, seq, IN_MAIN)
        tail = _in_proj(h, w_in, l, IN_MAIN, IN_COLS - IN_MAIN, IN_COLS - IN_MAIN)
        tail = tail.reshape(bsz, seq, IN_COLS - IN_MAIN)
        params = (row(shift_mu[l]), row(w_decay0[l]), w_decay_up[l], row(a0[l]), a_up[l],
                  row(k_k[l]), row(k_a[l]), row(r_k[l]), row(ln_x_w[l]), row(ln_x_b[l]))
        vmix = None if l == 0 else (row(v_mix0[l - 1]), v_mix_down[l - 1], v_mix_up[l - 1])
        o_mix, v_first = _mixer_layer(proj, tail, l, params, vmix, v_first, lb_logits,
                                      row(g_norm_w[l]))
        h = _out_proj_ln(o_mix.reshape(bsz * seq, D_MODEL), h, w_out, l, row(ln_w[l]),
                         row(ln_b[l]))
    return h.reshape(bsz, seq, D_MODEL).astype(out_dtype)
```

```python
import functools
import math

import jax
import jax.numpy as jnp
from jax import lax
from jax.experimental import pallas as pl
from jax.experimental.pallas import tpu as pltpu

D_MODEL = 2048
DEPTH = 2
D_RWKV = D_MODEL // 2
D_HGRN = D_MODEL - D_RWKV
RWKV_HEAD = 64
DECAY_RANK = 64
A_RANK = 64
VRES_RANK = 32
HGRN_EXPAND = 128
HGRN_HEADS = D_HGRN // HGRN_EXPAND
RWKV_COLS = 4 * D_RWKV + DECAY_RANK + A_RANK
HGRN_COLS = 4 * D_HGRN
IN_COLS = RWKV_COLS + HGRN_COLS
ALPHA = (2 * DEPTH) ** 0.25
LN_EPS = 1e-5
GN_EPS = 64e-5
RMS_EPS = 1e-5
LB_FLOOR = 1e-30
LOG2E = math.log2(math.e)

LANES = 128
CHUNK = 64
PAIR_ROWS = 2 * CHUNK
RWKV_PAIRS = D_RWKV // LANES
VMEM_LIMIT = 56 * 1024 * 1024
MXU_WIDTH = 256
IN_TM = 512
IN_TN = 6 * MXU_WIDTH
IN_MAIN = (IN_COLS // IN_TN) * IN_TN
OUT_TM, OUT_SLABS = 512, 2

F32 = jnp.float32


def _bf(x):
    return x.astype(jnp.bfloat16)


def _bdot(a, b):
    return jnp.dot(_bf(a), _bf(b), preferred_element_type=F32)


def _bdot_nt(a, b):
    return lax.dot_general(_bf(a), _bf(b), (((1,), (1,)), ((), ())), preferred_element_type=F32)


def _bdot_tn(a, b):
    return lax.dot_general(_bf(a), _bf(b), (((0,), (0,)), ((), ())), preferred_element_type=F32)


def _split_dot(m01, x, terms):
    pieces = []
    rem = x
    for t in range(terms):
        pieces.append(_bf(rem))
        if t + 1 < terms:
            rem = rem - pieces[-1].astype(F32)
    return jnp.dot(_bf(m01), jnp.concatenate(pieces, axis=0), preferred_element_type=F32)


def _sigmoid(x):
    return 0.5 * jnp.tanh(0.5 * x) + 0.5


def _head_sums(xs, scale=1.0):
    rows, cols = xs[0].shape
    width = len(xs) * LANES
    ones = ((_iota2((width, width), 0) // RWKV_HEAD) == (_iota2((width, width), 1) // RWKV_HEAD))
    ones = jnp.where(ones, scale, 0.0).astype(jnp.bfloat16)
    tiles = range(cols // LANES)
    lhs = jnp.concatenate(
        [jnp.concatenate([x[:, t * LANES:(t + 1) * LANES] for t in tiles], axis=0) for x in xs],
        axis=1)
    sums = jnp.dot(_bf(lhs), ones, preferred_element_type=F32)
    return [jnp.concatenate([sums[t * rows:(t + 1) * rows, i * LANES:(i + 1) * LANES]
                             for t in tiles], axis=1) for i in range(len(xs))]


def _head_means_split(x):
    rows, cols = x.shape
    hi = _bf(x)
    lo = _bf(x - hi.astype(F32))
    same = ((_iota2((2 * LANES, LANES), 0) % LANES) // RWKV_HEAD
            == _iota2((2 * LANES, LANES), 1) // RWKV_HEAD)
    ones = jnp.where(same, 1.0 / RWKV_HEAD, 0.0).astype(jnp.bfloat16)
    tiles = range(cols // LANES)
    lhs = jnp.concatenate(
        [jnp.concatenate([p[:, t * LANES:(t + 1) * LANES] for t in tiles], axis=0)
         for p in (hi, lo)], axis=1)
    sums = jnp.dot(lhs, ones, preferred_element_type=F32)
    return jnp.concatenate([sums[t * rows:(t + 1) * rows, :] for t in tiles], axis=1)


def _silu(x):
    return x * _sigmoid(x)


def _iota2(shape, dim):
    return lax.broadcasted_iota(jnp.int32, shape, dim)


def _mm_kernel(x_ref, w_ref, o_ref, wb_ref):
    @pl.when(pl.program_id(1) == 0)
    def _():
        wb_ref[...] = w_ref[0].astype(jnp.bfloat16)

    o_ref[...] = jnp.dot(x_ref[...].astype(jnp.bfloat16), wb_ref[...],
                         preferred_element_type=F32)


def _in_proj(x, w_in, layer, col0, n, tn):
    m, k = x.shape
    tm = IN_TM
    return pl.pallas_call(
        _mm_kernel,
        out_shape=jax.ShapeDtypeStruct((m, n), F32),
        grid=(n // tn, m // tm),
        in_specs=[pl.BlockSpec((tm, k), lambda j, i: (i, 0)),
                  pl.BlockSpec((pl.Element(1), pl.Element(k), pl.Element(tn)),
                               lambda j, i: (layer, 0, pl.multiple_of(col0 + j * tn, LANES)))],
        out_specs=pl.BlockSpec((tm, tn), lambda j, i: (i, j)),
        scratch_shapes=[pltpu.VMEM((k, tn), jnp.bfloat16)],
        compiler_params=pltpu.CompilerParams(
            dimension_semantics=("arbitrary", "arbitrary"), vmem_limit_bytes=VMEM_LIMIT),
        name="in_proj",
    )(x, w_in)


def _out_kernel(mix_ref, h_ref, w_ref, lnw_ref, lnb_ref, o_ref, wb_ref):
    @pl.when(pl.program_id(0) == 0)
    def _():
        wb_ref[...] = w_ref[...].astype(jnp.bfloat16)

    slab = OUT_TM // OUT_SLABS
    for s in range(OUT_SLABS):
        rows = slice(s * slab, (s + 1) * slab)
        y = jnp.dot(mix_ref[rows, :].astype(jnp.bfloat16), wb_ref[...],
                    preferred_element_type=F32)
        u = ALPHA * h_ref[rows, :] + y
        mu = jnp.mean(u, axis=-1, keepdims=True)
        d = u - mu
        var = jnp.mean(d * d, axis=-1, keepdims=True)
        o_ref[rows, :] = d * lax.rsqrt(var + LN_EPS) * lnw_ref[...] + lnb_ref[...]


def _out_proj_ln(o_mix, h, w_out, layer, lnw, lnb):
    m = h.shape[0]
    tm = OUT_TM
    return pl.pallas_call(
        _out_kernel,
        out_shape=jax.ShapeDtypeStruct((m, D_MODEL), F32),
        grid=(m // tm,),
        in_specs=[pl.BlockSpec((tm, D_MODEL), lambda i: (i, 0)),
                  pl.BlockSpec((tm, D_MODEL), lambda i: (i, 0)),
                  pl.BlockSpec((None, D_MODEL, D_MODEL), lambda i: (layer, 0, 0),
                               pipeline_mode=pl.Buffered(1)),
                  pl.BlockSpec((1, D_MODEL), lambda i: (0, 0)),
                  pl.BlockSpec((1, D_MODEL), lambda i: (0, 0))],
        out_specs=pl.BlockSpec((tm, D_MODEL), lambda i: (i, 0)),
        scratch_shapes=[pltpu.VMEM((D_MODEL, D_MODEL), jnp.bfloat16)],
        compiler_params=pltpu.CompilerParams(
            dimension_semantics=("arbitrary",), vmem_limit_bytes=VMEM_LIMIT),
        name="out_proj_ln",
    )(o_mix, h, w_out, lnw, lnb)


_HGRN_LEVELS = tuple(CHUNK >> (i + 1) for i in range(int(math.log2(CHUNK))))
_N_LEVELS = len(_HGRN_LEVELS)
_PAIR_SL = [slice(p * LANES, (p + 1) * LANES) for p in range(RWKV_PAIRS)]
_HEAD_SL = [slice(h * LANES, (h + 1) * LANES) for h in range(HGRN_HEADS)]

_BUF_SHAPES = (
    ("ar", (RWKV_PAIRS, PAIR_ROWS, LANES), jnp.bfloat16),
    ("bkT", (RWKV_PAIRS, LANES, 2 * PAIR_ROWS), jnp.bfloat16),
    ("vs", (RWKV_PAIRS, PAIR_ROWS, LANES), jnp.bfloat16),
    ("uv", (RWKV_PAIRS, PAIR_ROWS, LANES), jnp.bfloat16),
    ("bhkh", (RWKV_PAIRS, PAIR_ROWS, LANES), jnp.bfloat16),
    ("gall_r", (1, D_RWKV), F32),
    ("bonus", (CHUNK, D_RWKV), F32),
    ("gate_r", (CHUNK, D_RWKV), F32),
    ("vnat", (CHUNK, D_RWKV), F32),
    ("qe", (_N_LEVELS, CHUNK, D_HGRN), jnp.bfloat16),
    ("keT", (_N_LEVELS, HGRN_HEADS // 2, 2 * LANES, PAIR_ROWS), jnp.bfloat16),
    ("qin", (CHUNK, D_HGRN), jnp.bfloat16),
    ("kdec", (CHUNK, D_HGRN), jnp.bfloat16),
    ("ib", (CHUNK, D_HGRN), jnp.bfloat16),
    ("odiag", (CHUNK, D_HGRN), F32),
    ("gate_h", (CHUNK, D_HGRN), F32),
    ("gall_h", (1, D_HGRN), F32),
)


def _head_mask():
    lane = _iota2((PAIR_ROWS, LANES), 1)
    row = _iota2((PAIR_ROWS, LANES), 0)
    return ((lane < RWKV_HEAD) == (row < CHUNK)).astype(F32)


def _rwkv_prepare(first, has_vmix, y_ref, vf_ref, prm, prev_ref, buf):
    (mu_ref, w0_ref, wup_ref, a0_ref, aup_ref, kk_ref, ka_ref, rk_ref, v0_ref, vdn_ref,
     vup_ref) = prm

    def shifted(c0, c1):
        y = y_ref[0, :, c0:c1]
        rolled = pltpu.roll(y, shift=1, axis=0)
        top = rolled[0:8, :]
        prev = jnp.zeros_like(y[0:1, :]) if first else prev_ref[:, c0:c1]
        top = jnp.where(_iota2(top.shape, 0) == 0, prev, top)
        y_prev = jnp.concatenate([top, rolled[8:, :]], axis=0)
        prev_ref[:, c0:c1] = y[CHUNK - 1:CHUNK, :]
        return y + mu_ref[:, c0:c1] * (y_prev - y)

    tail = shifted(4 * D_RWKV, RWKV_COLS)
    wd = tail[:, 0:DECAY_RANK]
    ad = tail[:, DECAY_RANK:]
    w_raw = w0_ref[...] + _bdot(jnp.tanh(wd), wup_ref[...])
    logw = (-math.exp(-0.5) * LOG2E) * _sigmoid(w_raw)
    yield
    a = _sigmoid(a0_ref[...] + _bdot(ad, aup_ref[...]))
    yield
    tri = (_iota2((CHUNK, 3 * CHUNK), 1) % CHUNK <= _iota2((CHUNK, 3 * CHUNK), 0)).astype(F32)
    cl = _split_dot(tri, logw, 3)
    cl_last = cl[CHUNK - 1:CHUNK, :]
    buf["gall_r"][...] = jnp.exp2(cl_last)
    yield
    buf["gate_r"][...] = _silu(shifted(3 * D_RWKV, 4 * D_RWKV))
    yield
    v = shifted(2 * D_RWKV, 3 * D_RWKV)
    if has_vmix:
        gate = _sigmoid(v0_ref[...] + _bdot(_bdot(v, vdn_ref[...]), vup_ref[...]))
        v = v + (vf_ref[0] - v) * gate
    else:
        buf["vnat"][...] = v
    v_b = _bf(v)
    yield
    r = shifted(0, D_RWKV)
    r_t = _bf(r * jnp.exp2(cl))
    yield
    k = shifted(D_RWKV, 2 * D_RWKV)
    kk = k * kk_ref[...]
    k = k * (1.0 + (a - 1.0) * ka_ref[...])
    yield
    kk_sq, rk_sum = _head_sums([kk * kk, r * k * rk_ref[...]])
    kk = kk * lax.rsqrt(jnp.maximum(kk_sq, 1e-24))
    buf["bonus"][...] = rk_sum * v
    yield
    b = kk * a
    a_t = _bf(kk * -jnp.exp2(cl - logw))
    yield
    g_inv = jnp.exp2(-cl)
    b_t = _bf(b * g_inv)
    k_t = _bf(k * g_inv)
    yield
    g_end = jnp.exp2(cl_last - cl)
    b_h = _bf(b * g_end)
    k_h = _bf(k * g_end)
    yield

    mask = _head_mask().astype(jnp.bfloat16)

    def stack(x, p):
        xp = x[:, _PAIR_SL[p]]
        return jnp.concatenate([xp, xp], axis=0) * mask

    for p in range(RWKV_PAIRS):
        sl = _PAIR_SL[p]
        buf["ar"][p] = jnp.concatenate([a_t[:, sl], r_t[:, sl]], axis=0)
        buf["bkT"][p] = jnp.concatenate([stack(b_t, p), stack(k_t, p)], axis=0).T
        buf["bhkh"][p] = jnp.concatenate([b_h[:, sl], k_h[:, sl]], axis=0)
        buf["vs"][p] = stack(v_b, p)
        buf["uv"][p, CHUNK:, :] = v_b[:, sl]
        yield


def _rwkv_chain(half, emit_v, buf, gnw_ref, gnb_ref, o_ref, vf_out_ref, s_ref):
    mask_b = _head_mask().astype(jnp.bfloat16)
    t_idx = _iota2((CHUNK, LANES), 0)
    s_idx = _iota2((CHUNK, LANES), 1) % CHUNK
    strict = (s_idx < t_idx).astype(F32)
    strict_b = strict.astype(jnp.bfloat16)
    incl_b = (s_idx <= t_idx).astype(jnp.bfloat16)
    eye = (s_idx == t_idx).astype(F32)
    prow = _iota2((PAIR_ROWS, LANES), 0)
    pcol = _iota2((PAIR_ROWS, LANES), 1)
    same_head = (prow // RWKV_HEAD) == (pcol // RWKV_HEAD)
    rows = slice(half * CHUNK, (half + 1) * CHUNK)

    def diag(x):
        return jnp.concatenate([x, x], axis=0) * mask_b

    pairs = range(RWKV_PAIRS)
    ar = [buf["ar"][p] for p in pairs]
    sc = [_bdot(ar[p], buf["bkT"][p]) for p in pairs]
    pw = [strict * sc[p][:CHUNK, :LANES] for p in pairs]
    tinv = [eye + pw[p] for p in pairs]
    a_kv = [jnp.concatenate([_bf(sc[p][:CHUNK, LANES:]) * strict_b,
                             _bf(sc[p][CHUNK:, LANES:]) * incl_b], axis=0) for p in pairs]
    a_rb = [_bf(sc[p][CHUNK:, :LANES]) * incl_b for p in pairs]
    pwb = [_bf(x) for x in pw]
    pwd = [diag(x) for x in pwb]
    s_old = [s_ref[p] for p in pairs]
    ars = [_bdot_nt(ar[p], s_old[p]) for p in pairs]
    akv = [_bdot(a_kv[p], buf["vs"][p]) for p in pairs]
    yield
    for it in range(int(math.log2(CHUNK)) - 1):
        pw = [_bdot(pwb[p], pwd[p]) for p in pairs]
        if it > 0:
            tinv = [tinv[p] + _bdot(tinv[p], pwd[p]) for p in pairs]
        pwb = [_bf(x) for x in pw]
        pwd = [diag(x) for x in pwb]
        yield
    tinv = [tinv[p] + _bdot(tinv[p], pwd[p]) for p in pairs]
    yield

    u = [_bf(_bdot(tinv[p], diag(_bf(ars[p][:CHUNK] + akv[p][:CHUNK])))) for p in pairs]
    yield
    for p in pairs:
        buf["uv"][p, :CHUNK, :] = u[p]
        upd = _bdot_tn(buf["uv"][p], buf["bhkh"][p])
        s_ref[p] = jnp.where(same_head, s_old[p] * buf["gall_r"][:, _PAIR_SL[p]] + upd, 0.0)
    o = jnp.concatenate([ars[p][CHUNK:] + _bdot(a_rb[p], diag(u[p])) + akv[p][CHUNK:]
                         for p in pairs], axis=1)
    yield

    d = o - _head_means_split(o)
    yield
    var = _head_sums([d * d], 1.0 / RWKV_HEAD)[0]
    on = d * lax.rsqrt(var + GN_EPS) * gnw_ref[...] + gnb_ref[...]
    o_ref[0, rows, 0:D_RWKV] = (on + buf["bonus"][...]) * buf["gate_r"][...]
    if emit_v:
        vf_out_ref[0, rows, :] = buf["vnat"][...]
    yield


def _hgrn_prepare(layer, main_ref, tail_ref, lbl_ref, gw_ref, buf):
    lg = lbl_ref[...]
    e = jnp.exp(lg - jnp.max(lg, axis=0, keepdims=True))
    sm = e / jnp.sum(e, axis=0, keepdims=True)
    lb = jnp.sum(sm[0:layer + 1, :], axis=0, keepdims=True) - sm[0:1, :]

    def cols(c0, c1):
        lo, hi = RWKV_COLS + c0, RWKV_COLS + c1
        if hi <= IN_MAIN:
            return main_ref[0, :, lo:hi]
        return jnp.concatenate([main_ref[0, :, lo:IN_MAIN], tail_ref[0, :, 0:hi - IN_MAIN]],
                               axis=1)

    sig = _sigmoid(cols(D_HGRN, 2 * D_HGRN))
    log_f = jnp.log2(jnp.maximum(lb, LB_FLOOR) + (1.0 - lb) * sig)
    k = (1.0 - lb) * (1.0 - sig)
    yield

    trow = _iota2((CHUNK, 2 * CHUNK), 0)
    tcol = _iota2((CHUNK, 2 * CHUNK), 1) % CHUNK
    mats = [tcol <= trow]
    for m in _HGRN_LEVELS:
        mid = (trow // (2 * m)) * (2 * m) + m
        after = trow >= mid
        mats.append((after & (tcol >= mid) & (tcol <= trow))
                    | (~after & (tcol > trow) & (tcol < mid)))
    cums = _split_dot(jnp.concatenate(mats, axis=0).astype(F32), log_f, 2)
    b = cums[0:CHUNK, :]
    b_last = b[CHUNK - 1:CHUNK, :]
    buf["gall_h"][...] = jnp.exp2(b_last)
    yield
    q = _silu(cols(0, D_HGRN))
    buf["qin"][...] = _bf(q * jnp.exp2(b))
    q_b = _bf(q)
    yield
    buf["kdec"][...] = _bf(k * jnp.exp2(b_last - b))
    k_b = _bf(k)
    yield
    i_in = cols(2 * D_HGRN, 3 * D_HGRN)
    buf["ib"][...] = _bf(i_in)
    qk = q * k
    for h in range(HGRN_HEADS):
        sl = _HEAD_SL[h]
        buf["odiag"][:, sl] = jnp.sum(qk[:, sl], axis=-1, keepdims=True) * i_in[:, sl]
    yield
    for li in range(_N_LEVELS):
        e = _bf(jnp.exp2(cums[(li + 1) * CHUNK:(li + 2) * CHUNK, :]))
        buf["qe"][li] = q_b * e
        yield
        ke = k_b * e
        for hp in range(HGRN_HEADS // 2):
            buf["keT"][li, hp] = _diag2(ke[:, _HEAD_SL[2 * hp]], ke[:, _HEAD_SL[2 * hp + 1]]).T
        yield
    buf["gate_h"][...] = gw_ref[...] * _silu(cols(3 * D_HGRN, 4 * D_HGRN))
    yield


def _diag2(x0, x1):
    return jnp.concatenate([jnp.concatenate([x0, jnp.zeros_like(x1)], axis=1),
                            jnp.concatenate([jnp.zeros_like(x0), x1], axis=1)], axis=0)


def _hgrn_chain(half, buf, o_ref, s_ref):
    trow = _iota2((CHUNK, LANES), 0)
    tcol = _iota2((CHUNK, LANES), 1) % CHUNK
    rows = slice(half * CHUNK, (half + 1) * CHUNK)
    heads = range(HGRN_HEADS)
    pairs = range(HGRN_HEADS // 2)
    pair_sl = [slice(2 * hp * LANES, 2 * (hp + 1) * LANES) for hp in pairs]

    att = [None] * len(pairs)
    yield
    for li, m in enumerate(_HGRN_LEVELS):
        mid = (trow // (2 * m)) * (2 * m) + m
        keep = ((trow >= mid) & (tcol < mid) & (tcol // (2 * m) == trow // (2 * m))).astype(F32)
        qe = buf["qe"][li]
        for hp in pairs:
            part = keep * _bdot(qe[:, pair_sl[hp]], buf["keT"][li, hp])
            att[hp] = part if att[hp] is None else att[hp] + part
        yield
    s_old = [s_ref[h] for h in heads]
    i_b = buf["ib"][...]
    q_in = buf["qin"][...]
    k_dec = buf["kdec"][...]
    o_2 = []
    for hp in pairs:
        h0, h1 = 2 * hp, 2 * hp + 1
        i2 = _diag2(i_b[:, _HEAD_SL[h0]], i_b[:, _HEAD_SL[h1]])
        inter = [_bdot_nt(q_in[:, _HEAD_SL[h]], s_old[h]) for h in (h0, h1)]
        o_2.append(_bdot(att[hp], i2) + jnp.concatenate(inter, axis=1))
    for h in heads:
        sl = _HEAD_SL[h]
        s_ref[h] = s_old[h] * buf["gall_h"][:, sl] + _bdot_tn(i_b[:, sl], k_dec[:, sl])
    yield
    for h in heads:
        sl = _HEAD_SL[h]
        o = o_2[h // 2][:, (h % 2) * LANES:(h % 2 + 1) * LANES] + buf["odiag"][:, sl]
        o = o * lax.rsqrt(jnp.mean(o * o, axis=-1, keepdims=True) + RMS_EPS)
        o_ref[0, rows, D_RWKV + h * LANES:D_RWKV + (h + 1) * LANES] = o * buf["gate_h"][:, sl]
    yield


_STAGE_ORDER = (0, 1, 0, 1, 0, 1, 0, 1, 0, 1, 0, 1, 0, 1, 0, 1, 0, 1, 0, 0)
_DONE = object()


def _mixer_kernel(layer, *refs):
    has_vmix = layer > 0
    n_in = 3 * (3 if has_vmix else 2)
    n_prm = 13 if has_vmix else 10
    proj_refs = list(zip(refs[0:3], refs[3:6]))
    vf_refs = refs[6:9] if has_vmix else (None, None, None)
    prm = list(refs[n_in:n_in + n_prm])
    (mu_ref, w0_ref, wup_ref, a0_ref, aup_ref, kk_ref, ka_ref, rk_ref, gnw_ref, gnb_ref) = prm[:10]
    vm = prm[10:13] if has_vmix else [None, None, None]
    prep_prm = (mu_ref, w0_ref, wup_ref, a0_ref, aup_ref, kk_ref, ka_ref, rk_ref, *vm)
    pos = n_in + n_prm
    lbl_ref, gw_ref, o_ref = refs[pos:pos + 3]
    pos += 3
    vf_out_ref = None
    if not has_vmix:
        vf_out_ref = refs[pos]
        pos += 1
    s_r_ref, prev_ref, s_h_ref = refs[pos:pos + 3]
    pos += 3
    nb = len(_BUF_SHAPES)
    bufs = [dict(zip([n for n, _, _ in _BUF_SHAPES], refs[pos + i * nb:pos + (i + 1) * nb]))
            for i in range(2)]

    def prepare(first, proj_ref, vf_ref, buf):
        main_ref, tail_ref = proj_ref
        return [_rwkv_prepare(first, has_vmix, main_ref, vf_ref, prep_prm, prev_ref, buf),
                _hgrn_prepare(layer, main_ref, tail_ref, lbl_ref, gw_ref, buf)]

    def chain(half, buf):
        gens = (_rwkv_chain(half, not has_vmix, buf, gnw_ref, gnb_ref, o_ref, vf_out_ref, s_r_ref),
                _hgrn_chain(half, buf, o_ref, s_h_ref))
        for g in _STAGE_ORDER:
            next(gens[g])
            yield

    def run(chain_gen, prep_gens):
        live = [chain_gen] + list(prep_gens)
        while live:
            for g in list(live):
                if next(g, _DONE) is _DONE:
                    live.remove(g)

    @pl.when(pl.program_id(1) == 0)
    def _():
        s_r_ref[...] = jnp.zeros_like(s_r_ref)
        s_h_ref[...] = jnp.zeros_like(s_h_ref)
        run(iter(()), prepare(True, proj_refs[0], vf_refs[0], bufs[0]))

    run(chain(0, bufs[0]), prepare(False, proj_refs[1], vf_refs[1], bufs[1]))
    run(chain(1, bufs[1]), prepare(False, proj_refs[2], vf_refs[2], bufs[0]))


def _mixer_layer(proj, proj_tail, layer, rwkv_params, vmix, v_first, lb_logits, g_norm_w):
    bsz, seq, _ = proj.shape
    has_vmix = layer > 0
    nc = seq // CHUNK
    row_spec = lambda n: pl.BlockSpec((1, n), lambda b, j: (0, 0))
    full_spec = lambda s: pl.BlockSpec(s, lambda b, j: (0, 0))
    chunk_specs = lambda n: [
        pl.BlockSpec((1, CHUNK, n), lambda b, j: (b, 0, 0)),
        pl.BlockSpec((1, CHUNK, n), lambda b, j: (b, 2 * j + 1, 0)),
        pl.BlockSpec((1, CHUNK, n), lambda b, j: (b, jnp.minimum(2 * j + 2, nc - 1), 0))]
    step_spec = lambda n: pl.BlockSpec((1, 2 * CHUNK, n), lambda b, j: (b, j, 0))
    args = [proj] * 3 + [proj_tail] * 3
    in_specs = chunk_specs(IN_MAIN) + chunk_specs(IN_COLS - IN_MAIN)
    if has_vmix:
        args += [v_first] * 3
        in_specs += chunk_specs(D_RWKV)
    args += list(rwkv_params)
    in_specs += [row_spec(RWKV_COLS), row_spec(D_RWKV), full_spec((DECAY_RANK, D_RWKV)),
                 row_spec(D_RWKV), full_spec((A_RANK, D_RWKV)), row_spec(D_RWKV),
                 row_spec(D_RWKV), row_spec(D_RWKV), row_spec(D_RWKV), row_spec(D_RWKV)]
    out_sds = jax.ShapeDtypeStruct((bsz, seq, D_MODEL), F32)
    if has_vmix:
        args += list(vmix)
        in_specs += [row_spec(D_RWKV), full_spec((D_RWKV, VRES_RANK)),
                     full_spec((VRES_RANK, D_RWKV))]
        out_shape, out_specs = out_sds, step_spec(D_MODEL)
    else:
        out_shape = (out_sds, jax.ShapeDtypeStruct((bsz, seq, D_RWKV), F32))
        out_specs = (step_spec(D_MODEL), step_spec(D_RWKV))
    args += [lb_logits, g_norm_w]
    in_specs += [full_spec((DEPTH, D_HGRN)), row_spec(D_HGRN)]
    scratch = [pltpu.VMEM((RWKV_PAIRS, PAIR_ROWS, LANES), F32),
               pltpu.VMEM((1, RWKV_COLS), F32),
               pltpu.VMEM((HGRN_HEADS, HGRN_EXPAND, LANES), F32)]
    scratch += [pltpu.VMEM(shape, dtype) for _ in range(2) for _, shape, dtype in _BUF_SHAPES]
    res = pl.pallas_call(
        functools.partial(_mixer_kernel, layer),
        out_shape=out_shape,
        grid=(bsz, nc // 2),
        in_specs=in_specs,
        out_specs=out_specs,
        scratch_shapes=scratch,
        compiler_params=pltpu.CompilerParams(
            dimension_semantics=("arbitrary", "arbitrary"), vmem_limit_bytes=VMEM_LIMIT),
        name="mixer",
    )(*args)
    if has_vmix:
        return res, v_first
    return res


def kernel(x, w_in, shift_mu, w_decay0, w_decay_up, a0, a_up, k_k, k_a, r_k, ln_x_w, ln_x_b,
           v_mix0, v_mix_down, v_mix_up, lb_logits, g_norm_w, w_out, ln_w, ln_b):
    out_dtype = x.dtype
    bsz, seq, _ = x.shape
    h = x.astype(F32).reshape(bsz * seq, D_MODEL)
    lb_logits = lb_logits.astype(F32)
    row = lambda t: t.reshape(1, -1)
    v_first = None
    for l in range(DEPTH):
        proj = _in_proj(h, w_in, l, 0, IN_MAIN, IN_TN).reshape(bsz, seq, IN_MAIN)
        tail = _in_proj(h, w_in, l, IN_MAIN, IN_COLS - IN_MAIN, IN_COLS - IN_MAIN)
        tail = tail.reshape(bsz, seq, IN_COLS - IN_MAIN)
        params = (row(shift_mu[l]), row(w_decay0[l]), w_decay_up[l], row(a0[l]), a_up[l],
                  row(k_k[l]), row(k_a[l]), row(r_k[l]), row(ln_x_w[l]), row(ln_x_b[l]))
        vmix = None if l == 0 else (row(v_mix0[l - 1]), v_mix_down[l - 1], v_mix_up[l - 1])
        o_mix, v_first = _mixer_layer(proj, tail, l, params, vmix, v_first, lb_logits,
                                      row(g_norm_w[l]))
        h = _out_proj_ln(o_mix.reshape(bsz * seq, D_MODEL), h, w_out, l, row(ln_w[l]),
                         row(ln_b[l]))
    return h.reshape(bsz, seq, D_MODEL).astype(out_dtype)
```

```python
import functools
import math

import jax
import jax.numpy as jnp
from jax import lax
from jax.experimental import pallas as pl
from jax.experimental.pallas import tpu as pltpu

D_MODEL = 2048
DEPTH = 2
D_RWKV = D_MODEL // 2
D_HGRN = D_MODEL - D_RWKV
RWKV_HEAD = 64
DECAY_RANK = 64
A_RANK = 64
VRES_RANK = 32
HGRN_EXPAND = 128
HGRN_HEADS = D_HGRN // HGRN_EXPAND
RWKV_COLS = 4 * D_RWKV + DECAY_RANK + A_RANK
HGRN_COLS = 4 * D_HGRN
IN_COLS = RWKV_COLS + HGRN_COLS
ALPHA = (2 * DEPTH) ** 0.25
LN_EPS = 1e-5
GN_EPS = 64e-5
RMS_EPS = 1e-5
LB_FLOOR = 1e-30
LOG2E = math.log2(math.e)

LANES = 128
CHUNK = 64
PAIR_ROWS = 2 * CHUNK
RWKV_PAIRS = D_RWKV // LANES
VMEM_LIMIT = 56 * 1024 * 1024
MXU_WIDTH = 256
IN_TM = 512
IN_TN = 6 * MXU_WIDTH
IN_MAIN = (IN_COLS // IN_TN) * IN_TN
OUT_TM, OUT_SLABS = 512, 2

F32 = jnp.float32


def _bf(x):
    return x.astype(jnp.bfloat16)


def _bdot(a, b):
    return jnp.dot(_bf(a), _bf(b), preferred_element_type=F32)


def _bdot_nt(a, b):
    return lax.dot_general(_bf(a), _bf(b), (((1,), (1,)), ((), ())), preferred_element_type=F32)


def _bdot_tn(a, b):
    return lax.dot_general(_bf(a), _bf(b), (((0,), (0,)), ((), ())), preferred_element_type=F32)


def _split_dot(m01, x, terms):
    pieces = []
    rem = x
    for t in range(terms):
        pieces.append(_bf(rem))
        if t + 1 < terms:
            rem = rem - pieces[-1].astype(F32)
    return jnp.dot(_bf(m01), jnp.concatenate(pieces, axis=0), preferred_element_type=F32)


def _sigmoid(x):
    return 0.5 * jnp.tanh(0.5 * x) + 0.5


def _head_sums(xs, scale=1.0):
    rows, cols = xs[0].shape
    width = len(xs) * LANES
    ones = ((_iota2((width, width), 0) // RWKV_HEAD) == (_iota2((width, width), 1) // RWKV_HEAD))
    ones = jnp.where(ones, scale, 0.0).astype(jnp.bfloat16)
    tiles = range(cols // LANES)
    lhs = jnp.concatenate(
        [jnp.concatenate([x[:, t * LANES:(t + 1) * LANES] for t in tiles], axis=0) for x in xs],
        axis=1)
    sums = jnp.dot(_bf(lhs), ones, preferred_element_type=F32)
    return [jnp.concatenate([sums[t * rows:(t + 1) * rows, i * LANES:(i + 1) * LANES]
                             for t in tiles], axis=1) for i in range(len(xs))]


def _head_means_split(x):
    rows, cols = x.shape
    hi = _bf(x)
    lo = _bf(x - hi.astype(F32))
    same = ((_iota2((2 * LANES, LANES), 0) % LANES) // RWKV_HEAD
            == _iota2((2 * LANES, LANES), 1) // RWKV_HEAD)
    ones = jnp.where(same, 1.0 / RWKV_HEAD, 0.0).astype(jnp.bfloat16)
    tiles = range(cols // LANES)
    lhs = jnp.concatenate(
        [jnp.concatenate([p[:, t * LANES:(t + 1) * LANES] for t in tiles], axis=0)
         for p in (hi, lo)], axis=1)
    sums = jnp.dot(lhs, ones, preferred_element_type=F32)
    return jnp.concatenate([sums[t * rows:(t + 1) * rows, :] for t in tiles], axis=1)


def _silu(x):
    return x * _sigmoid(x)


def _iota2(shape, dim):
    return lax.broadcasted_iota(jnp.int32, shape, dim)


def _mm_kernel(x_ref, w_ref, o_ref, wb_ref):
    @pl.when(pl.program_id(1) == 0)
    def _():
        wb_ref[...] = w_ref[0].astype(jnp.bfloat16)

    o_ref[...] = jnp.dot(x_ref[...].astype(jnp.bfloat16), wb_ref[...],
                         preferred_element_type=F32)


def _in_proj(x, w_in, layer, col0, n, tn):
    m, k = x.shape
    tm = IN_TM
    return pl.pallas_call(
        _mm_kernel,
        out_shape=jax.ShapeDtypeStruct((m, n), F32),
        grid=(n // tn, m // tm),
        in_specs=[pl.BlockSpec((tm, k), lambda j, i: (i, 0)),
                  pl.BlockSpec((pl.Element(1), pl.Element(k), pl.Element(tn)),
                               lambda j, i: (layer, 0, pl.multiple_of(col0 + j * tn, LANES)))],
        out_specs=pl.BlockSpec((tm, tn), lambda j, i: (i, j)),
        scratch_shapes=[pltpu.VMEM((k, tn), jnp.bfloat16)],
        compiler_params=pltpu.CompilerParams(
            dimension_semantics=("arbitrary", "arbitrary"), vmem_limit_bytes=VMEM_LIMIT),
        name="in_proj",
    )(x, w_in)


def _out_kernel(mix_ref, h_ref, w_ref, lnw_ref, lnb_ref, o_ref, wb_ref):
    @pl.when(pl.program_id(0) == 0)
    def _():
        wb_ref[...] = w_ref[...].astype(jnp.bfloat16)

    slab = OUT_TM // OUT_SLABS
    for s in range(OUT_SLABS):
        rows = slice(s * slab, (s + 1) * slab)
        y = jnp.dot(mix_ref[rows, :].astype(jnp.bfloat16), wb_ref[...],
                    preferred_element_type=F32)
        u = ALPHA * h_ref[rows, :] + y
        mu = jnp.mean(u, axis=-1, keepdims=True)
        d = u - mu
        var = jnp.mean(d * d, axis=-1, keepdims=True)
        o_ref[rows, :] = d * lax.rsqrt(var + LN_EPS) * lnw_ref[...] + lnb_ref[...]


def _out_proj_ln(o_mix, h, w_out, layer, lnw, lnb):
    m = h.shape[0]
    tm = OUT_TM
    return pl.pallas_call(
        _out_kernel,
        out_shape=jax.ShapeDtypeStruct((m, D_MODEL), F32),
        grid=(m // tm,),
        in_specs=[pl.BlockSpec((tm, D_MODEL), lambda i: (i, 0)),
                  pl.BlockSpec((tm, D_MODEL), lambda i: (i, 0)),
                  pl.BlockSpec((None, D_MODEL, D_MODEL), lambda i: (layer, 0, 0),
                               pipeline_mode=pl.Buffered(1)),
                  pl.BlockSpec((1, D_MODEL), lambda i: (0, 0)),
                  pl.BlockSpec((1, D_MODEL), lambda i: (0, 0))],
        out_specs=pl.BlockSpec((tm, D_MODEL), lambda i: (i, 0)),
        scratch_shapes=[pltpu.VMEM((D_MODEL, D_MODEL), jnp.bfloat16)],
        compiler_params=pltpu.CompilerParams(
            dimension_semantics=("arbitrary",), vmem_limit_bytes=VMEM_LIMIT),
        name="out_proj_ln",
    )(o_mix, h, w_out, lnw, lnb)


_HGRN_LEVELS = tuple(CHUNK >> (i + 1) for i in range(int(math.log2(CHUNK))))
_N_LEVELS = len(_HGRN_LEVELS)
_PAIR_SL = [slice(p * LANES, (p + 1) * LANES) for p in range(RWKV_PAIRS)]
_HEAD_SL = [slice(h * LANES, (h + 1) * LANES) for h in range(HGRN_HEADS)]

_BUF_SHAPES = (
    ("ar", (RWKV_PAIRS, PAIR_ROWS, LANES), jnp.bfloat16),
    ("bkT", (RWKV_PAIRS, LANES, 2 * PAIR_ROWS), jnp.bfloat16),
    ("vs", (RWKV_PAIRS, PAIR_ROWS, LANES), jnp.bfloat16),
    ("uv", (RWKV_PAIRS, PAIR_ROWS, LANES), jnp.bfloat16),
    ("bhkh", (RWKV_PAIRS, PAIR_ROWS, LANES), jnp.bfloat16),
    ("gall_r", (1, D_RWKV), F32),
    ("bonus", (CHUNK, D_RWKV), F32),
    ("gate_r", (CHUNK, D_RWKV), F32),
    ("vnat", (CHUNK, D_RWKV), F32),
    ("qe", (_N_LEVELS, CHUNK, D_HGRN), jnp.bfloat16),
    ("keT", (_N_LEVELS, HGRN_HEADS // 2, 2 * LANES, PAIR_ROWS), jnp.bfloat16),
    ("qin", (CHUNK, D_HGRN), jnp.bfloat16),
    ("kdec", (CHUNK, D_HGRN), jnp.bfloat16),
    ("ib", (CHUNK, D_HGRN), jnp.bfloat16),
    ("odiag", (CHUNK, D_HGRN), F32),
    ("gate_h", (CHUNK, D_HGRN), F32),
    ("gall_h", (1, D_HGRN), F32),
)


def _head_mask():
    lane = _iota2((PAIR_ROWS, LANES), 1)
    row = _iota2((PAIR_ROWS, LANES), 0)
    return ((lane < RWKV_HEAD) == (row < CHUNK)).astype(F32)


def _rwkv_prepare(first, has_vmix, y_ref, vf_ref, prm, prev_ref, buf):
    (mu_ref, w0_ref, wup_ref, a0_ref, aup_ref, kk_ref, ka_ref, rk_ref, v0_ref, vdn_ref,
     vup_ref) = prm

    def shifted(c0, c1):
        y = y_ref[0, :, c0:c1]
        rolled = pltpu.roll(y, shift=1, axis=0)
        top = rolled[0:8, :]
        prev = jnp.zeros_like(y[0:1, :]) if first else prev_ref[:, c0:c1]
        top = jnp.where(_iota2(top.shape, 0) == 0, prev, top)
        y_prev = jnp.concatenate([top, rolled[8:, :]], axis=0)
        prev_ref[:, c0:c1] = y[CHUNK - 1:CHUNK, :]
        return y + mu_ref[:, c0:c1] * (y_prev - y)

    tail = shifted(4 * D_RWKV, RWKV_COLS)
    wd = tail[:, 0:DECAY_RANK]
    ad = tail[:, DECAY_RANK:]
    w_raw = w0_ref[...] + _bdot(jnp.tanh(wd), wup_ref[...])
    logw = (-math.exp(-0.5) * LOG2E) * _sigmoid(w_raw)
    yield
    a = _sigmoid(a0_ref[...] + _bdot(ad, aup_ref[...]))
    yield
    tri = (_iota2((CHUNK, 3 * CHUNK), 1) % CHUNK <= _iota2((CHUNK, 3 * CHUNK), 0)).astype(F32)
    cl = _split_dot(tri, logw, 3)
    cl_last = cl[CHUNK - 1:CHUNK, :]
    buf["gall_r"][...] = jnp.exp2(cl_last)
    yield
    buf["gate_r"][...] = _silu(shifted(3 * D_RWKV, 4 * D_RWKV))
    yield
    v = shifted(2 * D_RWKV, 3 * D_RWKV)
    if has_vmix:
        gate = _sigmoid(v0_ref[...] + _bdot(_bdot(v, vdn_ref[...]), vup_ref[...]))
        v = v + (vf_ref[0] - v) * gate
    else:
        buf["vnat"][...] = v
    v_b = _bf(v)
    yield
    r = shifted(0, D_RWKV)
    r_t = _bf(r * jnp.exp2(cl))
    yield
    k = shifted(D_RWKV, 2 * D_RWKV)
    kk = k * kk_ref[...]
    k = k * (1.0 + (a - 1.0) * ka_ref[...])
    yield
    kk_sq, rk_sum = _head_sums([kk * kk, r * k * rk_ref[...]])
    kk = kk * lax.rsqrt(jnp.maximum(kk_sq, 1e-24))
    buf["bonus"][...] = rk_sum * v
    yield
    b = kk * a
    a_t = _bf(kk * -jnp.exp2(cl - logw))
    yield
    g_inv = jnp.exp2(-cl)
    b_t = _bf(b * g_inv)
    k_t = _bf(k * g_inv)
    yield
    g_end = jnp.exp2(cl_last - cl)
    b_h = _bf(b * g_end)
    k_h = _bf(k * g_end)
    yield

    mask = _head_mask().astype(jnp.bfloat16)

    def stack(x, p):
        xp = x[:, _PAIR_SL[p]]
        return jnp.concatenate([xp, xp], axis=0) * mask

    for p in range(RWKV_PAIRS):
        sl = _PAIR_SL[p]
        buf["ar"][p] = jnp.concatenate([a_t[:, sl], r_t[:, sl]], axis=0)
        buf["bkT"][p] = jnp.concatenate([stack(b_t, p), stack(k_t, p)], axis=0).T
        buf["bhkh"][p] = jnp.concatenate([b_h[:, sl], k_h[:, sl]], axis=0)
        buf["vs"][p] = stack(v_b, p)
        buf["uv"][p, CHUNK:, :] = v_b[:, sl]
        yield


def _rwkv_chain(half, emit_v, buf, gnw_ref, gnb_ref, o_ref, vf_out_ref, s_ref):
    mask_b = _head_mask().astype(jnp.bfloat16)
    t_idx = _iota2((CHUNK, LANES), 0)
    s_idx = _iota2((CHUNK, LANES), 1) % CHUNK
    strict = (s_idx < t_idx).astype(F32)
    strict_b = strict.astype(jnp.bfloat16)
    incl_b = (s_idx <= t_idx).astype(jnp.bfloat16)
    eye = (s_idx == t_idx).astype(F32)
    prow = _iota2((PAIR_ROWS, LANES), 0)
    pcol = _iota2((PAIR_ROWS, LANES), 1)
    same_head = (prow // RWKV_HEAD) == (pcol // RWKV_HEAD)
    rows = slice(half * CHUNK, (half + 1) * CHUNK)

    def diag(x):
        return jnp.concatenate([x, x], axis=0) * mask_b

    pairs = range(RWKV_PAIRS)
    ar = [buf["ar"][p] for p in pairs]
    sc = [_bdot(ar[p], buf["bkT"][p]) for p in pairs]
    pw = [strict * sc[p][:CHUNK, :LANES] for p in pairs]
    tinv = [eye + pw[p] for p in pairs]
    a_kv = [jnp.concatenate([_bf(sc[p][:CHUNK, LANES:]) * strict_b,
                             _bf(sc[p][CHUNK:, LANES:]) * incl_b], axis=0) for p in pairs]
    a_rb = [_bf(sc[p][CHUNK:, :LANES]) * incl_b for p in pairs]
    pwb = [_bf(x) for x in pw]
    pwd = [diag(x) for x in pwb]
    s_old = [s_ref[p] for p in pairs]
    ars = [_bdot_nt(ar[p], s_old[p]) for p in pairs]
    akv = [_bdot(a_kv[p], buf["vs"][p]) for p in pairs]
    yield
    for it in range(int(math.log2(CHUNK)) - 1):
        if it == 0:
            pw = [_bdot(pwb[p], pwd[p]) for p in pairs]
        else:
            both = [_bdot(jnp.concatenate([pwb[p], _bf(tinv[p])], axis=0), pwd[p]) for p in pairs]
            pw = [x[:CHUNK] for x in both]
            tinv = [tinv[p] + both[p][CHUNK:] for p in pairs]
        pwb = [_bf(x) for x in pw]
        pwd = [diag(x) for x in pwb]
        yield
    tinv = [tinv[p] + _bdot(tinv[p], pwd[p]) for p in pairs]
    yield

    u = [_bf(_bdot(tinv[p], diag(_bf(ars[p][:CHUNK] + akv[p][:CHUNK])))) for p in pairs]
    yield
    for p in pairs:
        buf["uv"][p, :CHUNK, :] = u[p]
        upd = _bdot_tn(buf["uv"][p], buf["bhkh"][p])
        s_ref[p] = jnp.where(same_head, s_old[p] * buf["gall_r"][:, _PAIR_SL[p]] + upd, 0.0)
    o = jnp.concatenate([ars[p][CHUNK:] + _bdot(a_rb[p], diag(u[p])) + akv[p][CHUNK:]
                         for p in pairs], axis=1)
    yield

    d = o - _head_means_split(o)
    yield
    var = _head_sums([d * d], 1.0 / RWKV_HEAD)[0]
    on = d * lax.rsqrt(var + GN_EPS) * gnw_ref[...] + gnb_ref[...]
    o_ref[0, rows, 0:D_RWKV] = (on + buf["bonus"][...]) * buf["gate_r"][...]
    if emit_v:
        vf_out_ref[0, rows, :] = buf["vnat"][...]
    yield


def _hgrn_prepare(layer, main_ref, tail_ref, lbl_ref, gw_ref, buf):
    lg = lbl_ref[...]
    e = jnp.exp(lg - jnp.max(lg, axis=0, keepdims=True))
    sm = e / jnp.sum(e, axis=0, keepdims=True)
    lb = jnp.sum(sm[0:layer + 1, :], axis=0, keepdims=True) - sm[0:1, :]

    def cols(c0, c1):
        lo, hi = RWKV_COLS + c0, RWKV_COLS + c1
        if hi <= IN_MAIN:
            return main_ref[0, :, lo:hi]
        return jnp.concatenate([main_ref[0, :, lo:IN_MAIN], tail_ref[0, :, 0:hi - IN_MAIN]],
                               axis=1)

    sig = _sigmoid(cols(D_HGRN, 2 * D_HGRN))
    log_f = jnp.log2(jnp.maximum(lb, LB_FLOOR) + (1.0 - lb) * sig)
    k = (1.0 - lb) * (1.0 - sig)
    yield

    trow = _iota2((CHUNK, 2 * CHUNK), 0)
    tcol = _iota2((CHUNK, 2 * CHUNK), 1) % CHUNK
    mats = [tcol <= trow]
    for m in _HGRN_LEVELS:
        mid = (trow // (2 * m)) * (2 * m) + m
        after = trow >= mid
        mats.append((after & (tcol >= mid) & (tcol <= trow))
                    | (~after & (tcol > trow) & (tcol < mid)))
    cums = _split_dot(jnp.concatenate(mats, axis=0).astype(F32), log_f, 2)
    b = cums[0:CHUNK, :]
    b_last = b[CHUNK - 1:CHUNK, :]
    buf["gall_h"][...] = jnp.exp2(b_last)
    yield
    q = _silu(cols(0, D_HGRN))
    buf["qin"][...] = _bf(q * jnp.exp2(b))
    q_b = _bf(q)
    yield
    buf["kdec"][...] = _bf(k * jnp.exp2(b_last - b))
    k_b = _bf(k)
    yield
    i_in = cols(2 * D_HGRN, 3 * D_HGRN)
    buf["ib"][...] = _bf(i_in)
    qk = q * k
    for h in range(HGRN_HEADS):
        sl = _HEAD_SL[h]
        buf["odiag"][:, sl] = jnp.sum(qk[:, sl], axis=-1, keepdims=True) * i_in[:, sl]
    yield
    for li in range(_N_LEVELS):
        e = _bf(jnp.exp2(cums[(li + 1) * CHUNK:(li + 2) * CHUNK, :]))
        buf["qe"][li] = q_b * e
        yield
        ke = k_b * e
        for hp in range(HGRN_HEADS // 2):
            buf["keT"][li, hp] = _diag2(ke[:, _HEAD_SL[2 * hp]], ke[:, _HEAD_SL[2 * hp + 1]]).T
        yield
    buf["gate_h"][...] = gw_ref[...] * _silu(cols(3 * D_HGRN, 4 * D_HGRN))
    yield


def _diag2(x0, x1):
    return jnp.concatenate([jnp.concatenate([x0, jnp.zeros_like(x1)], axis=1),
                            jnp.concatenate([jnp.zeros_like(x0), x1], axis=1)], axis=0)


def _hgrn_chain(half, buf, o_ref, s_ref):
    trow = _iota2((CHUNK, LANES), 0)
    tcol = _iota2((CHUNK, LANES), 1) % CHUNK
    rows = slice(half * CHUNK, (half + 1) * CHUNK)
    heads = range(HGRN_HEADS)
    pairs = range(HGRN_HEADS // 2)
    pair_sl = [slice(2 * hp * LANES, 2 * (hp + 1) * LANES) for hp in pairs]

    att = [None] * len(pairs)
    yield
    for li, m in enumerate(_HGRN_LEVELS):
        mid = (trow // (2 * m)) * (2 * m) + m
        keep = ((trow >= mid) & (tcol < mid) & (tcol // (2 * m) == trow // (2 * m))).astype(F32)
        qe = buf["qe"][li]
        for hp in pairs:
            part = keep * _bdot(qe[:, pair_sl[hp]], buf["keT"][li, hp])
            att[hp] = part if att[hp] is None else att[hp] + part
        yield
    s_old = [s_ref[h] for h in heads]
    i_b = buf["ib"][...]
    q_in = buf["qin"][...]
    k_dec = buf["kdec"][...]
    o_2 = []
    for hp in pairs:
        h0, h1 = 2 * hp, 2 * hp + 1
        i2 = _diag2(i_b[:, _HEAD_SL[h0]], i_b[:, _HEAD_SL[h1]])
        inter = [_bdot_nt(q_in[:, _HEAD_SL[h]], s_old[h]) for h in (h0, h1)]
        o_2.append(_bdot(att[hp], i2) + jnp.concatenate(inter, axis=1))
    for h in heads:
        sl = _HEAD_SL[h]
        s_ref[h] = s_old[h] * buf["gall_h"][:, sl] + _bdot_tn(i_b[:, sl], k_dec[:, sl])
    yield
    for h in heads:
        sl = _HEAD_SL[h]
        o = o_2[h // 2][:, (h % 2) * LANES:(h % 2 + 1) * LANES] + buf["odiag"][:, sl]
        o = o * lax.rsqrt(jnp.mean(o * o, axis=-1, keepdims=True) + RMS_EPS)
        o_ref[0, rows, D_RWKV + h * LANES:D_RWKV + (h + 1) * LANES] = o * buf["gate_h"][:, sl]
    yield


_STAGE_ORDER = (0, 1, 0, 1, 0, 1, 0, 1, 0, 1, 0, 1, 0, 1, 0, 1, 0, 1, 0, 0)
_DONE = object()


def _mixer_kernel(layer, *refs):
    has_vmix = layer > 0
    n_in = 3 * (3 if has_vmix else 2)
    n_prm = 13 if has_vmix else 10
    proj_refs = list(zip(refs[0:3], refs[3:6]))
    vf_refs = refs[6:9] if has_vmix else (None, None, None)
    prm = list(refs[n_in:n_in + n_prm])
    (mu_ref, w0_ref, wup_ref, a0_ref, aup_ref, kk_ref, ka_ref, rk_ref, gnw_ref, gnb_ref) = prm[:10]
    vm = prm[10:13] if has_vmix else [None, None, None]
    prep_prm = (mu_ref, w0_ref, wup_ref, a0_ref, aup_ref, kk_ref, ka_ref, rk_ref, *vm)
    pos = n_in + n_prm
    lbl_ref, gw_ref, o_ref = refs[pos:pos + 3]
    pos += 3
    vf_out_ref = None
    if not has_vmix:
        vf_out_ref = refs[pos]
        pos += 1
    s_r_ref, prev_ref, s_h_ref = refs[pos:pos + 3]
    pos += 3
    nb = len(_BUF_SHAPES)
    bufs = [dict(zip([n for n, _, _ in _BUF_SHAPES], refs[pos + i * nb:pos + (i + 1) * nb]))
            for i in range(2)]

    def prepare(first, proj_ref, vf_ref, buf):
        main_ref, tail_ref = proj_ref
        return [_rwkv_prepare(first, has_vmix, main_ref, vf_ref, prep_prm, prev_ref, buf),
                _hgrn_prepare(layer, main_ref, tail_ref, lbl_ref, gw_ref, buf)]

    def chain(half, buf):
        gens = (_rwkv_chain(half, not has_vmix, buf, gnw_ref, gnb_ref, o_ref, vf_out_ref, s_r_ref),
                _hgrn_chain(half, buf, o_ref, s_h_ref))
        for g in _STAGE_ORDER:
            next(gens[g])
            yield

    def run(chain_gen, prep_gens):
        live = [chain_gen] + list(prep_gens)
        while live:
            for g in list(live):
                if next(g, _DONE) is _DONE:
                    live.remove(g)

    @pl.when(pl.program_id(1) == 0)
    def _():
        s_r_ref[...] = jnp.zeros_like(s_r_ref)
        s_h_ref[...] = jnp.zeros_like(s_h_ref)
        run(iter(()), prepare(True, proj_refs[0], vf_refs[0], bufs[0]))

    run(chain(0, bufs[0]), prepare(False, proj_refs[1], vf_refs[1], bufs[1]))
    run(chain(1, bufs[1]), prepare(False, proj_refs[2], vf_refs[2], bufs[0]))


def _mixer_layer(proj, proj_tail, layer, rwkv_params, vmix, v_first, lb_logits, g_norm_w):
    bsz, seq, _ = proj.shape
    has_vmix = layer > 0
    nc = seq // CHUNK
    row_spec = lambda n: pl.BlockSpec((1, n), lambda b, j: (0, 0))
    full_spec = lambda s: pl.BlockSpec(s, lambda b, j: (0, 0))
    chunk_specs = lambda n: [
        pl.BlockSpec((1, CHUNK, n), lambda b, j: (b, 0, 0)),
        pl.BlockSpec((1, CHUNK, n), lambda b, j: (b, 2 * j + 1, 0)),
        pl.BlockSpec((1, CHUNK, n), lambda b, j: (b, jnp.minimum(2 * j + 2, nc - 1), 0))]
    step_spec = lambda n: pl.BlockSpec((1, 2 * CHUNK, n), lambda b, j: (b, j, 0))
    args = [proj] * 3 + [proj_tail] * 3
    in_specs = chunk_specs(IN_MAIN) + chunk_specs(IN_COLS - IN_MAIN)
    if has_vmix:
        args += [v_first] * 3
        in_specs += chunk_specs(D_RWKV)
    args += list(rwkv_params)
    in_specs += [row_spec(RWKV_COLS), row_spec(D_RWKV), full_spec((DECAY_RANK, D_RWKV)),
                 row_spec(D_RWKV), full_spec((A_RANK, D_RWKV)), row_spec(D_RWKV),
                 row_spec(D_RWKV), row_spec(D_RWKV), row_spec(D_RWKV), row_spec(D_RWKV)]
    out_sds = jax.ShapeDtypeStruct((bsz, seq, D_MODEL), F32)
    if has_vmix:
        args += list(vmix)
        in_specs += [row_spec(D_RWKV), full_spec((D_RWKV, VRES_RANK)),
                     full_spec((VRES_RANK, D_RWKV))]
        out_shape, out_specs = out_sds, step_spec(D_MODEL)
    else:
        out_shape = (out_sds, jax.ShapeDtypeStruct((bsz, seq, D_RWKV), F32))
        out_specs = (step_spec(D_MODEL), step_spec(D_RWKV))
    args += [lb_logits, g_norm_w]
    in_specs += [full_spec((DEPTH, D_HGRN)), row_spec(D_HGRN)]
    scratch = [pltpu.VMEM((RWKV_PAIRS, PAIR_ROWS, LANES), F32),
               pltpu.VMEM((1, RWKV_COLS), F32),
               pltpu.VMEM((HGRN_HEADS, HGRN_EXPAND, LANES), F32)]
    scratch += [pltpu.VMEM(shape, dtype) for _ in range(2) for _, shape, dtype in _BUF_SHAPES]
    res = pl.pallas_call(
        functools.partial(_mixer_kernel, layer),
        out_shape=out_shape,
        grid=(bsz, nc // 2),
        in_specs=in_specs,
        out_specs=out_specs,
        scratch_shapes=scratch,
        compiler_params=pltpu.CompilerParams(
            dimension_semantics=("arbitrary", "arbitrary"), vmem_limit_bytes=VMEM_LIMIT),
        name="mixer",
    )(*args)
    if has_vmix:
        return res, v_first
    return res


def kernel(x, w_in, shift_mu, w_decay0, w_decay_up, a0, a_up, k_k, k_a, r_k, ln_x_w, ln_x_b,
           v_mix0, v_mix_down, v_mix_up, lb_logits, g_norm_w, w_out, ln_w, ln_b):
    out_dtype = x.dtype
    bsz, seq, _ = x.shape
    h = x.astype(F32).reshape(bsz * seq, D_MODEL)
    lb_logits = lb_logits.astype(F32)
    row = lambda t: t.reshape(1, -1)
    v_first = None
    for l in range(DEPTH):
        proj = _in_proj(h, w_in, l, 0, IN_MAIN, IN_TN).reshape(bsz, seq, IN_MAIN)
        tail = _in_proj(h, w_in, l, IN_MAIN, IN_COLS - IN_MAIN, IN_COLS - IN_MAIN)
        tail = tail.reshape(bsz, seq, IN_COLS - IN_MAIN)
        params = (row(shift_mu[l]), row(w_decay0[l]), w_decay_up[l], row(a0[l]), a_up[l],
                  row(k_k[l]), row(k_a[l]), row(r_k[l]), row(ln_x_w[l]), row(ln_x_b[l]))
        vmix = None if l == 0 else (row(v_mix0[l - 1]), v_mix_down[l - 1], v_mix_up[l - 1])
        o_mix, v_first = _mixer_layer(proj, tail, l, params, vmix, v_first, lb_logits,
                                      row(g_norm_w[l]))
        h = _out_proj_ln(o_mix.reshape(bsz * seq, D_MODEL), h, w_out, l, row(ln_w[l]),
                         row(ln_b[l]))
    return h.reshape(bsz, seq, D_MODEL).astype(out_dtype)
```

```python
import functools
import math

import jax
import jax.numpy as jnp
from jax import lax
from jax.experimental import pallas as pl
from jax.experimental.pallas import tpu as pltpu

D_MODEL = 2048
DEPTH = 2
D_RWKV = D_MODEL // 2
D_HGRN = D_MODEL - D_RWKV
RWKV_HEAD = 64
DECAY_RANK = 64
A_RANK = 64
VRES_RANK = 32
HGRN_EXPAND = 128
HGRN_HEADS = D_HGRN // HGRN_EXPAND
RWKV_COLS = 4 * D_RWKV + DECAY_RANK + A_RANK
HGRN_COLS = 4 * D_HGRN
IN_COLS = RWKV_COLS + HGRN_COLS
ALPHA = (2 * DEPTH) ** 0.25
LN_EPS = 1e-5
GN_EPS = 64e-5
RMS_EPS = 1e-5
LB_FLOOR = 1e-30
LOG2E = math.log2(math.e)

LANES = 128
CHUNK = 64
PAIR_ROWS = 2 * CHUNK
RWKV_PAIRS = D_RWKV // LANES
VMEM_LIMIT = 56 * 1024 * 1024
MXU_WIDTH = 256
IN_TM = 512
IN_TN = 6 * MXU_WIDTH
IN_MAIN = (IN_COLS // IN_TN) * IN_TN
OUT_TM, OUT_SLABS = 512, 2

F32 = jnp.float32


def _bf(x):
    return x.astype(jnp.bfloat16)


def _bdot(a, b):
    return jnp.dot(_bf(a), _bf(b), preferred_element_type=F32)


def _bdot_nt(a, b):
    return lax.dot_general(_bf(a), _bf(b), (((1,), (1,)), ((), ())), preferred_element_type=F32)


def _bdot_tn(a, b):
    return lax.dot_general(_bf(a), _bf(b), (((0,), (0,)), ((), ())), preferred_element_type=F32)


def _split_dot(m01, x, terms):
    pieces = []
    rem = x
    for t in range(terms):
        pieces.append(_bf(rem))
        if t + 1 < terms:
            rem = rem - pieces[-1].astype(F32)
    return jnp.dot(_bf(m01), jnp.concatenate(pieces, axis=0), preferred_element_type=F32)


def _sigmoid(x):
    return 0.5 * jnp.tanh(0.5 * x) + 0.5


def _head_sums(xs, scale=1.0):
    rows, cols = xs[0].shape
    width = len(xs) * LANES
    ones = ((_iota2((width, width), 0) // RWKV_HEAD) == (_iota2((width, width), 1) // RWKV_HEAD))
    ones = jnp.where(ones, scale, 0.0).astype(jnp.bfloat16)
    tiles = range(cols // LANES)
    lhs = jnp.concatenate(
        [jnp.concatenate([x[:, t * LANES:(t + 1) * LANES] for t in tiles], axis=0) for x in xs],
        axis=1)
    sums = jnp.dot(_bf(lhs), ones, preferred_element_type=F32)
    return [jnp.concatenate([sums[t * rows:(t + 1) * rows, i * LANES:(i + 1) * LANES]
                             for t in tiles], axis=1) for i in range(len(xs))]


def _head_means_split(x):
    rows, cols = x.shape
    hi = _bf(x)
    lo = _bf(x - hi.astype(F32))
    same = ((_iota2((2 * LANES, LANES), 0) % LANES) // RWKV_HEAD
            == _iota2((2 * LANES, LANES), 1) // RWKV_HEAD)
    ones = jnp.where(same, 1.0 / RWKV_HEAD, 0.0).astype(jnp.bfloat16)
    tiles = range(cols // LANES)
    lhs = jnp.concatenate(
        [jnp.concatenate([p[:, t * LANES:(t + 1) * LANES] for t in tiles], axis=0)
         for p in (hi, lo)], axis=1)
    sums = jnp.dot(lhs, ones, preferred_element_type=F32)
    return jnp.concatenate([sums[t * rows:(t + 1) * rows, :] for t in tiles], axis=1)


def _silu(x):
    return x * _sigmoid(x)


def _iota2(shape, dim):
    return lax.broadcasted_iota(jnp.int32, shape, dim)


def _mm_kernel(x_ref, w_ref, o_ref, wb_ref):
    @pl.when(pl.program_id(1) == 0)
    def _():
        wb_ref[...] = w_ref[0].astype(jnp.bfloat16)

    o_ref[...] = jnp.dot(x_ref[...].astype(jnp.bfloat16), wb_ref[...],
                         preferred_element_type=F32)


def _in_proj(x, w_in, layer, col0, n, tn, tm):
    m, k = x.shape
    return pl.pallas_call(
        _mm_kernel,
        out_shape=jax.ShapeDtypeStruct((m, n), F32),
        grid=(n // tn, m // tm),
        in_specs=[pl.BlockSpec((tm, k), lambda j, i: (i, 0)),
                  pl.BlockSpec((pl.Element(1), pl.Element(k), pl.Element(tn)),
                               lambda j, i: (layer, 0, pl.multiple_of(col0 + j * tn, LANES)))],
        out_specs=pl.BlockSpec((tm, tn), lambda j, i: (i, j)),
        scratch_shapes=[pltpu.VMEM((k, tn), jnp.bfloat16)],
        compiler_params=pltpu.CompilerParams(
            dimension_semantics=("arbitrary", "arbitrary"), vmem_limit_bytes=VMEM_LIMIT),
        name="in_proj",
    )(x, w_in)


def _out_kernel(mix_ref, h_ref, w_ref, lnw_ref, lnb_ref, o_ref, wb_ref):
    @pl.when(pl.program_id(0) == 0)
    def _():
        wb_ref[...] = w_ref[...].astype(jnp.bfloat16)

    slab = OUT_TM // OUT_SLABS
    for s in range(OUT_SLABS):
        rows = slice(s * slab, (s + 1) * slab)
        y = jnp.dot(mix_ref[rows, :].astype(jnp.bfloat16), wb_ref[...],
                    preferred_element_type=F32)
        u = ALPHA * h_ref[rows, :] + y
        mu = jnp.mean(u, axis=-1, keepdims=True)
        d = u - mu
        var = jnp.mean(d * d, axis=-1, keepdims=True)
        o_ref[rows, :] = d * lax.rsqrt(var + LN_EPS) * lnw_ref[...] + lnb_ref[...]


def _out_proj_ln(o_mix, h, w_out, layer, lnw, lnb):
    m = h.shape[0]
    tm = OUT_TM
    return pl.pallas_call(
        _out_kernel,
        out_shape=jax.ShapeDtypeStruct((m, D_MODEL), F32),
        grid=(m // tm,),
        in_specs=[pl.BlockSpec((tm, D_MODEL), lambda i: (i, 0)),
                  pl.BlockSpec((tm, D_MODEL), lambda i: (i, 0)),
                  pl.BlockSpec((None, D_MODEL, D_MODEL), lambda i: (layer, 0, 0),
                               pipeline_mode=pl.Buffered(1)),
                  pl.BlockSpec((1, D_MODEL), lambda i: (0, 0)),
                  pl.BlockSpec((1, D_MODEL), lambda i: (0, 0))],
        out_specs=pl.BlockSpec((tm, D_MODEL), lambda i: (i, 0)),
        scratch_shapes=[pltpu.VMEM((D_MODEL, D_MODEL), jnp.bfloat16)],
        compiler_params=pltpu.CompilerParams(
            dimension_semantics=("arbitrary",), vmem_limit_bytes=VMEM_LIMIT),
        name="out_proj_ln",
    )(o_mix, h, w_out, lnw, lnb)


_HGRN_LEVELS = tuple(CHUNK >> (i + 1) for i in range(int(math.log2(CHUNK))))
_N_LEVELS = len(_HGRN_LEVELS)
_PAIR_SL = [slice(p * LANES, (p + 1) * LANES) for p in range(RWKV_PAIRS)]
_HEAD_SL = [slice(h * LANES, (h + 1) * LANES) for h in range(HGRN_HEADS)]

_BUF_SHAPES = (
    ("ar", (RWKV_PAIRS, PAIR_ROWS, LANES), jnp.bfloat16),
    ("bkT", (RWKV_PAIRS, LANES, 2 * PAIR_ROWS), jnp.bfloat16),
    ("vs", (RWKV_PAIRS, PAIR_ROWS, LANES), jnp.bfloat16),
    ("uv", (RWKV_PAIRS, PAIR_ROWS, LANES), jnp.bfloat16),
    ("bhkh", (RWKV_PAIRS, PAIR_ROWS, LANES), jnp.bfloat16),
    ("gall_r", (1, D_RWKV), F32),
    ("bonus", (CHUNK, D_RWKV), F32),
    ("gate_r", (CHUNK, D_RWKV), F32),
    ("vnat", (CHUNK, D_RWKV), F32),
    ("qe", (_N_LEVELS, CHUNK, D_HGRN), jnp.bfloat16),
    ("keT", (_N_LEVELS, HGRN_HEADS // 2, 2 * LANES, PAIR_ROWS), jnp.bfloat16),
    ("qin", (CHUNK, D_HGRN), jnp.bfloat16),
    ("kdec", (CHUNK, D_HGRN), jnp.bfloat16),
    ("ib", (CHUNK, D_HGRN), jnp.bfloat16),
    ("odiag", (CHUNK, D_HGRN), F32),
    ("gate_h", (CHUNK, D_HGRN), F32),
    ("gall_h", (1, D_HGRN), F32),
)


def _head_mask():
    lane = _iota2((PAIR_ROWS, LANES), 1)
    row = _iota2((PAIR_ROWS, LANES), 0)
    return ((lane < RWKV_HEAD) == (row < CHUNK)).astype(F32)


def _rwkv_prepare(first, has_vmix, y_ref, vf_ref, prm, prev_ref, buf):
    (mu_ref, w0_ref, wup_ref, a0_ref, aup_ref, kk_ref, ka_ref, rk_ref, v0_ref, vdn_ref,
     vup_ref) = prm

    def shifted(c0, c1):
        y = y_ref[0, :, c0:c1]
        rolled = pltpu.roll(y, shift=1, axis=0)
        top = rolled[0:8, :]
        prev = jnp.zeros_like(y[0:1, :]) if first else prev_ref[:, c0:c1]
        top = jnp.where(_iota2(top.shape, 0) == 0, prev, top)
        y_prev = jnp.concatenate([top, rolled[8:, :]], axis=0)
        prev_ref[:, c0:c1] = y[CHUNK - 1:CHUNK, :]
        return y + mu_ref[:, c0:c1] * (y_prev - y)

    tail = shifted(4 * D_RWKV, RWKV_COLS)
    wd = tail[:, 0:DECAY_RANK]
    ad = tail[:, DECAY_RANK:]
    w_raw = w0_ref[...] + _bdot(jnp.tanh(wd), wup_ref[...])
    logw = (-math.exp(-0.5) * LOG2E) * _sigmoid(w_raw)
    yield
    a = _sigmoid(a0_ref[...] + _bdot(ad, aup_ref[...]))
    yield
    tri = (_iota2((CHUNK, 3 * CHUNK), 1) % CHUNK <= _iota2((CHUNK, 3 * CHUNK), 0)).astype(F32)
    cl = _split_dot(tri, logw, 3)
    cl_last = cl[CHUNK - 1:CHUNK, :]
    buf["gall_r"][...] = jnp.exp2(cl_last)
    yield
    buf["gate_r"][...] = _silu(shifted(3 * D_RWKV, 4 * D_RWKV))
    yield
    v = shifted(2 * D_RWKV, 3 * D_RWKV)
    if has_vmix:
        gate = _sigmoid(v0_ref[...] + _bdot(_bdot(v, vdn_ref[...]), vup_ref[...]))
        v = v + (vf_ref[0] - v) * gate
    else:
        buf["vnat"][...] = v
    v_b = _bf(v)
    yield
    r = shifted(0, D_RWKV)
    r_t = _bf(r * jnp.exp2(cl))
    yield
    k = shifted(D_RWKV, 2 * D_RWKV)
    kk = k * kk_ref[...]
    k = k * (1.0 + (a - 1.0) * ka_ref[...])
    yield
    kk_sq, rk_sum = _head_sums([kk * kk, r * k * rk_ref[...]])
    kk = kk * lax.rsqrt(jnp.maximum(kk_sq, 1e-24))
    buf["bonus"][...] = rk_sum * v
    yield
    b = kk * a
    a_t = _bf(kk * -jnp.exp2(cl - logw))
    yield
    g_inv = jnp.exp2(-cl)
    b_t = _bf(b * g_inv)
    k_t = _bf(k * g_inv)
    yield
    g_end = jnp.exp2(cl_last - cl)
    b_h = _bf(b * g_end)
    k_h = _bf(k * g_end)
    yield

    mask = _head_mask().astype(jnp.bfloat16)

    def stack(x, p):
        xp = x[:, _PAIR_SL[p]]
        return jnp.concatenate([xp, xp], axis=0) * mask

    for p in range(RWKV_PAIRS):
        sl = _PAIR_SL[p]
        buf["ar"][p] = jnp.concatenate([a_t[:, sl], r_t[:, sl]], axis=0)
        buf["bkT"][p] = jnp.concatenate([stack(b_t, p), stack(k_t, p)], axis=0).T
        buf["bhkh"][p] = jnp.concatenate([b_h[:, sl], k_h[:, sl]], axis=0)
        buf["vs"][p] = stack(v_b, p)
        buf["uv"][p, CHUNK:, :] = v_b[:, sl]
        yield


def _rwkv_chain(half, emit_v, buf, gnw_ref, gnb_ref, o_ref, vf_out_ref, s_ref, st_ref):
    mask_b = _head_mask().astype(jnp.bfloat16)
    t_idx = _iota2((CHUNK, LANES), 0)
    s_idx = _iota2((CHUNK, LANES), 1) % CHUNK
    strict = (s_idx < t_idx).astype(F32)
    strict_b = strict.astype(jnp.bfloat16)
    incl_b = (s_idx <= t_idx).astype(jnp.bfloat16)
    eye = (s_idx == t_idx).astype(F32)
    prow = _iota2((PAIR_ROWS, LANES), 0)
    pcol = _iota2((PAIR_ROWS, LANES), 1)
    same_head = (prow // RWKV_HEAD) == (pcol // RWKV_HEAD)
    rows = slice(half * CHUNK, (half + 1) * CHUNK)

    def diag(x):
        return jnp.concatenate([x, x], axis=0) * mask_b

    pairs = range(RWKV_PAIRS)
    ar = [buf["ar"][p] for p in pairs]
    sc = [_bdot(ar[p], buf["bkT"][p]) for p in pairs]
    pw = [strict * sc[p][:CHUNK, :LANES] for p in pairs]
    tinv = [eye + pw[p] for p in pairs]
    a_kv = [jnp.concatenate([_bf(sc[p][:CHUNK, LANES:]) * strict_b,
                             _bf(sc[p][CHUNK:, LANES:]) * incl_b], axis=0) for p in pairs]
    a_rb = [_bf(sc[p][CHUNK:, :LANES]) * incl_b for p in pairs]
    pwb = [_bf(x) for x in pw]
    pwd = [diag(x) for x in pwb]
    s_old = [s_ref[p] for p in pairs]
    ars = [_bdot(ar[p], st_ref[p]) for p in pairs]
    akv = [_bdot(a_kv[p], buf["vs"][p]) for p in pairs]
    yield
    for it in range(int(math.log2(CHUNK)) - 1):
        if it == 0:
            pw = [_bdot(pwb[p], pwd[p]) for p in pairs]
        else:
            both = [_bdot(jnp.concatenate([pwb[p], _bf(tinv[p])], axis=0), pwd[p]) for p in pairs]
            pw = [x[:CHUNK] for x in both]
            tinv = [tinv[p] + both[p][CHUNK:] for p in pairs]
        pwb = [_bf(x) for x in pw]
        pwd = [diag(x) for x in pwb]
        yield
    tinv = [tinv[p] + _bdot(tinv[p], pwd[p]) for p in pairs]
    yield

    u = [_bf(_bdot(tinv[p], diag(_bf(ars[p][:CHUNK] + akv[p][:CHUNK])))) for p in pairs]
    yield
    for p in pairs:
        buf["uv"][p, :CHUNK, :] = u[p]
        upd = _bdot_tn(buf["uv"][p], buf["bhkh"][p])
        s_new = jnp.where(same_head, s_old[p] * buf["gall_r"][:, _PAIR_SL[p]] + upd, 0.0)
        s_ref[p] = s_new
        st_ref[p] = _bf(s_new).T
    o = jnp.concatenate([ars[p][CHUNK:] + _bdot(a_rb[p], diag(u[p])) + akv[p][CHUNK:]
                         for p in pairs], axis=1)
    yield

    d = o - _head_means_split(o)
    yield
    var = _head_sums([d * d], 1.0 / RWKV_HEAD)[0]
    on = d * lax.rsqrt(var + GN_EPS) * gnw_ref[...] + gnb_ref[...]
    o_ref[0, rows, 0:D_RWKV] = (on + buf["bonus"][...]) * buf["gate_r"][...]
    if emit_v:
        vf_out_ref[0, rows, :] = buf["vnat"][...]
    yield


def _hgrn_prepare(layer, main_ref, tail_ref, lbl_ref, gw_ref, buf):
    lg = lbl_ref[...]
    e = jnp.exp(lg - jnp.max(lg, axis=0, keepdims=True))
    sm = e / jnp.sum(e, axis=0, keepdims=True)
    lb = jnp.sum(sm[0:layer + 1, :], axis=0, keepdims=True) - sm[0:1, :]

    def cols(c0, c1):
        lo, hi = RWKV_COLS + c0, RWKV_COLS + c1
        if hi <= IN_MAIN:
            return main_ref[0, :, lo:hi]
        return jnp.concatenate([main_ref[0, :, lo:IN_MAIN], tail_ref[0, :, 0:hi - IN_MAIN]],
                               axis=1)

    sig = _sigmoid(cols(D_HGRN, 2 * D_HGRN))
    log_f = jnp.log2(jnp.maximum(lb, LB_FLOOR) + (1.0 - lb) * sig)
    k = (1.0 - lb) * (1.0 - sig)
    yield

    trow = _iota2((CHUNK, 2 * CHUNK), 0)
    tcol = _iota2((CHUNK, 2 * CHUNK), 1) % CHUNK
    mats = [tcol <= trow]
    for m in _HGRN_LEVELS:
        mid = (trow // (2 * m)) * (2 * m) + m
        after = trow >= mid
        mats.append((after & (tcol >= mid) & (tcol <= trow))
                    | (~after & (tcol > trow) & (tcol < mid)))
    cums = _split_dot(jnp.concatenate(mats, axis=0).astype(F32), log_f, 2)
    b = cums[0:CHUNK, :]
    b_last = b[CHUNK - 1:CHUNK, :]
    buf["gall_h"][...] = jnp.exp2(b_last)
    yield
    q = _silu(cols(0, D_HGRN))
    buf["qin"][...] = _bf(q * jnp.exp2(b))
    q_b = _bf(q)
    yield
    buf["kdec"][...] = _bf(k * jnp.exp2(b_last - b))
    k_b = _bf(k)
    yield
    i_in = cols(2 * D_HGRN, 3 * D_HGRN)
    buf["ib"][...] = _bf(i_in)
    qk = q * k
    for h in range(HGRN_HEADS):
        sl = _HEAD_SL[h]
        buf["odiag"][:, sl] = jnp.sum(qk[:, sl], axis=-1, keepdims=True) * i_in[:, sl]
    yield
    for li in range(_N_LEVELS):
        e = _bf(jnp.exp2(cums[(li + 1) * CHUNK:(li + 2) * CHUNK, :]))
        buf["qe"][li] = q_b * e
        yield
        ke = k_b * e
        for hp in range(HGRN_HEADS // 2):
            buf["keT"][li, hp] = _diag2(ke[:, _HEAD_SL[2 * hp]], ke[:, _HEAD_SL[2 * hp + 1]]).T
        yield
    buf["gate_h"][...] = gw_ref[...] * _silu(cols(3 * D_HGRN, 4 * D_HGRN))
    yield


def _diag2(x0, x1):
    return jnp.concatenate([jnp.concatenate([x0, jnp.zeros_like(x1)], axis=1),
                            jnp.concatenate([jnp.zeros_like(x0), x1], axis=1)], axis=0)


def _hgrn_chain(half, buf, o_ref, s_ref, st_ref):
    trow = _iota2((CHUNK, LANES), 0)
    tcol = _iota2((CHUNK, LANES), 1) % CHUNK
    rows = slice(half * CHUNK, (half + 1) * CHUNK)
    heads = range(HGRN_HEADS)
    pairs = range(HGRN_HEADS // 2)
    pair_sl = [slice(2 * hp * LANES, 2 * (hp + 1) * LANES) for hp in pairs]

    att = [None] * len(pairs)
    yield
    for li, m in enumerate(_HGRN_LEVELS):
        mid = (trow // (2 * m)) * (2 * m) + m
        keep = ((trow >= mid) & (tcol < mid) & (tcol // (2 * m) == trow // (2 * m))).astype(F32)
        qe = buf["qe"][li]
        for hp in pairs:
            part = keep * _bdot(qe[:, pair_sl[hp]], buf["keT"][li, hp])
            att[hp] = part if att[hp] is None else att[hp] + part
        yield
    s_old = [s_ref[h] for h in heads]
    i_b = buf["ib"][...]
    q_in = buf["qin"][...]
    k_dec = buf["kdec"][...]
    o_2 = []
    for hp in pairs:
        h0, h1 = 2 * hp, 2 * hp + 1
        i2 = _diag2(i_b[:, _HEAD_SL[h0]], i_b[:, _HEAD_SL[h1]])
        inter = [_bdot(q_in[:, _HEAD_SL[h]], st_ref[h]) for h in (h0, h1)]
        o_2.append(_bdot(att[hp], i2) + jnp.concatenate(inter, axis=1))
    for h in heads:
        sl = _HEAD_SL[h]
        s_new = s_old[h] * buf["gall_h"][:, sl] + _bdot_tn(i_b[:, sl], k_dec[:, sl])
        s_ref[h] = s_new
        st_ref[h] = _bf(s_new).T
    yield
    for h in heads:
        sl = _HEAD_SL[h]
        o = o_2[h // 2][:, (h % 2) * LANES:(h % 2 + 1) * LANES] + buf["odiag"][:, sl]
        o = o * lax.rsqrt(jnp.mean(o * o, axis=-1, keepdims=True) + RMS_EPS)
        o_ref[0, rows, D_RWKV + h * LANES:D_RWKV + (h + 1) * LANES] = o * buf["gate_h"][:, sl]
    yield


_STAGE_ORDER = (0, 1, 0, 1, 0, 1, 0, 1, 0, 1, 0, 1, 0, 1, 0, 1, 0, 1, 0, 0)
_DONE = object()


def _mixer_kernel(layer, *refs):
    has_vmix = layer > 0
    n_in = 3 * (3 if has_vmix else 2)
    n_prm = 13 if has_vmix else 10
    proj_refs = list(zip(refs[0:3], refs[3:6]))
    vf_refs = refs[6:9] if has_vmix else (None, None, None)
    prm = list(refs[n_in:n_in + n_prm])
    (mu_ref, w0_ref, wup_ref, a0_ref, aup_ref, kk_ref, ka_ref, rk_ref, gnw_ref, gnb_ref) = prm[:10]
    vm = prm[10:13] if has_vmix else [None, None, None]
    prep_prm = (mu_ref, w0_ref, wup_ref, a0_ref, aup_ref, kk_ref, ka_ref, rk_ref, *vm)
    pos = n_in + n_prm
    lbl_ref, gw_ref, o_ref = refs[pos:pos + 3]
    pos += 3
    vf_out_ref = None
    if not has_vmix:
        vf_out_ref = refs[pos]
        pos += 1
    s_r_ref, prev_ref, s_h_ref, st_r_ref, st_h_ref = refs[pos:pos + 5]
    pos += 5
    nb = len(_BUF_SHAPES)
    bufs = [dict(zip([n for n, _, _ in _BUF_SHAPES], refs[pos + i * nb:pos + (i + 1) * nb]))
            for i in range(2)]

    def prepare(first, proj_ref, vf_ref, buf):
        main_ref, tail_ref = proj_ref
        return [_rwkv_prepare(first, has_vmix, main_ref, vf_ref, prep_prm, prev_ref, buf),
                _hgrn_prepare(layer, main_ref, tail_ref, lbl_ref, gw_ref, buf)]

    def chain(half, buf):
        gens = (_rwkv_chain(half, not has_vmix, buf, gnw_ref, gnb_ref, o_ref, vf_out_ref, s_r_ref,
                            st_r_ref),
                _hgrn_chain(half, buf, o_ref, s_h_ref, st_h_ref))
        for g in _STAGE_ORDER:
            next(gens[g])
            yield

    def run(chain_gen, prep_gens):
        live = [chain_gen] + list(prep_gens)
        while live:
            for g in list(live):
                if next(g, _DONE) is _DONE:
                    live.remove(g)

    @pl.when(pl.program_id(1) == 0)
    def _():
        s_r_ref[...] = jnp.zeros_like(s_r_ref)
        s_h_ref[...] = jnp.zeros_like(s_h_ref)
        st_r_ref[...] = jnp.zeros_like(st_r_ref)
        st_h_ref[...] = jnp.zeros_like(st_h_ref)
        run(iter(()), prepare(True, proj_refs[0], vf_refs[0], bufs[0]))

    run(chain(0, bufs[0]), prepare(False, proj_refs[1], vf_refs[1], bufs[1]))
    run(chain(1, bufs[1]), prepare(False, proj_refs[2], vf_refs[2], bufs[0]))


def _mixer_layer(proj, proj_tail, layer, rwkv_params, vmix, v_first, lb_logits, g_norm_w):
    bsz, seq, _ = proj.shape
    has_vmix = layer > 0
    nc = seq // CHUNK
    row_spec = lambda n: pl.BlockSpec((1, n), lambda b, j: (0, 0))
    full_spec = lambda s: pl.BlockSpec(s, lambda b, j: (0, 0))
    chunk_specs = lambda n: [
        pl.BlockSpec((1, CHUNK, n), lambda b, j: (b, 0, 0)),
        pl.BlockSpec((1, CHUNK, n), lambda b, j: (b, 2 * j + 1, 0)),
        pl.BlockSpec((1, CHUNK, n), lambda b, j: (b, jnp.minimum(2 * j + 2, nc - 1), 0))]
    step_spec = lambda n: pl.BlockSpec((1, 2 * CHUNK, n), lambda b, j: (b, j, 0))
    args = [proj] * 3 + [proj_tail] * 3
    in_specs = chunk_specs(IN_MAIN) + chunk_specs(IN_COLS - IN_MAIN)
    if has_vmix:
        args += [v_first] * 3
        in_specs += chunk_specs(D_RWKV)
    args += list(rwkv_params)
    in_specs += [row_spec(RWKV_COLS), row_spec(D_RWKV), full_spec((DECAY_RANK, D_RWKV)),
                 row_spec(D_RWKV), full_spec((A_RANK, D_RWKV)), row_spec(D_RWKV),
                 row_spec(D_RWKV), row_spec(D_RWKV), row_spec(D_RWKV), row_spec(D_RWKV)]
    out_sds = jax.ShapeDtypeStruct((bsz, seq, D_MODEL), F32)
    if has_vmix:
        args += list(vmix)
        in_specs += [row_spec(D_RWKV), full_spec((D_RWKV, VRES_RANK)),
                     full_spec((VRES_RANK, D_RWKV))]
        out_shape, out_specs = out_sds, step_spec(D_MODEL)
    else:
        out_shape = (out_sds, jax.ShapeDtypeStruct((bsz, seq, D_RWKV), F32))
        out_specs = (step_spec(D_MODEL), step_spec(D_RWKV))
    args += [lb_logits, g_norm_w]
    in_specs += [full_spec((DEPTH, D_HGRN)), row_spec(D_HGRN)]
    scratch = [pltpu.VMEM((RWKV_PAIRS, PAIR_ROWS, LANES), F32),
               pltpu.VMEM((1, RWKV_COLS), F32),
               pltpu.VMEM((HGRN_HEADS, HGRN_EXPAND, LANES), F32),
               pltpu.VMEM((RWKV_PAIRS, LANES, PAIR_ROWS), jnp.bfloat16),
               pltpu.VMEM((HGRN_HEADS, LANES, HGRN_EXPAND), jnp.bfloat16)]
    scratch += [pltpu.VMEM(shape, dtype) for _ in range(2) for _, shape, dtype in _BUF_SHAPES]
    res = pl.pallas_call(
        functools.partial(_mixer_kernel, layer),
        out_shape=out_shape,
        grid=(bsz, nc // 2),
        in_specs=in_specs,
        out_specs=out_specs,
        scratch_shapes=scratch,
        compiler_params=pltpu.CompilerParams(
            dimension_semantics=("arbitrary", "arbitrary"), vmem_limit_bytes=VMEM_LIMIT),
        name="mixer",
    )(*args)
    if has_vmix:
        return res, v_first
    return res


def kernel(x, w_in, shift_mu, w_decay0, w_decay_up, a0, a_up, k_k, k_a, r_k, ln_x_w, ln_x_b,
           v_mix0, v_mix_down, v_mix_up, lb_logits, g_norm_w, w_out, ln_w, ln_b):
    out_dtype = x.dtype
    bsz, seq, _ = x.shape
    h = x.astype(F32).reshape(bsz * seq, D_MODEL)
    lb_logits = lb_logits.astype(F32)
    row = lambda t: t.reshape(1, -1)
    v_first = None
    for l in range(DEPTH):
        proj = _in_proj(h, w_in, l, 0, IN_MAIN, IN_TN, IN_TM).reshape(bsz, seq, IN_MAIN)
        tail = _in_proj(h, w_in, l, IN_MAIN, IN_COLS - IN_MAIN, IN_COLS - IN_MAIN, 2 * IN_TM)
        tail = tail.reshape(bsz, seq, IN_COLS - IN_MAIN)
        params = (row(shift_mu[l]), row(w_decay0[l]), w_decay_up[l], row(a0[l]), a_up[l],
                  row(k_k[l]), row(k_a[l]), row(r_k[l]), row(ln_x_w[l]), row(ln_x_b[l]))
        vmix = None if l == 0 else (row(v_mix0[l - 1]), v_mix_down[l - 1], v_mix_up[l - 1])
        o_mix, v_first = _mixer_layer(proj, tail, l, params, vmix, v_first, lb_logits,
                                      row(g_norm_w[l]))
        h = _out_proj_ln(o_mix.reshape(bsz * seq, D_MODEL), h, w_out, l, row(ln_w[l]),
                         row(ln_b[l]))
    return h.reshape(bsz, seq, D_MODEL).astype(out_dtype)
```

```python
import functools
import math

import jax
import jax.numpy as jnp
from jax import lax
from jax.experimental import pallas as pl
from jax.experimental.pallas import tpu as pltpu

D_MODEL = 2048
DEPTH = 2
D_RWKV = D_MODEL // 2
D_HGRN = D_MODEL - D_RWKV
RWKV_HEAD = 64
DECAY_RANK = 64
A_RANK = 64
VRES_RANK = 32
HGRN_EXPAND = 128
HGRN_HEADS = D_HGRN // HGRN_EXPAND
RWKV_COLS = 4 * D_RWKV + DECAY_RANK + A_RANK
HGRN_COLS = 4 * D_HGRN
IN_COLS = RWKV_COLS + HGRN_COLS
ALPHA = (2 * DEPTH) ** 0.25
LN_EPS = 1e-5
GN_EPS = 64e-5
RMS_EPS = 1e-5
LB_FLOOR = 1e-30
LOG2E = math.log2(math.e)

LANES = 128
CHUNK = 64
PAIR_ROWS = 2 * CHUNK
RWKV_PAIRS = D_RWKV // LANES
VMEM_LIMIT = 56 * 1024 * 1024
MXU_WIDTH = 256
IN_TM = 1024
IN_TN = 5 * MXU_WIDTH
IN_MAIN = (IN_COLS // IN_TN) * IN_TN
OUT_TM, OUT_SLABS = 512, 2

F32 = jnp.float32


def _bf(x):
    return x.astype(jnp.bfloat16)


def _bdot(a, b):
    return jnp.dot(_bf(a), _bf(b), preferred_element_type=F32)


def _bdot_tn(a, b):
    return lax.dot_general(_bf(a), _bf(b), (((0,), (0,)), ((), ())), preferred_element_type=F32)


def _split_dot(m01, x, terms):
    pieces = []
    rem = x
    for t in range(terms):
        pieces.append(_bf(rem))
        if t + 1 < terms:
            rem = rem - pieces[-1].astype(F32)
    return jnp.dot(_bf(m01), jnp.concatenate(pieces, axis=0), preferred_element_type=F32)


def _sigmoid(x):
    return 0.5 * jnp.tanh(0.5 * x) + 0.5


def _head_sums(xs, scale=1.0):
    rows, cols = xs[0].shape
    width = len(xs) * LANES
    ones = ((_iota2((width, width), 0) // RWKV_HEAD) == (_iota2((width, width), 1) // RWKV_HEAD))
    ones = jnp.where(ones, scale, 0.0).astype(jnp.bfloat16)
    tiles = range(cols // LANES)
    lhs = jnp.concatenate(
        [jnp.concatenate([x[:, t * LANES:(t + 1) * LANES] for t in tiles], axis=0) for x in xs],
        axis=1)
    sums = jnp.dot(_bf(lhs), ones, preferred_element_type=F32)
    return [jnp.concatenate([sums[t * rows:(t + 1) * rows, i * LANES:(i + 1) * LANES]
                             for t in tiles], axis=1) for i in range(len(xs))]


def _head_means_split(x):
    rows, cols = x.shape
    hi = _bf(x)
    lo = _bf(x - hi.astype(F32))
    same = ((_iota2((2 * LANES, LANES), 0) % LANES) // RWKV_HEAD
            == _iota2((2 * LANES, LANES), 1) // RWKV_HEAD)
    ones = jnp.where(same, 1.0 / RWKV_HEAD, 0.0).astype(jnp.bfloat16)
    tiles = range(cols // LANES)
    lhs = jnp.concatenate(
        [jnp.concatenate([p[:, t * LANES:(t + 1) * LANES] for t in tiles], axis=0)
         for p in (hi, lo)], axis=1)
    sums = jnp.dot(lhs, ones, preferred_element_type=F32)
    return jnp.concatenate([sums[t * rows:(t + 1) * rows, :] for t in tiles], axis=1)


def _silu(x):
    return x * _sigmoid(x)


def _iota2(shape, dim):
    return lax.broadcasted_iota(jnp.int32, shape, dim)


def _mm_kernel(x_ref, w_ref, o_ref, wb_ref):
    @pl.when(pl.program_id(1) == 0)
    def _():
        wb_ref[...] = w_ref[0].astype(jnp.bfloat16)

    o_ref[...] = jnp.dot(x_ref[...].astype(jnp.bfloat16), wb_ref[...],
                         preferred_element_type=F32)


def _in_proj(x, w_in, layer, col0, n, tn, tm):
    m, k = x.shape
    return pl.pallas_call(
        _mm_kernel,
        out_shape=jax.ShapeDtypeStruct((m, n), F32),
        grid=(n // tn, m // tm),
        in_specs=[pl.BlockSpec((tm, k), lambda j, i: (i, 0)),
                  pl.BlockSpec((pl.Element(1), pl.Element(k), pl.Element(tn)),
                               lambda j, i: (layer, 0, pl.multiple_of(col0 + j * tn, LANES)))],
        out_specs=pl.BlockSpec((tm, tn), lambda j, i: (i, j)),
        scratch_shapes=[pltpu.VMEM((k, tn), jnp.bfloat16)],
        compiler_params=pltpu.CompilerParams(
            dimension_semantics=("arbitrary", "arbitrary"), vmem_limit_bytes=VMEM_LIMIT),
        name="in_proj",
    )(x, w_in)


def _out_kernel(mix_ref, h_ref, w_ref, lnw_ref, lnb_ref, o_ref, wb_ref):
    @pl.when(pl.program_id(0) == 0)
    def _():
        wb_ref[...] = w_ref[...].astype(jnp.bfloat16)

    slab = OUT_TM // OUT_SLABS
    for s in range(OUT_SLABS):
        rows = slice(s * slab, (s + 1) * slab)
        y = jnp.dot(mix_ref[rows, :].astype(jnp.bfloat16), wb_ref[...],
                    preferred_element_type=F32)
        u = ALPHA * h_ref[rows, :] + y
        mu = jnp.mean(u, axis=-1, keepdims=True)
        d = u - mu
        var = jnp.mean(d * d, axis=-1, keepdims=True)
        o_ref[rows, :] = d * lax.rsqrt(var + LN_EPS) * lnw_ref[...] + lnb_ref[...]


def _out_proj_ln(o_mix, h, w_out, layer, lnw, lnb):
    m = h.shape[0]
    tm = OUT_TM
    return pl.pallas_call(
        _out_kernel,
        out_shape=jax.ShapeDtypeStruct((m, D_MODEL), F32),
        grid=(m // tm,),
        in_specs=[pl.BlockSpec((tm, D_MODEL), lambda i: (i, 0)),
                  pl.BlockSpec((tm, D_MODEL), lambda i: (i, 0)),
                  pl.BlockSpec((None, D_MODEL, D_MODEL), lambda i: (layer, 0, 0),
                               pipeline_mode=pl.Buffered(1)),
                  pl.BlockSpec((1, D_MODEL), lambda i: (0, 0)),
                  pl.BlockSpec((1, D_MODEL), lambda i: (0, 0))],
        out_specs=pl.BlockSpec((tm, D_MODEL), lambda i: (i, 0)),
        scratch_shapes=[pltpu.VMEM((D_MODEL, D_MODEL), jnp.bfloat16)],
        compiler_params=pltpu.CompilerParams(
            dimension_semantics=("arbitrary",), vmem_limit_bytes=VMEM_LIMIT),
        name="out_proj_ln",
    )(o_mix, h, w_out, lnw, lnb)


_HGRN_LEVELS = tuple(CHUNK >> (i + 1) for i in range(int(math.log2(CHUNK))))
_N_LEVELS = len(_HGRN_LEVELS)
_PAIR_SL = [slice(p * LANES, (p + 1) * LANES) for p in range(RWKV_PAIRS)]
_HEAD_SL = [slice(h * LANES, (h + 1) * LANES) for h in range(HGRN_HEADS)]

_BUF_SHAPES = (
    ("ar", (RWKV_PAIRS, PAIR_ROWS, LANES), jnp.bfloat16),
    ("bkT", (RWKV_PAIRS, LANES, 2 * PAIR_ROWS), jnp.bfloat16),
    ("vs", (RWKV_PAIRS, PAIR_ROWS, LANES), jnp.bfloat16),
    ("uv", (RWKV_PAIRS, PAIR_ROWS, LANES), jnp.bfloat16),
    ("bhkh", (RWKV_PAIRS, PAIR_ROWS, LANES), jnp.bfloat16),
    ("gall_r", (1, D_RWKV), F32),
    ("bonus", (CHUNK, D_RWKV), F32),
    ("gate_r", (CHUNK, D_RWKV), F32),
    ("vnat", (CHUNK, D_RWKV), F32),
    ("qe", (_N_LEVELS, CHUNK, D_HGRN), jnp.bfloat16),
    ("keT", (_N_LEVELS, HGRN_HEADS // 2, 2 * LANES, PAIR_ROWS), jnp.bfloat16),
    ("qin", (CHUNK, D_HGRN), jnp.bfloat16),
    ("kdec", (CHUNK, D_HGRN), jnp.bfloat16),
    ("ib", (CHUNK, D_HGRN), jnp.bfloat16),
    ("odiag", (CHUNK, D_HGRN), F32),
    ("gate_h", (CHUNK, D_HGRN), F32),
    ("gall_h", (1, D_HGRN), F32),
)


def _head_mask():
    lane = _iota2((PAIR_ROWS, LANES), 1)
    row = _iota2((PAIR_ROWS, LANES), 0)
    return ((lane < RWKV_HEAD) == (row < CHUNK)).astype(F32)


def _rwkv_prepare(first, has_vmix, y_ref, vf_ref, prm, prev_ref, buf):
    (mu_ref, w0_ref, wup_ref, a0_ref, aup_ref, kk_ref, ka_ref, rk_ref, v0_ref, vdn_ref,
     vup_ref) = prm

    def shifted(c0, c1):
        y = y_ref[0, :, c0:c1]
        rolled = pltpu.roll(y, shift=1, axis=0)
        top = rolled[0:8, :]
        prev = jnp.zeros_like(y[0:1, :]) if first else prev_ref[:, c0:c1]
        top = jnp.where(_iota2(top.shape, 0) == 0, prev, top)
        y_prev = jnp.concatenate([top, rolled[8:, :]], axis=0)
        prev_ref[:, c0:c1] = y[CHUNK - 1:CHUNK, :]
        return y + mu_ref[:, c0:c1] * (y_prev - y)

    tail = shifted(4 * D_RWKV, RWKV_COLS)
    wd = tail[:, 0:DECAY_RANK]
    ad = tail[:, DECAY_RANK:]
    w_raw = w0_ref[...] + _bdot(jnp.tanh(wd), wup_ref[...])
    logw = (-math.exp(-0.5) * LOG2E) * _sigmoid(w_raw)
    yield
    a = _sigmoid(a0_ref[...] + _bdot(ad, aup_ref[...]))
    yield
    tri = (_iota2((CHUNK, 3 * CHUNK), 1) % CHUNK <= _iota2((CHUNK, 3 * CHUNK), 0)).astype(F32)
    cl = _split_dot(tri, logw, 3)
    cl_last = cl[CHUNK - 1:CHUNK, :]
    buf["gall_r"][...] = jnp.exp2(cl_last)
    yield
    buf["gate_r"][...] = _silu(shifted(3 * D_RWKV, 4 * D_RWKV))
    yield
    v = shifted(2 * D_RWKV, 3 * D_RWKV)
    if has_vmix:
        gate = _sigmoid(v0_ref[...] + _bdot(_bdot(v, vdn_ref[...]), vup_ref[...]))
        v = v + (vf_ref[0] - v) * gate
    else:
        buf["vnat"][...] = v
    v_b = _bf(v)
    yield
    r = shifted(0, D_RWKV)
    r_t = _bf(r * jnp.exp2(cl))
    yield
    k = shifted(D_RWKV, 2 * D_RWKV)
    kk = k * kk_ref[...]
    k = k * (1.0 + (a - 1.0) * ka_ref[...])
    yield
    kk_sq, rk_sum = _head_sums([kk * kk, r * k * rk_ref[...]])
    kk = kk * lax.rsqrt(jnp.maximum(kk_sq, 1e-24))
    buf["bonus"][...] = rk_sum * v
    yield
    b = kk * a
    a_t = _bf(kk * -jnp.exp2(cl - logw))
    yield
    g_inv = jnp.exp2(-cl)
    b_t = _bf(b * g_inv)
    k_t = _bf(k * g_inv)
    yield
    g_end = jnp.exp2(cl_last - cl)
    b_h = _bf(b * g_end)
    k_h = _bf(k * g_end)
    yield

    mask = _head_mask().astype(jnp.bfloat16)

    def stack(x, p):
        xp = x[:, _PAIR_SL[p]]
        return jnp.concatenate([xp, xp], axis=0) * mask

    for p in range(RWKV_PAIRS):
        sl = _PAIR_SL[p]
        buf["ar"][p] = jnp.concatenate([a_t[:, sl], r_t[:, sl]], axis=0)
        buf["bkT"][p] = jnp.concatenate([stack(b_t, p), stack(k_t, p)], axis=0).T
        buf["bhkh"][p] = jnp.concatenate([b_h[:, sl], k_h[:, sl]], axis=0)
        buf["vs"][p] = stack(v_b, p)
        buf["uv"][p, CHUNK:, :] = v_b[:, sl]
        yield


def _rwkv_chain(half, emit_v, buf, gnw_ref, gnb_ref, o_ref, vf_out_ref, s_ref, st_ref):
    mask_b = _head_mask().astype(jnp.bfloat16)
    t_idx = _iota2((CHUNK, LANES), 0)
    s_idx = _iota2((CHUNK, LANES), 1) % CHUNK
    strict = (s_idx < t_idx).astype(F32)
    strict_b = strict.astype(jnp.bfloat16)
    incl_b = (s_idx <= t_idx).astype(jnp.bfloat16)
    eye = (s_idx == t_idx).astype(F32)
    prow = _iota2((PAIR_ROWS, LANES), 0)
    pcol = _iota2((PAIR_ROWS, LANES), 1)
    same_head = (prow // RWKV_HEAD) == (pcol // RWKV_HEAD)
    rows = slice(half * CHUNK, (half + 1) * CHUNK)

    def diag(x):
        return jnp.concatenate([x, x], axis=0) * mask_b

    pairs = range(RWKV_PAIRS)
    ar = [buf["ar"][p] for p in pairs]
    sc = [_bdot(ar[p], buf["bkT"][p]) for p in pairs]
    pw = [strict * sc[p][:CHUNK, :LANES] for p in pairs]
    tinv = [eye + pw[p] for p in pairs]
    a_kv = [jnp.concatenate([_bf(sc[p][:CHUNK, LANES:]) * strict_b,
                             _bf(sc[p][CHUNK:, LANES:]) * incl_b], axis=0) for p in pairs]
    a_rb = [_bf(sc[p][CHUNK:, :LANES]) * incl_b for p in pairs]
    pwb = [_bf(x) for x in pw]
    pwd = [diag(x) for x in pwb]
    s_old = [s_ref[p] for p in pairs]
    ars = [_bdot(ar[p], st_ref[p]) for p in pairs]
    akv = [_bdot(a_kv[p], buf["vs"][p]) for p in pairs]
    yield
    for it in range(int(math.log2(CHUNK)) - 1):
        if it == 0:
            pw = [_bdot(pwb[p], pwd[p]) for p in pairs]
        else:
            both = [_bdot(jnp.concatenate([pwb[p], _bf(tinv[p])], axis=0), pwd[p]) for p in pairs]
            pw = [x[:CHUNK] for x in both]
            tinv = [tinv[p] + both[p][CHUNK:] for p in pairs]
        pwb = [_bf(x) for x in pw]
        pwd = [diag(x) for x in pwb]
        yield
    tinv = [tinv[p] + _bdot(tinv[p], pwd[p]) for p in pairs]
    yield

    u = [_bf(_bdot(tinv[p], diag(_bf(ars[p][:CHUNK] + akv[p][:CHUNK])))) for p in pairs]
    yield
    for p in pairs:
        buf["uv"][p, :CHUNK, :] = u[p]
        upd = _bdot_tn(buf["uv"][p], buf["bhkh"][p])
        s_new = jnp.where(same_head, s_old[p] * buf["gall_r"][:, _PAIR_SL[p]] + upd, 0.0)
        s_ref[p] = s_new
        st_ref[p] = _bf(s_new).T
    o = jnp.concatenate([ars[p][CHUNK:] + _bdot(a_rb[p], diag(u[p])) + akv[p][CHUNK:]
                         for p in pairs], axis=1)
    yield

    d = o - _head_means_split(o)
    yield
    var = _head_sums([d * d], 1.0 / RWKV_HEAD)[0]
    on = d * lax.rsqrt(var + GN_EPS) * gnw_ref[...] + gnb_ref[...]
    o_ref[0, rows, 0:D_RWKV] = (on + buf["bonus"][...]) * buf["gate_r"][...]
    if emit_v:
        vf_out_ref[0, rows, :] = buf["vnat"][...]
    yield


def _hgrn_prepare(layer, main_ref, tail_ref, lbl_ref, gw_ref, buf):
    lg = lbl_ref[...]
    e = jnp.exp(lg - jnp.max(lg, axis=0, keepdims=True))
    sm = e / jnp.sum(e, axis=0, keepdims=True)
    lb = jnp.sum(sm[0:layer + 1, :], axis=0, keepdims=True) - sm[0:1, :]

    def cols(c0, c1):
        lo, hi = RWKV_COLS + c0, RWKV_COLS + c1
        if hi <= IN_MAIN:
            return main_ref[0, :, lo:hi]
        return jnp.concatenate([main_ref[0, :, lo:IN_MAIN], tail_ref[0, :, 0:hi - IN_MAIN]],
                               axis=1)

    sig = _sigmoid(cols(D_HGRN, 2 * D_HGRN))
    log_f = jnp.log2(jnp.maximum(lb, LB_FLOOR) + (1.0 - lb) * sig)
    k = (1.0 - lb) * (1.0 - sig)
    yield

    trow = _iota2((CHUNK, 2 * CHUNK), 0)
    tcol = _iota2((CHUNK, 2 * CHUNK), 1) % CHUNK
    mats = [tcol <= trow]
    for m in _HGRN_LEVELS:
        mid = (trow // (2 * m)) * (2 * m) + m
        after = trow >= mid
        mats.append((after & (tcol >= mid) & (tcol <= trow))
                    | (~after & (tcol > trow) & (tcol < mid)))
    cums = _split_dot(jnp.concatenate(mats, axis=0).astype(F32), log_f, 2)
    b = cums[0:CHUNK, :]
    b_last = b[CHUNK - 1:CHUNK, :]
    buf["gall_h"][...] = jnp.exp2(b_last)
    yield
    q = _silu(cols(0, D_HGRN))
    buf["qin"][...] = _bf(q * jnp.exp2(b))
    q_b = _bf(q)
    yield
    buf["kdec"][...] = _bf(k * jnp.exp2(b_last - b))
    k_b = _bf(k)
    yield
    i_in = cols(2 * D_HGRN, 3 * D_HGRN)
    buf["ib"][...] = _bf(i_in)
    qk = q * k
    for h in range(HGRN_HEADS):
        sl = _HEAD_SL[h]
        buf["odiag"][:, sl] = jnp.sum(qk[:, sl], axis=-1, keepdims=True) * i_in[:, sl]
    yield
    for li in range(_N_LEVELS):
        e = _bf(jnp.exp2(cums[(li + 1) * CHUNK:(li + 2) * CHUNK, :]))
        buf["qe"][li] = q_b * e
        yield
        ke = k_b * e
        for hp in range(HGRN_HEADS // 2):
            buf["keT"][li, hp] = _diag2(ke[:, _HEAD_SL[2 * hp]], ke[:, _HEAD_SL[2 * hp + 1]]).T
        yield
    buf["gate_h"][...] = gw_ref[...] * _silu(cols(3 * D_HGRN, 4 * D_HGRN))
    yield


def _diag2(x0, x1):
    return jnp.concatenate([jnp.concatenate([x0, jnp.zeros_like(x1)], axis=1),
                            jnp.concatenate([jnp.zeros_like(x0), x1], axis=1)], axis=0)


def _hgrn_chain(half, buf, o_ref, s_ref, st_ref):
    trow = _iota2((CHUNK, LANES), 0)
    tcol = _iota2((CHUNK, LANES), 1) % CHUNK
    rows = slice(half * CHUNK, (half + 1) * CHUNK)
    heads = range(HGRN_HEADS)
    pairs = range(HGRN_HEADS // 2)
    pair_sl = [slice(2 * hp * LANES, 2 * (hp + 1) * LANES) for hp in pairs]

    att = [None] * len(pairs)
    yield
    for li, m in enumerate(_HGRN_LEVELS):
        mid = (trow // (2 * m)) * (2 * m) + m
        keep = ((trow >= mid) & (tcol < mid) & (tcol // (2 * m) == trow // (2 * m))).astype(F32)
        qe = buf["qe"][li]
        for hp in pairs:
            part = keep * _bdot(qe[:, pair_sl[hp]], buf["keT"][li, hp])
            att[hp] = part if att[hp] is None else att[hp] + part
        yield
    s_old = [s_ref[h] for h in heads]
    i_b = buf["ib"][...]
    q_in = buf["qin"][...]
    k_dec = buf["kdec"][...]
    o_2 = []
    for hp in pairs:
        h0, h1 = 2 * hp, 2 * hp + 1
        i2 = _diag2(i_b[:, _HEAD_SL[h0]], i_b[:, _HEAD_SL[h1]])
        inter = [_bdot(q_in[:, _HEAD_SL[h]], st_ref[h]) for h in (h0, h1)]
        o_2.append(_bdot(att[hp], i2) + jnp.concatenate(inter, axis=1))
    for h in heads:
        sl = _HEAD_SL[h]
        s_new = s_old[h] * buf["gall_h"][:, sl] + _bdot_tn(i_b[:, sl], k_dec[:, sl])
        s_ref[h] = s_new
        st_ref[h] = _bf(s_new).T
    yield
    for h in heads:
        sl = _HEAD_SL[h]
        o = o_2[h // 2][:, (h % 2) * LANES:(h % 2 + 1) * LANES] + buf["odiag"][:, sl]
        o = o * lax.rsqrt(jnp.mean(o * o, axis=-1, keepdims=True) + RMS_EPS)
        o_ref[0, rows, D_RWKV + h * LANES:D_RWKV + (h + 1) * LANES] = o * buf["gate_h"][:, sl]
    yield


_STAGE_ORDER = (0, 1, 0, 1, 0, 1, 0, 1, 0, 1, 0, 1, 0, 1, 0, 1, 0, 1, 0, 0)
_DONE = object()


def _mixer_kernel(layer, *refs):
    has_vmix = layer > 0
    n_in = 3 * (3 if has_vmix else 2)
    n_prm = 13 if has_vmix else 10
    proj_refs = list(zip(refs[0:3], refs[3:6]))
    vf_refs = refs[6:9] if has_vmix else (None, None, None)
    prm = list(refs[n_in:n_in + n_prm])
    (mu_ref, w0_ref, wup_ref, a0_ref, aup_ref, kk_ref, ka_ref, rk_ref, gnw_ref, gnb_ref) = prm[:10]
    vm = prm[10:13] if has_vmix else [None, None, None]
    prep_prm = (mu_ref, w0_ref, wup_ref, a0_ref, aup_ref, kk_ref, ka_ref, rk_ref, *vm)
    pos = n_in + n_prm
    lbl_ref, gw_ref, o_ref = refs[pos:pos + 3]
    pos += 3
    vf_out_ref = None
    if not has_vmix:
        vf_out_ref = refs[pos]
        pos += 1
    s_r_ref, prev_ref, s_h_ref, st_r_ref, st_h_ref = refs[pos:pos + 5]
    pos += 5
    nb = len(_BUF_SHAPES)
    bufs = [dict(zip([n for n, _, _ in _BUF_SHAPES], refs[pos + i * nb:pos + (i + 1) * nb]))
            for i in range(2)]

    def prepare(first, proj_ref, vf_ref, buf):
        main_ref, tail_ref = proj_ref
        return [_rwkv_prepare(first, has_vmix, main_ref, vf_ref, prep_prm, prev_ref, buf),
                _hgrn_prepare(layer, main_ref, tail_ref, lbl_ref, gw_ref, buf)]

    def chain(half, buf):
        gens = (_rwkv_chain(half, not has_vmix, buf, gnw_ref, gnb_ref, o_ref, vf_out_ref, s_r_ref,
                            st_r_ref),
                _hgrn_chain(half, buf, o_ref, s_h_ref, st_h_ref))
        for g in _STAGE_ORDER:
            next(gens[g])
            yield

    def run(chain_gen, prep_gens):
        live = [chain_gen] + list(prep_gens)
        while live:
            for g in list(live):
                if next(g, _DONE) is _DONE:
                    live.remove(g)

    @pl.when(pl.program_id(1) == 0)
    def _():
        s_r_ref[...] = jnp.zeros_like(s_r_ref)
        s_h_ref[...] = jnp.zeros_like(s_h_ref)
        st_r_ref[...] = jnp.zeros_like(st_r_ref)
        st_h_ref[...] = jnp.zeros_like(st_h_ref)
        run(iter(()), prepare(True, proj_refs[0], vf_refs[0], bufs[0]))

    run(chain(0, bufs[0]), prepare(False, proj_refs[1], vf_refs[1], bufs[1]))
    run(chain(1, bufs[1]), prepare(False, proj_refs[2], vf_refs[2], bufs[0]))


def _mixer_layer(proj, proj_tail, layer, rwkv_params, vmix, v_first, lb_logits, g_norm_w):
    bsz, seq, _ = proj.shape
    has_vmix = layer > 0
    nc = seq // CHUNK
    row_spec = lambda n: pl.BlockSpec((1, n), lambda b, j: (0, 0))
    full_spec = lambda s: pl.BlockSpec(s, lambda b, j: (0, 0))
    chunk_specs = lambda n: [
        pl.BlockSpec((1, CHUNK, n), lambda b, j: (b, 0, 0)),
        pl.BlockSpec((1, CHUNK, n), lambda b, j: (b, 2 * j + 1, 0)),
        pl.BlockSpec((1, CHUNK, n), lambda b, j: (b, jnp.minimum(2 * j + 2, nc - 1), 0))]
    step_spec = lambda n: pl.BlockSpec((1, 2 * CHUNK, n), lambda b, j: (b, j, 0))
    args = [proj] * 3 + [proj_tail] * 3
    in_specs = chunk_specs(IN_MAIN) + chunk_specs(IN_COLS - IN_MAIN)
    if has_vmix:
        args += [v_first] * 3
        in_specs += chunk_specs(D_RWKV)
    args += list(rwkv_params)
    in_specs += [row_spec(RWKV_COLS), row_spec(D_RWKV), full_spec((DECAY_RANK, D_RWKV)),
                 row_spec(D_RWKV), full_spec((A_RANK, D_RWKV)), row_spec(D_RWKV),
                 row_spec(D_RWKV), row_spec(D_RWKV), row_spec(D_RWKV), row_spec(D_RWKV)]
    out_sds = jax.ShapeDtypeStruct((bsz, seq, D_MODEL), F32)
    if has_vmix:
        args += list(vmix)
        in_specs += [row_spec(D_RWKV), full_spec((D_RWKV, VRES_RANK)),
                     full_spec((VRES_RANK, D_RWKV))]
        out_shape, out_specs = out_sds, step_spec(D_MODEL)
    else:
        out_shape = (out_sds, jax.ShapeDtypeStruct((bsz, seq, D_RWKV), F32))
        out_specs = (step_spec(D_MODEL), step_spec(D_RWKV))
    args += [lb_logits, g_norm_w]
    in_specs += [full_spec((DEPTH, D_HGRN)), row_spec(D_HGRN)]
    scratch = [pltpu.VMEM((RWKV_PAIRS, PAIR_ROWS, LANES), F32),
               pltpu.VMEM((1, RWKV_COLS), F32),
               pltpu.VMEM((HGRN_HEADS, HGRN_EXPAND, LANES), F32),
               pltpu.VMEM((RWKV_PAIRS, LANES, PAIR_ROWS), jnp.bfloat16),
               pltpu.VMEM((HGRN_HEADS, LANES, HGRN_EXPAND), jnp.bfloat16)]
    scratch += [pltpu.VMEM(shape, dtype) for _ in range(2) for _, shape, dtype in _BUF_SHAPES]
    res = pl.pallas_call(
        functools.partial(_mixer_kernel, layer),
        out_shape=out_shape,
        grid=(bsz, nc // 2),
        in_specs=in_specs,
        out_specs=out_specs,
        scratch_shapes=scratch,
        compiler_params=pltpu.CompilerParams(
            dimension_semantics=("arbitrary", "arbitrary"), vmem_limit_bytes=VMEM_LIMIT),
        name="mixer",
    )(*args)
    if has_vmix:
        return res, v_first
    return res


def kernel(x, w_in, shift_mu, w_decay0, w_decay_up, a0, a_up, k_k, k_a, r_k, ln_x_w, ln_x_b,
           v_mix0, v_mix_down, v_mix_up, lb_logits, g_norm_w, w_out, ln_w, ln_b):
    out_dtype = x.dtype
    bsz, seq, _ = x.shape
    h = x.astype(F32).reshape(bsz * seq, D_MODEL)
    lb_logits = lb_logits.astype(F32)
    row = lambda t: t.reshape(1, -1)
    v_first = None
    for l in range(DEPTH):
        proj = _in_proj(h, w_in, l, 0, IN_MAIN, IN_TN, IN_TM).reshape(bsz, seq, IN_MAIN)
        tail = _in_proj(h, w_in, l, IN_MAIN, IN_COLS - IN_MAIN, IN_COLS - IN_MAIN, IN_TM)
        tail = tail.reshape(bsz, seq, IN_COLS - IN_MAIN)
        params = (row(shift_mu[l]), row(w_decay0[l]), w_decay_up[l], row(a0[l]), a_up[l],
                  row(k_k[l]), row(k_a[l]), row(r_k[l]), row(ln_x_w[l]), row(ln_x_b[l]))
        vmix = None if l == 0 else (row(v_mix0[l - 1]), v_mix_down[l - 1], v_mix_up[l - 1])
        o_mix, v_first = _mixer_layer(proj, tail, l, params, vmix, v_first, lb_logits,
                                      row(g_norm_w[l]))
        h = _out_proj_ln(o_mix.reshape(bsz * seq, D_MODEL), h, w_out, l, row(ln_w[l]),
                         row(ln_b[l]))
    return h.reshape(bsz, seq, D_MODEL).astype(out_dtype)
```

```python
import functools
import math

import jax
import jax.numpy as jnp
from jax import lax
from jax.experimental import pallas as pl
from jax.experimental.pallas import tpu as pltpu

D_MODEL = 2048
DEPTH = 2
D_RWKV = D_MODEL // 2
D_HGRN = D_MODEL - D_RWKV
RWKV_HEAD = 64
DECAY_RANK = 64
A_RANK = 64
VRES_RANK = 32
HGRN_EXPAND = 128
HGRN_HEADS = D_HGRN // HGRN_EXPAND
RWKV_COLS = 4 * D_RWKV + DECAY_RANK + A_RANK
HGRN_COLS = 4 * D_HGRN
IN_COLS = RWKV_COLS + HGRN_COLS
ALPHA = (2 * DEPTH) ** 0.25
LN_EPS = 1e-5
GN_EPS = 64e-5
RMS_EPS = 1e-5
LB_FLOOR = 1e-30
LOG2E = math.log2(math.e)

LANES = 128
CHUNK = 64
PAIR_ROWS = 2 * CHUNK
RWKV_PAIRS = D_RWKV // LANES
VMEM_LIMIT = 56 * 1024 * 1024
MXU_WIDTH = 256
IN_TM = 512
IN_TN = 6 * MXU_WIDTH
IN_MAIN = (IN_COLS // IN_TN) * IN_TN
OUT_TM, OUT_SLABS = 512, 2

F32 = jnp.float32


def _bf(x):
    return x.astype(jnp.bfloat16)


def _bdot(a, b):
    return jnp.dot(_bf(a), _bf(b), preferred_element_type=F32)


def _bdot_nt(a, b):
    return lax.dot_general(_bf(a), _bf(b), (((1,), (1,)), ((), ())), preferred_element_type=F32)


def _bdot_tn(a, b):
    return lax.dot_general(_bf(a), _bf(b), (((0,), (0,)), ((), ())), preferred_element_type=F32)


def _split_dot(m01, x, terms):
    pieces = []
    rem = x
    for t in range(terms):
        pieces.append(_bf(rem))
        if t + 1 < terms:
            rem = rem - pieces[-1].astype(F32)
    return jnp.dot(_bf(m01), jnp.concatenate(pieces, axis=0), preferred_element_type=F32)


def _sigmoid(x):
    return 0.5 * jnp.tanh(0.5 * x) + 0.5


def _head_sums(xs, scale=1.0):
    rows, cols = xs[0].shape
    width = len(xs) * LANES
    ones = ((_iota2((width, width), 0) // RWKV_HEAD) == (_iota2((width, width), 1) // RWKV_HEAD))
    ones = jnp.where(ones, scale, 0.0).astype(jnp.bfloat16)
    tiles = range(cols // LANES)
    lhs = jnp.concatenate(
        [jnp.concatenate([x[:, t * LANES:(t + 1) * LANES] for t in tiles], axis=0) for x in xs],
        axis=1)
    sums = jnp.dot(_bf(lhs), ones, preferred_element_type=F32)
    return [jnp.concatenate([sums[t * rows:(t + 1) * rows, i * LANES:(i + 1) * LANES]
                             for t in tiles], axis=1) for i in range(len(xs))]


def _head_means_split(x):
    rows, cols = x.shape
    hi = _bf(x)
    lo = _bf(x - hi.astype(F32))
    same = ((_iota2((2 * LANES, LANES), 0) % LANES) // RWKV_HEAD
            == _iota2((2 * LANES, LANES), 1) // RWKV_HEAD)
    ones = jnp.where(same, 1.0 / RWKV_HEAD, 0.0).astype(jnp.bfloat16)
    tiles = range(cols // LANES)
    lhs = jnp.concatenate(
        [jnp.concatenate([p[:, t * LANES:(t + 1) * LANES] for t in tiles], axis=0)
         for p in (hi, lo)], axis=1)
    sums = jnp.dot(lhs, ones, preferred_element_type=F32)
    return jnp.concatenate([sums[t * rows:(t + 1) * rows, :] for t in tiles], axis=1)


def _silu(x):
    return x * _sigmoid(x)


def _iota2(shape, dim):
    return lax.broadcasted_iota(jnp.int32, shape, dim)


def _mm_kernel(x_ref, w_ref, o_ref, wb_ref):
    @pl.when(pl.program_id(1) == 0)
    def _():
        wb_ref[...] = w_ref[0].astype(jnp.bfloat16)

    o_ref[...] = jnp.dot(x_ref[...].astype(jnp.bfloat16), wb_ref[...],
                         preferred_element_type=F32)


def _in_proj(x, w_in, layer, col0, n, tn, tm):
    m, k = x.shape
    return pl.pallas_call(
        _mm_kernel,
        out_shape=jax.ShapeDtypeStruct((m, n), F32),
        grid=(n // tn, m // tm),
        in_specs=[pl.BlockSpec((tm, k), lambda j, i: (i, 0)),
                  pl.BlockSpec((pl.Element(1), pl.Element(k), pl.Element(tn)),
                               lambda j, i: (layer, 0, pl.multiple_of(col0 + j * tn, LANES)))],
        out_specs=pl.BlockSpec((tm, tn), lambda j, i: (i, j)),
        scratch_shapes=[pltpu.VMEM((k, tn), jnp.bfloat16)],
        compiler_params=pltpu.CompilerParams(
            dimension_semantics=("arbitrary", "arbitrary"), vmem_limit_bytes=VMEM_LIMIT),
        name="in_proj",
    )(x, w_in)


def _out_kernel(mix_ref, h_ref, w_ref, lnw_ref, lnb_ref, o_ref, wb_ref):
    @pl.when(pl.program_id(0) == 0)
    def _():
        wb_ref[...] = w_ref[...].astype(jnp.bfloat16)

    slab = OUT_TM // OUT_SLABS
    for s in range(OUT_SLABS):
        rows = slice(s * slab, (s + 1) * slab)
        y = jnp.dot(mix_ref[rows, :].astype(jnp.bfloat16), wb_ref[...],
                    preferred_element_type=F32)
        u = ALPHA * h_ref[rows, :] + y
        mu = jnp.mean(u, axis=-1, keepdims=True)
        d = u - mu
        var = jnp.mean(d * d, axis=-1, keepdims=True)
        o_ref[rows, :] = d * lax.rsqrt(var + LN_EPS) * lnw_ref[...] + lnb_ref[...]


def _out_proj_ln(o_mix, h, w_out, layer, lnw, lnb):
    m = h.shape[0]
    tm = OUT_TM
    return pl.pallas_call(
        _out_kernel,
        out_shape=jax.ShapeDtypeStruct((m, D_MODEL), F32),
        grid=(m // tm,),
        in_specs=[pl.BlockSpec((tm, D_MODEL), lambda i: (i, 0)),
                  pl.BlockSpec((tm, D_MODEL), lambda i: (i, 0)),
                  pl.BlockSpec((None, D_MODEL, D_MODEL), lambda i: (layer, 0, 0),
                               pipeline_mode=pl.Buffered(1)),
                  pl.BlockSpec((1, D_MODEL), lambda i: (0, 0)),
                  pl.BlockSpec((1, D_MODEL), lambda i: (0, 0))],
        out_specs=pl.BlockSpec((tm, D_MODEL), lambda i: (i, 0)),
        scratch_shapes=[pltpu.VMEM((D_MODEL, D_MODEL), jnp.bfloat16)],
        compiler_params=pltpu.CompilerParams(
            dimension_semantics=("arbitrary",), vmem_limit_bytes=VMEM_LIMIT),
        name="out_proj_ln",
    )(o_mix, h, w_out, lnw, lnb)


_INV_BASE = 4
_HGRN_LEVELS = tuple(CHUNK >> (i + 1) for i in range(int(math.log2(CHUNK))))
_N_LEVELS = len(_HGRN_LEVELS)
_PAIR_SL = [slice(p * LANES, (p + 1) * LANES) for p in range(RWKV_PAIRS)]
_HEAD_SL = [slice(h * LANES, (h + 1) * LANES) for h in range(HGRN_HEADS)]

_BUF_SHAPES = (
    ("ar", (RWKV_PAIRS, PAIR_ROWS, LANES), jnp.bfloat16),
    ("bkT", (RWKV_PAIRS, LANES, 2 * PAIR_ROWS), jnp.bfloat16),
    ("vs", (RWKV_PAIRS, PAIR_ROWS, LANES), jnp.bfloat16),
    ("uv", (RWKV_PAIRS, PAIR_ROWS, LANES), jnp.bfloat16),
    ("bhkh", (RWKV_PAIRS, PAIR_ROWS, LANES), jnp.bfloat16),
    ("gall_r", (1, D_RWKV), F32),
    ("bonus", (CHUNK, D_RWKV), F32),
    ("gate_r", (CHUNK, D_RWKV), F32),
    ("vnat", (CHUNK, D_RWKV), F32),
    ("qe", (_N_LEVELS, CHUNK, D_HGRN), jnp.bfloat16),
    ("keT", (_N_LEVELS, HGRN_HEADS // 2, 2 * LANES, PAIR_ROWS), jnp.bfloat16),
    ("qin", (CHUNK, D_HGRN), jnp.bfloat16),
    ("kdec", (CHUNK, D_HGRN), jnp.bfloat16),
    ("ib", (CHUNK, D_HGRN), jnp.bfloat16),
    ("odiag", (CHUNK, D_HGRN), F32),
    ("gate_h", (CHUNK, D_HGRN), F32),
    ("gall_h", (1, D_HGRN), F32),
)


def _head_mask():
    lane = _iota2((PAIR_ROWS, LANES), 1)
    row = _iota2((PAIR_ROWS, LANES), 0)
    return ((lane < RWKV_HEAD) == (row < CHUNK)).astype(F32)


def _rwkv_prepare(first, has_vmix, y_ref, vf_ref, prm, prev_ref, buf):
    (mu_ref, w0_ref, wup_ref, a0_ref, aup_ref, kk_ref, ka_ref, rk_ref, v0_ref, vdn_ref,
     vup_ref) = prm

    def shifted(c0, c1):
        y = y_ref[0, :, c0:c1]
        rolled = pltpu.roll(y, shift=1, axis=0)
        top = rolled[0:8, :]
        prev = jnp.zeros_like(y[0:1, :]) if first else prev_ref[:, c0:c1]
        top = jnp.where(_iota2(top.shape, 0) == 0, prev, top)
        y_prev = jnp.concatenate([top, rolled[8:, :]], axis=0)
        prev_ref[:, c0:c1] = y[CHUNK - 1:CHUNK, :]
        return y + mu_ref[:, c0:c1] * (y_prev - y)

    tail = shifted(4 * D_RWKV, RWKV_COLS)
    wd = tail[:, 0:DECAY_RANK]
    ad = tail[:, DECAY_RANK:]
    w_raw = w0_ref[...] + _bdot(jnp.tanh(wd), wup_ref[...])
    logw = (-math.exp(-0.5) * LOG2E) * _sigmoid(w_raw)
    yield
    a = _sigmoid(a0_ref[...] + _bdot(ad, aup_ref[...]))
    yield
    tri = (_iota2((CHUNK, 3 * CHUNK), 1) % CHUNK <= _iota2((CHUNK, 3 * CHUNK), 0)).astype(F32)
    cl = _split_dot(tri, logw, 3)
    cl_last = cl[CHUNK - 1:CHUNK, :]
    buf["gall_r"][...] = jnp.exp2(cl_last)
    yield
    buf["gate_r"][...] = _silu(shifted(3 * D_RWKV, 4 * D_RWKV))
    yield
    v = shifted(2 * D_RWKV, 3 * D_RWKV)
    if has_vmix:
        gate = _sigmoid(v0_ref[...] + _bdot(_bdot(v, vdn_ref[...]), vup_ref[...]))
        v = v + (vf_ref[0] - v) * gate
    else:
        buf["vnat"][...] = v
    v_b = _bf(v)
    yield
    r = shifted(0, D_RWKV)
    r_t = _bf(r * jnp.exp2(cl))
    yield
    k = shifted(D_RWKV, 2 * D_RWKV)
    kk = k * kk_ref[...]
    k = k * (1.0 + (a - 1.0) * ka_ref[...])
    yield
    kk_sq, rk_sum = _head_sums([kk * kk, r * k * rk_ref[...]])
    kk = kk * lax.rsqrt(jnp.maximum(kk_sq, 1e-24))
    buf["bonus"][...] = rk_sum * v
    yield
    b = kk * a
    a_t = _bf(kk * -jnp.exp2(cl - logw))
    yield
    g_inv = jnp.exp2(-cl)
    b_t = _bf(b * g_inv)
    k_t = _bf(k * g_inv)
    yield
    g_end = jnp.exp2(cl_last - cl)
    b_h = _bf(b * g_end)
    k_h = _bf(k * g_end)
    yield

    mask = _head_mask().astype(jnp.bfloat16)

    def stack(x, p):
        xp = x[:, _PAIR_SL[p]]
        return jnp.concatenate([xp, xp], axis=0) * mask

    for p in range(RWKV_PAIRS):
        sl = _PAIR_SL[p]
        buf["ar"][p] = jnp.concatenate([a_t[:, sl], r_t[:, sl]], axis=0)
        buf["bkT"][p] = jnp.concatenate([stack(b_t, p), stack(k_t, p)], axis=0).T
        buf["bhkh"][p] = jnp.concatenate([b_h[:, sl], k_h[:, sl]], axis=0)
        buf["vs"][p] = stack(v_b, p)
        buf["uv"][p, CHUNK:, :] = v_b[:, sl]
        yield


def _rwkv_chain(half, emit_v, buf, gnw_ref, gnb_ref, o_ref, vf_out_ref, s_ref, st_ref):
    mask_b = _head_mask().astype(jnp.bfloat16)
    t_idx = _iota2((CHUNK, LANES), 0)
    s_idx = _iota2((CHUNK, LANES), 1) % CHUNK
    strict_b = (s_idx < t_idx).astype(jnp.bfloat16)
    incl_b = (s_idx <= t_idx).astype(jnp.bfloat16)
    eye = (s_idx == t_idx).astype(F32)
    prow = _iota2((PAIR_ROWS, LANES), 0)
    pcol = _iota2((PAIR_ROWS, LANES), 1)
    same_head = (prow // RWKV_HEAD) == (pcol // RWKV_HEAD)
    rows = slice(half * CHUNK, (half + 1) * CHUNK)

    def diag(x):
        return jnp.concatenate([x, x], axis=0) * mask_b

    pairs = range(RWKV_PAIRS)
    ar = [buf["ar"][p] for p in pairs]
    sc = [_bdot(ar[p], buf["bkT"][p]) for p in pairs]
    a_kv = [jnp.concatenate([_bf(sc[p][:CHUNK, LANES:]) * strict_b,
                             _bf(sc[p][CHUNK:, LANES:]) * incl_b], axis=0) for p in pairs]
    a_rb = [_bf(sc[p][CHUNK:, :LANES]) * incl_b for p in pairs]
    s_old = [s_ref[p] for p in pairs]
    ars = [_bdot(ar[p], st_ref[p]) for p in pairs]
    akv = [_bdot(a_kv[p], buf["vs"][p]) for p in pairs]

    def part(p, m):
        return jnp.where(m, sc[p][:CHUNK, :LANES], 0.0)

    lower = s_idx < t_idx
    blk = lambda b: (t_idx // b) == (s_idx // b)
    base = lower & blk(_INV_BASE)
    d0 = [part(p, base) for p in pairs]
    d0w = [diag(_bf(x)) for x in d0]
    yield
    sq = [_bdot(d0[p], d0w[p]) for p in pairs]
    yield
    tinv = [eye + d0[p] for p in pairs]
    tinv = [tinv[p] + _bdot(tinv[p], diag(_bf(sq[p]))) for p in pairs]
    yield
    b = _INV_BASE
    while b < CHUNK:
        off = lower & blk(2 * b) & ~blk(b)
        tb = [_bf(x) for x in tinv]
        x1 = [_bdot(tb[p], diag(_bf(part(p, off)))) for p in pairs]
        yield
        tinv = [tinv[p] + _bdot(x1[p], diag(tb[p])) for p in pairs]
        yield
        b *= 2

    u = [_bf(_bdot(tinv[p], diag(_bf(ars[p][:CHUNK] + akv[p][:CHUNK])))) for p in pairs]
    yield
    for p in pairs:
        buf["uv"][p, :CHUNK, :] = u[p]
        upd = _bdot_tn(buf["uv"][p], buf["bhkh"][p])
        s_new = jnp.where(same_head, s_old[p] * buf["gall_r"][:, _PAIR_SL[p]] + upd, 0.0)
        s_ref[p] = s_new
        st_ref[p] = _bf(s_new).T
    o = jnp.concatenate([ars[p][CHUNK:] + _bdot(a_rb[p], diag(u[p])) + akv[p][CHUNK:]
                         for p in pairs], axis=1)
    yield

    d = o - _head_means_split(o)
    yield
    var = _head_sums([d * d], 1.0 / RWKV_HEAD)[0]
    on = d * lax.rsqrt(var + GN_EPS) * gnw_ref[...] + gnb_ref[...]
    o_ref[0, rows, 0:D_RWKV] = (on + buf["bonus"][...]) * buf["gate_r"][...]
    if emit_v:
        vf_out_ref[0, rows, :] = buf["vnat"][...]
    yield


def _hgrn_prepare(layer, main_ref, tail_ref, lbl_ref, gw_ref, buf):
    lg = lbl_ref[...]
    e = jnp.exp(lg - jnp.max(lg, axis=0, keepdims=True))
    sm = e / jnp.sum(e, axis=0, keepdims=True)
    lb = jnp.sum(sm[0:layer + 1, :], axis=0, keepdims=True) - sm[0:1, :]

    def cols(c0, c1):
        lo, hi = RWKV_COLS + c0, RWKV_COLS + c1
        if hi <= IN_MAIN:
            return main_ref[0, :, lo:hi]
        return jnp.concatenate([main_ref[0, :, lo:IN_MAIN], tail_ref[0, :, 0:hi - IN_MAIN]],
                               axis=1)

    sig = _sigmoid(cols(D_HGRN, 2 * D_HGRN))
    log_f = jnp.log2(jnp.maximum(lb, LB_FLOOR) + (1.0 - lb) * sig)
    k = (1.0 - lb) * (1.0 - sig)
    yield

    trow = _iota2((CHUNK, 2 * CHUNK), 0)
    tcol = _iota2((CHUNK, 2 * CHUNK), 1) % CHUNK
    mats = [tcol <= trow]
    for m in _HGRN_LEVELS:
        mid = (trow // (2 * m)) * (2 * m) + m
        after = trow >= mid
        mats.append((after & (tcol >= mid) & (tcol <= trow))
                    | (~after & (tcol > trow) & (tcol < mid)))
    cums = _split_dot(jnp.concatenate(mats, axis=0).astype(F32), log_f, 2)
    b = cums[0:CHUNK, :]
    b_last = b[CHUNK - 1:CHUNK, :]
    buf["gall_h"][...] = jnp.exp2(b_last)
    yield
    q = _silu(cols(0, D_HGRN))
    buf["qin"][...] = _bf(q * jnp.exp2(b))
    q_b = _bf(q)
    yield
    buf["kdec"][...] = _bf(k * jnp.exp2(b_last - b))
    k_b = _bf(k)
    yield
    i_in = cols(2 * D_HGRN, 3 * D_HGRN)
    buf["ib"][...] = _bf(i_in)
    qk = q * k
    for h in range(HGRN_HEADS):
        sl = _HEAD_SL[h]
        buf["odiag"][:, sl] = jnp.sum(qk[:, sl], axis=-1, keepdims=True) * i_in[:, sl]
    yield
    for li in range(_N_LEVELS):
        e = _bf(jnp.exp2(cums[(li + 1) * CHUNK:(li + 2) * CHUNK, :]))
        buf["qe"][li] = q_b * e
        yield
        ke = k_b * e
        for hp in range(HGRN_HEADS // 2):
            buf["keT"][li, hp] = _diag2(ke[:, _HEAD_SL[2 * hp]], ke[:, _HEAD_SL[2 * hp + 1]]).T
        yield
    buf["gate_h"][...] = gw_ref[...] * _silu(cols(3 * D_HGRN, 4 * D_HGRN))
    yield


def _diag2(x0, x1):
    return jnp.concatenate([jnp.concatenate([x0, jnp.zeros_like(x1)], axis=1),
                            jnp.concatenate([jnp.zeros_like(x0), x1], axis=1)], axis=0)


def _hgrn_chain(half, buf, o_ref, s_ref, st_ref):
    trow = _iota2((CHUNK, LANES), 0)
    tcol = _iota2((CHUNK, LANES), 1) % CHUNK
    rows = slice(half * CHUNK, (half + 1) * CHUNK)
    heads = range(HGRN_HEADS)
    pairs = range(HGRN_HEADS // 2)
    pair_sl = [slice(2 * hp * LANES, 2 * (hp + 1) * LANES) for hp in pairs]

    att = [None] * len(pairs)
    yield
    for li, m in enumerate(_HGRN_LEVELS):
        mid = (trow // (2 * m)) * (2 * m) + m
        keep = ((trow >= mid) & (tcol < mid) & (tcol // (2 * m) == trow // (2 * m))).astype(F32)
        qe = buf["qe"][li]
        for hp in pairs:
            part = keep * _bdot(qe[:, pair_sl[hp]], buf["keT"][li, hp])
            att[hp] = part if att[hp] is None else att[hp] + part
        yield
    s_old = [s_ref[h] for h in heads]
    i_b = buf["ib"][...]
    q_in = buf["qin"][...]
    k_dec = buf["kdec"][...]
    o_2 = []
    for hp in pairs:
        h0, h1 = 2 * hp, 2 * hp + 1
        i2 = _diag2(i_b[:, _HEAD_SL[h0]], i_b[:, _HEAD_SL[h1]])
        inter = [_bdot(q_in[:, _HEAD_SL[h]], st_ref[h]) for h in (h0, h1)]
        o_2.append(_bdot(att[hp], i2) + jnp.concatenate(inter, axis=1))
    for h in heads:
        sl = _HEAD_SL[h]
        s_new = s_old[h] * buf["gall_h"][:, sl] + _bdot_tn(i_b[:, sl], k_dec[:, sl])
        s_ref[h] = s_new
        st_ref[h] = _bf(s_new).T
    yield
    for h in heads:
        sl = _HEAD_SL[h]
        o = o_2[h // 2][:, (h % 2) * LANES:(h % 2 + 1) * LANES] + buf["odiag"][:, sl]
        o = o * lax.rsqrt(jnp.mean(o * o, axis=-1, keepdims=True) + RMS_EPS)
        o_ref[0, rows, D_RWKV + h * LANES:D_RWKV + (h + 1) * LANES] = o * buf["gate_h"][:, sl]
    yield


_STAGE_ORDER = (0, 1, 0, 0, 1, 0, 0, 1, 0, 0, 1, 0, 0, 1, 0, 0, 1, 0, 1, 0, 1, 0, 1, 0)
_DONE = object()


def _mixer_kernel(layer, *refs):
    has_vmix = layer > 0
    n_in = 3 * (3 if has_vmix else 2)
    n_prm = 13 if has_vmix else 10
    proj_refs = list(zip(refs[0:3], refs[3:6]))
    vf_refs = refs[6:9] if has_vmix else (None, None, None)
    prm = list(refs[n_in:n_in + n_prm])
    (mu_ref, w0_ref, wup_ref, a0_ref, aup_ref, kk_ref, ka_ref, rk_ref, gnw_ref, gnb_ref) = prm[:10]
    vm = prm[10:13] if has_vmix else [None, None, None]
    prep_prm = (mu_ref, w0_ref, wup_ref, a0_ref, aup_ref, kk_ref, ka_ref, rk_ref, *vm)
    pos = n_in + n_prm
    lbl_ref, gw_ref, o_ref = refs[pos:pos + 3]
    pos += 3
    vf_out_ref = None
    if not has_vmix:
        vf_out_ref = refs[pos]
        pos += 1
    s_r_ref, prev_ref, s_h_ref, st_r_ref, st_h_ref = refs[pos:pos + 5]
    pos += 5
    nb = len(_BUF_SHAPES)
    bufs = [dict(zip([n for n, _, _ in _BUF_SHAPES], refs[pos + i * nb:pos + (i + 1) * nb]))
            for i in range(2)]

    def prepare(first, proj_ref, vf_ref, buf):
        main_ref, tail_ref = proj_ref
        return [_rwkv_prepare(first, has_vmix, main_ref, vf_ref, prep_prm, prev_ref, buf),
                _hgrn_prepare(layer, main_ref, tail_ref, lbl_ref, gw_ref, buf)]

    def chain(half, buf):
        gens = (_rwkv_chain(half, not has_vmix, buf, gnw_ref, gnb_ref, o_ref, vf_out_ref, s_r_ref,
                            st_r_ref),
                _hgrn_chain(half, buf, o_ref, s_h_ref, st_h_ref))
        for g in _STAGE_ORDER:
            next(gens[g])
            yield

    def run(chain_gen, prep_gens):
        live = [chain_gen] + list(prep_gens)
        while live:
            for g in list(live):
                if next(g, _DONE) is _DONE:
                    live.remove(g)

    @pl.when(pl.program_id(1) == 0)
    def _():
        s_r_ref[...] = jnp.zeros_like(s_r_ref)
        s_h_ref[...] = jnp.zeros_like(s_h_ref)
        st_r_ref[...] = jnp.zeros_like(st_r_ref)
        st_h_ref[...] = jnp.zeros_like(st_h_ref)
        run(iter(()), prepare(True, proj_refs[0], vf_refs[0], bufs[0]))

    run(chain(0, bufs[0]), prepare(False, proj_refs[1], vf_refs[1], bufs[1]))
    run(chain(1, bufs[1]), prepare(False, proj_refs[2], vf_refs[2], bufs[0]))


def _mixer_layer(proj, proj_tail, layer, rwkv_params, vmix, v_first, lb_logits, g_norm_w):
    bsz, seq, _ = proj.shape
    has_vmix = layer > 0
    nc = seq // CHUNK
    row_spec = lambda n: pl.BlockSpec((1, n), lambda b, j: (0, 0))
    full_spec = lambda s: pl.BlockSpec(s, lambda b, j: (0, 0))
    chunk_specs = lambda n: [
        pl.BlockSpec((1, CHUNK, n), lambda b, j: (b, 0, 0)),
        pl.BlockSpec((1, CHUNK, n), lambda b, j: (b, 2 * j + 1, 0)),
        pl.BlockSpec((1, CHUNK, n), lambda b, j: (b, jnp.minimum(2 * j + 2, nc - 1), 0))]
    step_spec = lambda n: pl.BlockSpec((1, 2 * CHUNK, n), lambda b, j: (b, j, 0))
    args = [proj] * 3 + [proj_tail] * 3
    in_specs = chunk_specs(IN_MAIN) + chunk_specs(IN_COLS - IN_MAIN)
    if has_vmix:
        args += [v_first] * 3
        in_specs += chunk_specs(D_RWKV)
    args += list(rwkv_params)
    in_specs += [row_spec(RWKV_COLS), row_spec(D_RWKV), full_spec((DECAY_RANK, D_RWKV)),
                 row_spec(D_RWKV), full_spec((A_RANK, D_RWKV)), row_spec(D_RWKV),
                 row_spec(D_RWKV), row_spec(D_RWKV), row_spec(D_RWKV), row_spec(D_RWKV)]
    out_sds = jax.ShapeDtypeStruct((bsz, seq, D_MODEL), F32)
    if has_vmix:
        args += list(vmix)
        in_specs += [row_spec(D_RWKV), full_spec((D_RWKV, VRES_RANK)),
                     full_spec((VRES_RANK, D_RWKV))]
        out_shape, out_specs = out_sds, step_spec(D_MODEL)
    else:
        out_shape = (out_sds, jax.ShapeDtypeStruct((bsz, seq, D_RWKV), F32))
        out_specs = (step_spec(D_MODEL), step_spec(D_RWKV))
    args += [lb_logits, g_norm_w]
    in_specs += [full_spec((DEPTH, D_HGRN)), row_spec(D_HGRN)]
    scratch = [pltpu.VMEM((RWKV_PAIRS, PAIR_ROWS, LANES), F32),
               pltpu.VMEM((1, RWKV_COLS), F32),
               pltpu.VMEM((HGRN_HEADS, HGRN_EXPAND, LANES), F32),
               pltpu.VMEM((RWKV_PAIRS, LANES, PAIR_ROWS), jnp.bfloat16),
               pltpu.VMEM((HGRN_HEADS, LANES, HGRN_EXPAND), jnp.bfloat16)]
    scratch += [pltpu.VMEM(shape, dtype) for _ in range(2) for _, shape, dtype in _BUF_SHAPES]
    res = pl.pallas_call(
        functools.partial(_mixer_kernel, layer),
        out_shape=out_shape,
        grid=(bsz, nc // 2),
        in_specs=in_specs,
        out_specs=out_specs,
        scratch_shapes=scratch,
        compiler_params=pltpu.CompilerParams(
            dimension_semantics=("arbitrary", "arbitrary"), vmem_limit_bytes=VMEM_LIMIT),
        name="mixer",
    )(*args)
    if has_vmix:
        return res, v_first
    return res


def kernel(x, w_in, shift_mu, w_decay0, w_decay_up, a0, a_up, k_k, k_a, r_k, ln_x_w, ln_x_b,
           v_mix0, v_mix_down, v_mix_up, lb_logits, g_norm_w, w_out, ln_w, ln_b):
    out_dtype = x.dtype
    bsz, seq, _ = x.shape
    h = x.astype(F32).reshape(bsz * seq, D_MODEL)
    lb_logits = lb_logits.astype(F32)
    row = lambda t: t.reshape(1, -1)
    v_first = None
    for l in range(DEPTH):
        proj = _in_proj(h, w_in, l, 0, IN_MAIN, IN_TN, IN_TM).reshape(bsz, seq, IN_MAIN)
        tail = _in_proj(h, w_in, l, IN_MAIN, IN_COLS - IN_MAIN, IN_COLS - IN_MAIN, 2 * IN_TM)
        tail = tail.reshape(bsz, seq, IN_COLS - IN_MAIN)
        params = (row(shift_mu[l]), row(w_decay0[l]), w_decay_up[l], row(a0[l]), a_up[l],
                  row(k_k[l]), row(k_a[l]), row(r_k[l]), row(ln_x_w[l]), row(ln_x_b[l]))
        vmix = None if l == 0 else (row(v_mix0[l - 1]), v_mix_down[l - 1], v_mix_up[l - 1])
        o_mix, v_first = _mixer_layer(proj, tail, l, params, vmix, v_first, lb_logits,
                                      row(g_norm_w[l]))
        h = _out_proj_ln(o_mix.reshape(bsz * seq, D_MODEL), h, w_out, l, row(ln_w[l]),
                         row(ln_b[l]))
    return h.reshape(bsz, seq, D_MODEL).astype(out_dtype)
```

```python
import functools
import math

import jax
import jax.numpy as jnp
from jax import lax
from jax.experimental import pallas as pl
from jax.experimental.pallas import tpu as pltpu

D_MODEL = 2048
DEPTH = 2
D_RWKV = D_MODEL // 2
D_HGRN = D_MODEL - D_RWKV
RWKV_HEAD = 64
DECAY_RANK = 64
A_RANK = 64
VRES_RANK = 32
HGRN_EXPAND = 128
HGRN_HEADS = D_HGRN // HGRN_EXPAND
RWKV_COLS = 4 * D_RWKV + DECAY_RANK + A_RANK
HGRN_COLS = 4 * D_HGRN
IN_COLS = RWKV_COLS + HGRN_COLS
ALPHA = (2 * DEPTH) ** 0.25
LN_EPS = 1e-5
GN_EPS = 64e-5
RMS_EPS = 1e-5
LB_FLOOR = 1e-30
LOG2E = math.log2(math.e)

LANES = 128
CHUNK = 64
PAIR_ROWS = 2 * CHUNK
RWKV_PAIRS = D_RWKV // LANES
VMEM_LIMIT = 56 * 1024 * 1024
MXU_WIDTH = 256
IN_TM = 512
IN_TN = 6 * MXU_WIDTH
IN_MAIN = (IN_COLS // IN_TN) * IN_TN
OUT_TM, OUT_SLABS = 512, 2

F32 = jnp.float32


def _bf(x):
    return x.astype(jnp.bfloat16)


def _bdot(a, b):
    return jnp.dot(_bf(a), _bf(b), preferred_element_type=F32)


def _bdot_nt(a, b):
    return lax.dot_general(_bf(a), _bf(b), (((1,), (1,)), ((), ())), preferred_element_type=F32)


def _bdot_tn(a, b):
    return lax.dot_general(_bf(a), _bf(b), (((0,), (0,)), ((), ())), preferred_element_type=F32)


def _split_dot(m01, x, terms):
    pieces = []
    rem = x
    for t in range(terms):
        pieces.append(_bf(rem))
        if t + 1 < terms:
            rem = rem - pieces[-1].astype(F32)
    return jnp.dot(_bf(m01), jnp.concatenate(pieces, axis=0), preferred_element_type=F32)


def _sigmoid(x):
    return 0.5 * jnp.tanh(0.5 * x) + 0.5


def _head_sums(xs, scale=1.0):
    rows, cols = xs[0].shape
    width = len(xs) * LANES
    ones = ((_iota2((width, width), 0) // RWKV_HEAD) == (_iota2((width, width), 1) // RWKV_HEAD))
    ones = jnp.where(ones, scale, 0.0).astype(jnp.bfloat16)
    tiles = range(cols // LANES)
    lhs = jnp.concatenate(
        [jnp.concatenate([x[:, t * LANES:(t + 1) * LANES] for t in tiles], axis=0) for x in xs],
        axis=1)
    sums = jnp.dot(_bf(lhs), ones, preferred_element_type=F32)
    return [jnp.concatenate([sums[t * rows:(t + 1) * rows, i * LANES:(i + 1) * LANES]
                             for t in tiles], axis=1) for i in range(len(xs))]


def _head_means_split(x):
    rows, cols = x.shape
    hi = _bf(x)
    lo = _bf(x - hi.astype(F32))
    same = ((_iota2((2 * LANES, LANES), 0) % LANES) // RWKV_HEAD
            == _iota2((2 * LANES, LANES), 1) // RWKV_HEAD)
    ones = jnp.where(same, 1.0 / RWKV_HEAD, 0.0).astype(jnp.bfloat16)
    tiles = range(cols // LANES)
    lhs = jnp.concatenate(
        [jnp.concatenate([p[:, t * LANES:(t + 1) * LANES] for t in tiles], axis=0)
         for p in (hi, lo)], axis=1)
    sums = jnp.dot(lhs, ones, preferred_element_type=F32)
    return jnp.concatenate([sums[t * rows:(t + 1) * rows, :] for t in tiles], axis=1)


def _silu(x):
    return x * _sigmoid(x)


def _iota2(shape, dim):
    return lax.broadcasted_iota(jnp.int32, shape, dim)


def _mm_kernel(x_ref, w_ref, o_ref, wb_ref):
    @pl.when(pl.program_id(1) == 0)
    def _():
        wb_ref[...] = w_ref[0].astype(jnp.bfloat16)

    o_ref[...] = jnp.dot(x_ref[...].astype(jnp.bfloat16), wb_ref[...],
                         preferred_element_type=F32)


def _in_proj(x, w_in, layer, col0, n, tn, tm):
    m, k = x.shape
    return pl.pallas_call(
        _mm_kernel,
        out_shape=jax.ShapeDtypeStruct((m, n), F32),
        grid=(n // tn, m // tm),
        in_specs=[pl.BlockSpec((tm, k), lambda j, i: (i, 0)),
                  pl.BlockSpec((pl.Element(1), pl.Element(k), pl.Element(tn)),
                               lambda j, i: (layer, 0, pl.multiple_of(col0 + j * tn, LANES)))],
        out_specs=pl.BlockSpec((tm, tn), lambda j, i: (i, j)),
        scratch_shapes=[pltpu.VMEM((k, tn), jnp.bfloat16)],
        compiler_params=pltpu.CompilerParams(
            dimension_semantics=("arbitrary", "arbitrary"), vmem_limit_bytes=VMEM_LIMIT),
        name="in_proj",
    )(x, w_in)


def _out_kernel(mix_ref, h_ref, w_ref, lnw_ref, lnb_ref, o_ref, wb_ref):
    @pl.when(pl.program_id(0) == 0)
    def _():
        wb_ref[...] = w_ref[...].astype(jnp.bfloat16)

    slab = OUT_TM // OUT_SLABS
    for s in range(OUT_SLABS):
        rows = slice(s * slab, (s + 1) * slab)
        y = jnp.dot(mix_ref[rows, :].astype(jnp.bfloat16), wb_ref[...],
                    preferred_element_type=F32)
        u = ALPHA * h_ref[rows, :] + y
        mu = jnp.mean(u, axis=-1, keepdims=True)
        d = u - mu
        var = jnp.mean(d * d, axis=-1, keepdims=True)
        o_ref[rows, :] = d * lax.rsqrt(var + LN_EPS) * lnw_ref[...] + lnb_ref[...]


def _out_proj_ln(o_mix, h, w_out, layer, lnw, lnb):
    m = h.shape[0]
    tm = OUT_TM
    return pl.pallas_call(
        _out_kernel,
        out_shape=jax.ShapeDtypeStruct((m, D_MODEL), F32),
        grid=(m // tm,),
        in_specs=[pl.BlockSpec((tm, D_MODEL), lambda i: (i, 0)),
                  pl.BlockSpec((tm, D_MODEL), lambda i: (i, 0)),
                  pl.BlockSpec((None, D_MODEL, D_MODEL), lambda i: (layer, 0, 0),
                               pipeline_mode=pl.Buffered(1)),
                  pl.BlockSpec((1, D_MODEL), lambda i: (0, 0)),
                  pl.BlockSpec((1, D_MODEL), lambda i: (0, 0))],
        out_specs=pl.BlockSpec((tm, D_MODEL), lambda i: (i, 0)),
        scratch_shapes=[pltpu.VMEM((D_MODEL, D_MODEL), jnp.bfloat16)],
        compiler_params=pltpu.CompilerParams(
            dimension_semantics=("arbitrary",), vmem_limit_bytes=VMEM_LIMIT),
        name="out_proj_ln",
    )(o_mix, h, w_out, lnw, lnb)


_INV_BASE = 4
_HGRN_LEVELS = tuple(CHUNK >> (i + 1) for i in range(int(math.log2(CHUNK))))
_N_LEVELS = len(_HGRN_LEVELS)
_PAIR_SL = [slice(p * LANES, (p + 1) * LANES) for p in range(RWKV_PAIRS)]
_HEAD_SL = [slice(h * LANES, (h + 1) * LANES) for h in range(HGRN_HEADS)]

_BUF_SHAPES = (
    ("ar", (RWKV_PAIRS, PAIR_ROWS, LANES), jnp.bfloat16),
    ("bkT", (RWKV_PAIRS, LANES, 2 * PAIR_ROWS), jnp.bfloat16),
    ("vs", (RWKV_PAIRS, PAIR_ROWS, LANES), jnp.bfloat16),
    ("uv", (RWKV_PAIRS, PAIR_ROWS, LANES), jnp.bfloat16),
    ("bhkh", (RWKV_PAIRS, PAIR_ROWS, LANES), jnp.bfloat16),
    ("gall_r", (1, D_RWKV), F32),
    ("bonus", (CHUNK, D_RWKV), F32),
    ("gate_r", (CHUNK, D_RWKV), F32),
    ("vnat", (CHUNK, D_RWKV), F32),
    ("qe", (_N_LEVELS, CHUNK, D_HGRN), jnp.bfloat16),
    ("keT", (_N_LEVELS, HGRN_HEADS // 2, 2 * LANES, PAIR_ROWS), jnp.bfloat16),
    ("qin", (CHUNK, D_HGRN), jnp.bfloat16),
    ("kdec", (CHUNK, D_HGRN), jnp.bfloat16),
    ("ib", (CHUNK, D_HGRN), jnp.bfloat16),
    ("odiag", (CHUNK, D_HGRN), F32),
    ("gate_h", (CHUNK, D_HGRN), F32),
    ("gall_h", (1, D_HGRN), F32),
)


def _head_mask():
    lane = _iota2((PAIR_ROWS, LANES), 1)
    row = _iota2((PAIR_ROWS, LANES), 0)
    return ((lane < RWKV_HEAD) == (row < CHUNK)).astype(F32)


def _rwkv_prepare(first, has_vmix, y_ref, vf_ref, prm, prev_ref, buf):
    (mu_ref, w0_ref, wup_ref, a0_ref, aup_ref, kk_ref, ka_ref, rk_ref, v0_ref, vdn_ref,
     vup_ref) = prm

    def shifted(c0, c1):
        y = y_ref[0, :, c0:c1]
        rolled = pltpu.roll(y, shift=1, axis=0)
        top = rolled[0:8, :]
        prev = jnp.zeros_like(y[0:1, :]) if first else prev_ref[:, c0:c1]
        top = jnp.where(_iota2(top.shape, 0) == 0, prev, top)
        y_prev = jnp.concatenate([top, rolled[8:, :]], axis=0)
        prev_ref[:, c0:c1] = y[CHUNK - 1:CHUNK, :]
        return y + mu_ref[:, c0:c1] * (y_prev - y)

    tail = shifted(4 * D_RWKV, RWKV_COLS)
    wd = tail[:, 0:DECAY_RANK]
    ad = tail[:, DECAY_RANK:]
    w_raw = w0_ref[...] + _bdot(jnp.tanh(wd), wup_ref[...])
    logw = (-math.exp(-0.5) * LOG2E) * _sigmoid(w_raw)
    yield
    a = _sigmoid(a0_ref[...] + _bdot(ad, aup_ref[...]))
    yield
    tri = (_iota2((CHUNK, 3 * CHUNK), 1) % CHUNK <= _iota2((CHUNK, 3 * CHUNK), 0)).astype(F32)
    cl = _split_dot(tri, logw, 3)
    cl_last = cl[CHUNK - 1:CHUNK, :]
    buf["gall_r"][...] = jnp.exp2(cl_last)
    yield
    buf["gate_r"][...] = _silu(shifted(3 * D_RWKV, 4 * D_RWKV))
    yield
    v = shifted(2 * D_RWKV, 3 * D_RWKV)
    if has_vmix:
        gate = _sigmoid(v0_ref[...] + _bdot(_bdot(v, vdn_ref[...]), vup_ref[...]))
        v = v + (vf_ref[0] - v) * gate
    else:
        buf["vnat"][...] = v
    v_b = _bf(v)
    yield
    r = shifted(0, D_RWKV)
    r_t = _bf(r * jnp.exp2(cl))
    yield
    k = shifted(D_RWKV, 2 * D_RWKV)
    kk = k * kk_ref[...]
    k = k * (1.0 + (a - 1.0) * ka_ref[...])
    yield
    kk_sq, rk_sum = _head_sums([kk * kk, r * k * rk_ref[...]])
    kk = kk * lax.rsqrt(jnp.maximum(kk_sq, 1e-24))
    buf["bonus"][...] = rk_sum * v
    yield
    b = kk * a
    a_t = _bf(kk * -jnp.exp2(cl - logw))
    yield
    g_inv = jnp.exp2(-cl)
    b_t = _bf(b * g_inv)
    k_t = _bf(k * g_inv)
    yield
    g_end = jnp.exp2(cl_last - cl)
    b_h = _bf(b * g_end)
    k_h = _bf(k * g_end)
    yield

    mask = _head_mask().astype(jnp.bfloat16)

    def stack(x, p):
        xp = x[:, _PAIR_SL[p]]
        return jnp.concatenate([xp, xp], axis=0) * mask

    for p in range(RWKV_PAIRS):
        sl = _PAIR_SL[p]
        buf["ar"][p] = jnp.concatenate([a_t[:, sl], r_t[:, sl]], axis=0)
        buf["bkT"][p] = jnp.concatenate([stack(b_t, p), stack(k_t, p)], axis=0).T
        buf["bhkh"][p] = jnp.concatenate([b_h[:, sl], k_h[:, sl]], axis=0)
        buf["vs"][p] = stack(v_b, p)
        buf["uv"][p, CHUNK:, :] = v_b[:, sl]
        yield


def _rwkv_chain(half, emit_v, buf, gnw_ref, gnb_ref, o_ref, vf_out_ref, s_ref, st_ref):
    mask_b = _head_mask().astype(jnp.bfloat16)
    t_idx = _iota2((CHUNK, LANES), 0)
    s_idx = _iota2((CHUNK, LANES), 1) % CHUNK
    strict_b = (s_idx < t_idx).astype(jnp.bfloat16)
    incl_b = (s_idx <= t_idx).astype(jnp.bfloat16)
    eye = (s_idx == t_idx).astype(F32)
    prow = _iota2((PAIR_ROWS, LANES), 0)
    pcol = _iota2((PAIR_ROWS, LANES), 1)
    same_head = (prow // RWKV_HEAD) == (pcol // RWKV_HEAD)
    rows = slice(half * CHUNK, (half + 1) * CHUNK)

    def diag(x):
        return jnp.concatenate([x, x], axis=0) * mask_b

    pairs = range(RWKV_PAIRS)
    ar = [buf["ar"][p] for p in pairs]
    sc = [_bdot(ar[p], buf["bkT"][p]) for p in pairs]
    a_kv = [jnp.concatenate([_bf(sc[p][:CHUNK, LANES:]) * strict_b,
                             _bf(sc[p][CHUNK:, LANES:]) * incl_b], axis=0) for p in pairs]
    a_rb = [_bf(sc[p][CHUNK:, :LANES]) * incl_b for p in pairs]
    s_old = [s_ref[p] for p in pairs]

    def part(p, m):
        return jnp.where(m, sc[p][:CHUNK, :LANES], 0.0)

    lower = s_idx < t_idx
    blk = lambda b: (t_idx // b) == (s_idx // b)
    base = lower & blk(_INV_BASE)
    d0 = [part(p, base) for p in pairs]
    d0w = [diag(_bf(x)) for x in d0]
    yield
    sq = [_bdot(d0[p], d0w[p]) for p in pairs]
    yield
    tinv = [eye + d0[p] for p in pairs]
    tinv = [tinv[p] + _bdot(tinv[p], diag(_bf(sq[p]))) for p in pairs]
    yield
    half_a, half_b = pairs[:RWKV_PAIRS // 2], pairs[RWKV_PAIRS // 2:]
    b = _INV_BASE
    while b < CHUNK:
        off = lower & blk(2 * b) & ~blk(b)
        tb = [_bf(x) for x in tinv]
        x1 = [_bdot(tb[p], diag(_bf(part(p, off)))) for p in pairs]
        if b == 2 * _INV_BASE:
            ars = [_bdot(ar[p], st_ref[p]) for p in half_a]
        if b == 4 * _INV_BASE:
            ars += [_bdot(ar[p], st_ref[p]) for p in half_b]
            akv = [_bdot(a_kv[p], buf["vs"][p]) for p in half_a]
        if b == 8 * _INV_BASE:
            akv += [_bdot(a_kv[p], buf["vs"][p]) for p in half_b]
        yield
        tinv = [tinv[p] + _bdot(x1[p], diag(tb[p])) for p in pairs]
        yield
        b *= 2

    u = [_bf(_bdot(tinv[p], diag(_bf(ars[p][:CHUNK] + akv[p][:CHUNK])))) for p in pairs]
    yield
    for p in pairs:
        buf["uv"][p, :CHUNK, :] = u[p]
        upd = _bdot_tn(buf["uv"][p], buf["bhkh"][p])
        s_new = jnp.where(same_head, s_old[p] * buf["gall_r"][:, _PAIR_SL[p]] + upd, 0.0)
        s_ref[p] = s_new
        st_ref[p] = _bf(s_new).T
    o = jnp.concatenate([ars[p][CHUNK:] + _bdot(a_rb[p], diag(u[p])) + akv[p][CHUNK:]
                         for p in pairs], axis=1)
    yield

    d = o - _head_means_split(o)
    yield
    var = _head_sums([d * d], 1.0 / RWKV_HEAD)[0]
    on = d * lax.rsqrt(var + GN_EPS) * gnw_ref[...] + gnb_ref[...]
    o_ref[0, rows, 0:D_RWKV] = (on + buf["bonus"][...]) * buf["gate_r"][...]
    if emit_v:
        vf_out_ref[0, rows, :] = buf["vnat"][...]
    yield


def _hgrn_prepare(layer, main_ref, tail_ref, lbl_ref, gw_ref, buf):
    lg = lbl_ref[...]
    e = jnp.exp(lg - jnp.max(lg, axis=0, keepdims=True))
    sm = e / jnp.sum(e, axis=0, keepdims=True)
    lb = jnp.sum(sm[0:layer + 1, :], axis=0, keepdims=True) - sm[0:1, :]

    def cols(c0, c1):
        lo, hi = RWKV_COLS + c0, RWKV_COLS + c1
        if hi <= IN_MAIN:
            return main_ref[0, :, lo:hi]
        return jnp.concatenate([main_ref[0, :, lo:IN_MAIN], tail_ref[0, :, 0:hi - IN_MAIN]],
                               axis=1)

    sig = _sigmoid(cols(D_HGRN, 2 * D_HGRN))
    log_f = jnp.log2(jnp.maximum(lb, LB_FLOOR) + (1.0 - lb) * sig)
    k = (1.0 - lb) * (1.0 - sig)
    yield

    trow = _iota2((CHUNK, 2 * CHUNK), 0)
    tcol = _iota2((CHUNK, 2 * CHUNK), 1) % CHUNK
    mats = [tcol <= trow]
    for m in _HGRN_LEVELS:
        mid = (trow // (2 * m)) * (2 * m) + m
        after = trow >= mid
        mats.append((after & (tcol >= mid) & (tcol <= trow))
                    | (~after & (tcol > trow) & (tcol < mid)))
    cums = _split_dot(jnp.concatenate(mats, axis=0).astype(F32), log_f, 2)
    b = cums[0:CHUNK, :]
    b_last = b[CHUNK - 1:CHUNK, :]
    buf["gall_h"][...] = jnp.exp2(b_last)
    yield
    q = _silu(cols(0, D_HGRN))
    buf["qin"][...] = _bf(q * jnp.exp2(b))
    q_b = _bf(q)
    yield
    buf["kdec"][...] = _bf(k * jnp.exp2(b_last - b))
    k_b = _bf(k)
    yield
    i_in = cols(2 * D_HGRN, 3 * D_HGRN)
    buf["ib"][...] = _bf(i_in)
    qk = q * k
    for h in range(HGRN_HEADS):
        sl = _HEAD_SL[h]
        buf["odiag"][:, sl] = jnp.sum(qk[:, sl], axis=-1, keepdims=True) * i_in[:, sl]
    yield
    for li in range(_N_LEVELS):
        e = _bf(jnp.exp2(cums[(li + 1) * CHUNK:(li + 2) * CHUNK, :]))
        buf["qe"][li] = q_b * e
        yield
        ke = k_b * e
        for hp in range(HGRN_HEADS // 2):
            buf["keT"][li, hp] = _diag2(ke[:, _HEAD_SL[2 * hp]], ke[:, _HEAD_SL[2 * hp + 1]]).T
        yield
    buf["gate_h"][...] = gw_ref[...] * _silu(cols(3 * D_HGRN, 4 * D_HGRN))
    yield


def _diag2(x0, x1):
    return jnp.concatenate([jnp.concatenate([x0, jnp.zeros_like(x1)], axis=1),
                            jnp.concatenate([jnp.zeros_like(x0), x1], axis=1)], axis=0)


def _hgrn_chain(half, buf, o_ref, s_ref, st_ref):
    trow = _iota2((CHUNK, LANES), 0)
    tcol = _iota2((CHUNK, LANES), 1) % CHUNK
    rows = slice(half * CHUNK, (half + 1) * CHUNK)
    heads = range(HGRN_HEADS)
    pairs = range(HGRN_HEADS // 2)
    pair_sl = [slice(2 * hp * LANES, 2 * (hp + 1) * LANES) for hp in pairs]

    att = [None] * len(pairs)
    yield
    for li, m in enumerate(_HGRN_LEVELS):
        mid = (trow // (2 * m)) * (2 * m) + m
        keep = ((trow >= mid) & (tcol < mid) & (tcol // (2 * m) == trow // (2 * m))).astype(F32)
        qe = buf["qe"][li]
        for hp in pairs:
            part = keep * _bdot(qe[:, pair_sl[hp]], buf["keT"][li, hp])
            att[hp] = part if att[hp] is None else att[hp] + part
        yield
    s_old = [s_ref[h] for h in heads]
    i_b = buf["ib"][...]
    q_in = buf["qin"][...]
    k_dec = buf["kdec"][...]
    o_2 = []
    for hp in pairs:
        h0, h1 = 2 * hp, 2 * hp + 1
        i2 = _diag2(i_b[:, _HEAD_SL[h0]], i_b[:, _HEAD_SL[h1]])
        inter = [_bdot(q_in[:, _HEAD_SL[h]], st_ref[h]) for h in (h0, h1)]
        o_2.append(_bdot(att[hp], i2) + jnp.concatenate(inter, axis=1))
    for h in heads:
        sl = _HEAD_SL[h]
        s_new = s_old[h] * buf["gall_h"][:, sl] + _bdot_tn(i_b[:, sl], k_dec[:, sl])
        s_ref[h] = s_new
        st_ref[h] = _bf(s_new).T
    yield
    for h in heads:
        sl = _HEAD_SL[h]
        o = o_2[h // 2][:, (h % 2) * LANES:(h % 2 + 1) * LANES] + buf["odiag"][:, sl]
        o = o * lax.rsqrt(jnp.mean(o * o, axis=-1, keepdims=True) + RMS_EPS)
        o_ref[0, rows, D_RWKV + h * LANES:D_RWKV + (h + 1) * LANES] = o * buf["gate_h"][:, sl]
    yield


_STAGE_ORDER = (0, 1, 0, 0, 1, 0, 0, 1, 0, 0, 1, 0, 0, 1, 0, 0, 1, 0, 1, 0, 1, 0, 1, 0)
_DONE = object()


def _mixer_kernel(layer, *refs):
    has_vmix = layer > 0
    n_in = 3 * (3 if has_vmix else 2)
    n_prm = 13 if has_vmix else 10
    proj_refs = list(zip(refs[0:3], refs[3:6]))
    vf_refs = refs[6:9] if has_vmix else (None, None, None)
    prm = list(refs[n_in:n_in + n_prm])
    (mu_ref, w0_ref, wup_ref, a0_ref, aup_ref, kk_ref, ka_ref, rk_ref, gnw_ref, gnb_ref) = prm[:10]
    vm = prm[10:13] if has_vmix else [None, None, None]
    prep_prm = (mu_ref, w0_ref, wup_ref, a0_ref, aup_ref, kk_ref, ka_ref, rk_ref, *vm)
    pos = n_in + n_prm
    lbl_ref, gw_ref, o_ref = refs[pos:pos + 3]
    pos += 3
    vf_out_ref = None
    if not has_vmix:
        vf_out_ref = refs[pos]
        pos += 1
    s_r_ref, prev_ref, s_h_ref, st_r_ref, st_h_ref = refs[pos:pos + 5]
    pos += 5
    nb = len(_BUF_SHAPES)
    bufs = [dict(zip([n for n, _, _ in _BUF_SHAPES], refs[pos + i * nb:pos + (i + 1) * nb]))
            for i in range(2)]

    def prepare(first, proj_ref, vf_ref, buf):
        main_ref, tail_ref = proj_ref
        return [_rwkv_prepare(first, has_vmix, main_ref, vf_ref, prep_prm, prev_ref, buf),
                _hgrn_prepare(layer, main_ref, tail_ref, lbl_ref, gw_ref, buf)]

    def chain(half, buf):
        gens = (_rwkv_chain(half, not has_vmix, buf, gnw_ref, gnb_ref, o_ref, vf_out_ref, s_r_ref,
                            st_r_ref),
                _hgrn_chain(half, buf, o_ref, s_h_ref, st_h_ref))
        for g in _STAGE_ORDER:
            next(gens[g])
            yield

    def run(chain_gen, prep_gens):
        live = [chain_gen] + list(prep_gens)
        while live:
            for g in list(live):
                if next(g, _DONE) is _DONE:
                    live.remove(g)

    @pl.when(pl.program_id(1) == 0)
    def _():
        s_r_ref[...] = jnp.zeros_like(s_r_ref)
        s_h_ref[...] = jnp.zeros_like(s_h_ref)
        st_r_ref[...] = jnp.zeros_like(st_r_ref)
        st_h_ref[...] = jnp.zeros_like(st_h_ref)
        run(iter(()), prepare(True, proj_refs[0], vf_refs[0], bufs[0]))

    run(chain(0, bufs[0]), prepare(False, proj_refs[1], vf_refs[1], bufs[1]))
    run(chain(1, bufs[1]), prepare(False, proj_refs[2], vf_refs[2], bufs[0]))


def _mixer_layer(proj, proj_tail, layer, rwkv_params, vmix, v_first, lb_logits, g_norm_w):
    bsz, seq, _ = proj.shape
    has_vmix = layer > 0
    nc = seq // CHUNK
    row_spec = lambda n: pl.BlockSpec((1, n), lambda b, j: (0, 0))
    full_spec = lambda s: pl.BlockSpec(s, lambda b, j: (0, 0))
    chunk_specs = lambda n: [
        pl.BlockSpec((1, CHUNK, n), lambda b, j: (b, 0, 0)),
        pl.BlockSpec((1, CHUNK, n), lambda b, j: (b, 2 * j + 1, 0)),
        pl.BlockSpec((1, CHUNK, n), lambda b, j: (b, jnp.minimum(2 * j + 2, nc - 1), 0))]
    step_spec = lambda n: pl.BlockSpec((1, 2 * CHUNK, n), lambda b, j: (b, j, 0))
    args = [proj] * 3 + [proj_tail] * 3
    in_specs = chunk_specs(IN_MAIN) + chunk_specs(IN_COLS - IN_MAIN)
    if has_vmix:
        args += [v_first] * 3
        in_specs += chunk_specs(D_RWKV)
    args += list(rwkv_params)
    in_specs += [row_spec(RWKV_COLS), row_spec(D_RWKV), full_spec((DECAY_RANK, D_RWKV)),
                 row_spec(D_RWKV), full_spec((A_RANK, D_RWKV)), row_spec(D_RWKV),
                 row_spec(D_RWKV), row_spec(D_RWKV), row_spec(D_RWKV), row_spec(D_RWKV)]
    out_sds = jax.ShapeDtypeStruct((bsz, seq, D_MODEL), F32)
    if has_vmix:
        args += list(vmix)
        in_specs += [row_spec(D_RWKV), full_spec((D_RWKV, VRES_RANK)),
                     full_spec((VRES_RANK, D_RWKV))]
        out_shape, out_specs = out_sds, step_spec(D_MODEL)
    else:
        out_shape = (out_sds, jax.ShapeDtypeStruct((bsz, seq, D_RWKV), F32))
        out_specs = (step_spec(D_MODEL), step_spec(D_RWKV))
    args += [lb_logits, g_norm_w]
    in_specs += [full_spec((DEPTH, D_HGRN)), row_spec(D_HGRN)]
    scratch = [pltpu.VMEM((RWKV_PAIRS, PAIR_ROWS, LANES), F32),
               pltpu.VMEM((1, RWKV_COLS), F32),
               pltpu.VMEM((HGRN_HEADS, HGRN_EXPAND, LANES), F32),
               pltpu.VMEM((RWKV_PAIRS, LANES, PAIR_ROWS), jnp.bfloat16),
               pltpu.VMEM((HGRN_HEADS, LANES, HGRN_EXPAND), jnp.bfloat16)]
    scratch += [pltpu.VMEM(shape, dtype) for _ in range(2) for _, shape, dtype in _BUF_SHAPES]
    res = pl.pallas_call(
        functools.partial(_mixer_kernel, layer),
        out_shape=out_shape,
        grid=(bsz, nc // 2),
        in_specs=in_specs,
        out_specs=out_specs,
        scratch_shapes=scratch,
        compiler_params=pltpu.CompilerParams(
            dimension_semantics=("arbitrary", "arbitrary"), vmem_limit_bytes=VMEM_LIMIT),
        name="mixer",
    )(*args)
    if has_vmix:
        return res, v_first
    return res


def kernel(x, w_in, shift_mu, w_decay0, w_decay_up, a0, a_up, k_k, k_a, r_k, ln_x_w, ln_x_b,
           v_mix0, v_mix_down, v_mix_up, lb_logits, g_norm_w, w_out, ln_w, ln_b):
    out_dtype = x.dtype
    bsz, seq, _ = x.shape
    h = x.astype(F32).reshape(bsz * seq, D_MODEL)
    lb_logits = lb_logits.astype(F32)
    row = lambda t: t.reshape(1, -1)
    v_first = None
    for l in range(DEPTH):
        proj = _in_proj(h, w_in, l, 0, IN_MAIN, IN_TN, IN_TM).reshape(bsz, seq, IN_MAIN)
        tail = _in_proj(h, w_in, l, IN_MAIN, IN_COLS - IN_MAIN, IN_COLS - IN_MAIN, 2 * IN_TM)
        tail = tail.reshape(bsz, seq, IN_COLS - IN_MAIN)
        params = (row(shift_mu[l]), row(w_decay0[l]), w_decay_up[l], row(a0[l]), a_up[l],
                  row(k_k[l]), row(k_a[l]), row(r_k[l]), row(ln_x_w[l]), row(ln_x_b[l]))
        vmix = None if l == 0 else (row(v_mix0[l - 1]), v_mix_down[l - 1], v_mix_up[l - 1])
        o_mix, v_first = _mixer_layer(proj, tail, l, params, vmix, v_first, lb_logits,
                                      row(g_norm_w[l]))
        h = _out_proj_ln(o_mix.reshape(bsz * seq, D_MODEL), h, w_out, l, row(ln_w[l]),
                         row(ln_b[l]))
    return h.reshape(bsz, seq, D_MODEL).astype(out_dtype)
```

```python
import functools
import math

import jax
import jax.numpy as jnp
from jax import lax
from jax.experimental import pallas as pl
from jax.experimental.pallas import tpu as pltpu

D_MODEL = 2048
DEPTH = 2
D_RWKV = D_MODEL // 2
D_HGRN = D_MODEL - D_RWKV
RWKV_HEAD = 64
DECAY_RANK = 64
A_RANK = 64
VRES_RANK = 32
HGRN_EXPAND = 128
HGRN_HEADS = D_HGRN // HGRN_EXPAND
RWKV_COLS = 4 * D_RWKV + DECAY_RANK + A_RANK
HGRN_COLS = 4 * D_HGRN
IN_COLS = RWKV_COLS + HGRN_COLS
ALPHA = (2 * DEPTH) ** 0.25
LN_EPS = 1e-5
GN_EPS = 64e-5
RMS_EPS = 1e-5
LB_FLOOR = 1e-30
LOG2E = math.log2(math.e)

LANES = 128
CHUNK = 64
PAIR_ROWS = 2 * CHUNK
RWKV_PAIRS = D_RWKV // LANES
VMEM_LIMIT = 56 * 1024 * 1024
MXU_WIDTH = 256
IN_TM = 512
IN_TN = 6 * MXU_WIDTH
IN_MAIN = (IN_COLS // IN_TN) * IN_TN
OUT_TM, OUT_SLABS = 512, 2

F32 = jnp.float32


def _bf(x):
    return x.astype(jnp.bfloat16)


def _bdot(a, b):
    return jnp.dot(_bf(a), _bf(b), preferred_element_type=F32)


def _bdot_nt(a, b):
    return lax.dot_general(_bf(a), _bf(b), (((1,), (1,)), ((), ())), preferred_element_type=F32)


def _bdot_tn(a, b):
    return lax.dot_general(_bf(a), _bf(b), (((0,), (0,)), ((), ())), preferred_element_type=F32)


def _split_dot(m01, x, terms):
    pieces = []
    rem = x
    for t in range(terms):
        pieces.append(_bf(rem))
        if t + 1 < terms:
            rem = rem - pieces[-1].astype(F32)
    return jnp.dot(_bf(m01), jnp.concatenate(pieces, axis=0), preferred_element_type=F32)


def _sigmoid(x):
    return 0.5 * jnp.tanh(0.5 * x) + 0.5


def _head_sums(xs, scale=1.0):
    rows, cols = xs[0].shape
    width = len(xs) * LANES
    ones = ((_iota2((width, width), 0) // RWKV_HEAD) == (_iota2((width, width), 1) // RWKV_HEAD))
    ones = jnp.where(ones, scale, 0.0).astype(jnp.bfloat16)
    tiles = range(cols // LANES)
    lhs = jnp.concatenate(
        [jnp.concatenate([x[:, t * LANES:(t + 1) * LANES] for t in tiles], axis=0) for x in xs],
        axis=1)
    sums = jnp.dot(_bf(lhs), ones, preferred_element_type=F32)
    return [jnp.concatenate([sums[t * rows:(t + 1) * rows, i * LANES:(i + 1) * LANES]
                             for t in tiles], axis=1) for i in range(len(xs))]


def _head_means_split(x):
    rows, cols = x.shape
    hi = _bf(x)
    lo = _bf(x - hi.astype(F32))
    same = ((_iota2((2 * LANES, LANES), 0) % LANES) // RWKV_HEAD
            == _iota2((2 * LANES, LANES), 1) // RWKV_HEAD)
    ones = jnp.where(same, 1.0 / RWKV_HEAD, 0.0).astype(jnp.bfloat16)
    tiles = range(cols // LANES)
    lhs = jnp.concatenate(
        [jnp.concatenate([p[:, t * LANES:(t + 1) * LANES] for t in tiles], axis=0)
         for p in (hi, lo)], axis=1)
    sums = jnp.dot(lhs, ones, preferred_element_type=F32)
    return jnp.concatenate([sums[t * rows:(t + 1) * rows, :] for t in tiles], axis=1)


def _silu(x):
    h = 0.5 * x
    return h * jnp.tanh(h) + h


def _iota2(shape, dim):
    return lax.broadcasted_iota(jnp.int32, shape, dim)


def _mm_kernel(x_ref, w_ref, o_ref, wb_ref):
    @pl.when(pl.program_id(1) == 0)
    def _():
        wb_ref[...] = w_ref[0].astype(jnp.bfloat16)

    o_ref[...] = jnp.dot(x_ref[...].astype(jnp.bfloat16), wb_ref[...],
                         preferred_element_type=F32)


def _in_proj(x, w_in, layer, col0, n, tn, tm):
    m, k = x.shape
    return pl.pallas_call(
        _mm_kernel,
        out_shape=jax.ShapeDtypeStruct((m, n), F32),
        grid=(n // tn, m // tm),
        in_specs=[pl.BlockSpec((tm, k), lambda j, i: (i, 0)),
                  pl.BlockSpec((pl.Element(1), pl.Element(k), pl.Element(tn)),
                               lambda j, i: (layer, 0, pl.multiple_of(col0 + j * tn, LANES)))],
        out_specs=pl.BlockSpec((tm, tn), lambda j, i: (i, j)),
        scratch_shapes=[pltpu.VMEM((k, tn), jnp.bfloat16)],
        compiler_params=pltpu.CompilerParams(
            dimension_semantics=("arbitrary", "arbitrary"), vmem_limit_bytes=VMEM_LIMIT),
        name="in_proj",
    )(x, w_in)


def _out_kernel(mix_ref, h_ref, w_ref, lnw_ref, lnb_ref, o_ref, wb_ref):
    @pl.when(pl.program_id(0) == 0)
    def _():
        wb_ref[...] = w_ref[...].astype(jnp.bfloat16)

    slab = OUT_TM // OUT_SLABS
    for s in range(OUT_SLABS):
        rows = slice(s * slab, (s + 1) * slab)
        y = jnp.dot(mix_ref[rows, :].astype(jnp.bfloat16), wb_ref[...],
                    preferred_element_type=F32)
        u = ALPHA * h_ref[rows, :] + y
        mu = jnp.mean(u, axis=-1, keepdims=True)
        d = u - mu
        var = jnp.mean(d * d, axis=-1, keepdims=True)
        o_ref[rows, :] = d * lax.rsqrt(var + LN_EPS) * lnw_ref[...] + lnb_ref[...]


def _out_proj_ln(o_mix, h, w_out, layer, lnw, lnb):
    m = h.shape[0]
    tm = OUT_TM
    return pl.pallas_call(
        _out_kernel,
        out_shape=jax.ShapeDtypeStruct((m, D_MODEL), F32),
        grid=(m // tm,),
        in_specs=[pl.BlockSpec((tm, D_MODEL), lambda i: (i, 0)),
                  pl.BlockSpec((tm, D_MODEL), lambda i: (i, 0)),
                  pl.BlockSpec((None, D_MODEL, D_MODEL), lambda i: (layer, 0, 0),
                               pipeline_mode=pl.Buffered(1)),
                  pl.BlockSpec((1, D_MODEL), lambda i: (0, 0)),
                  pl.BlockSpec((1, D_MODEL), lambda i: (0, 0))],
        out_specs=pl.BlockSpec((tm, D_MODEL), lambda i: (i, 0)),
        scratch_shapes=[pltpu.VMEM((D_MODEL, D_MODEL), jnp.bfloat16)],
        compiler_params=pltpu.CompilerParams(
            dimension_semantics=("arbitrary",), vmem_limit_bytes=VMEM_LIMIT),
        name="out_proj_ln",
    )(o_mix, h, w_out, lnw, lnb)


_INV_BASE = 4
_HGRN_LEVELS = tuple(CHUNK >> (i + 1) for i in range(int(math.log2(CHUNK))))
_N_LEVELS = len(_HGRN_LEVELS)
_PAIR_SL = [slice(p * LANES, (p + 1) * LANES) for p in range(RWKV_PAIRS)]
_HEAD_SL = [slice(h * LANES, (h + 1) * LANES) for h in range(HGRN_HEADS)]

_BUF_SHAPES = (
    ("ar", (RWKV_PAIRS, PAIR_ROWS, LANES), jnp.bfloat16),
    ("bkT", (RWKV_PAIRS, LANES, 2 * PAIR_ROWS), jnp.bfloat16),
    ("vs", (RWKV_PAIRS, PAIR_ROWS, LANES), jnp.bfloat16),
    ("uv", (RWKV_PAIRS, PAIR_ROWS, LANES), jnp.bfloat16),
    ("bhkh", (RWKV_PAIRS, PAIR_ROWS, LANES), jnp.bfloat16),
    ("gall_r", (1, D_RWKV), F32),
    ("bonus", (CHUNK, D_RWKV), F32),
    ("gate_r", (CHUNK, D_RWKV), F32),
    ("vnat", (CHUNK, D_RWKV), F32),
    ("qe", (_N_LEVELS, CHUNK, D_HGRN), jnp.bfloat16),
    ("keT", (_N_LEVELS, HGRN_HEADS // 2, 2 * LANES, PAIR_ROWS), jnp.bfloat16),
    ("qin", (CHUNK, D_HGRN), jnp.bfloat16),
    ("kdec", (CHUNK, D_HGRN), jnp.bfloat16),
    ("ib", (CHUNK, D_HGRN), jnp.bfloat16),
    ("odiag", (CHUNK, D_HGRN), F32),
    ("gate_h", (CHUNK, D_HGRN), F32),
    ("gall_h", (1, D_HGRN), F32),
)


def _head_mask():
    lane = _iota2((PAIR_ROWS, LANES), 1)
    row = _iota2((PAIR_ROWS, LANES), 0)
    return ((lane < RWKV_HEAD) == (row < CHUNK)).astype(F32)


def _rwkv_prepare(first, has_vmix, y_ref, vf_ref, prm, prev_ref, buf):
    (mu_ref, w0_ref, wup_ref, a0_ref, aup_ref, kk_ref, ka_ref, rk_ref, v0_ref, vdn_ref,
     vup_ref) = prm

    def shifted(c0, c1):
        y = y_ref[0, :, c0:c1]
        rolled = pltpu.roll(y, shift=1, axis=0)
        top = rolled[0:8, :]
        prev = jnp.zeros_like(y[0:1, :]) if first else prev_ref[:, c0:c1]
        top = jnp.where(_iota2(top.shape, 0) == 0, prev, top)
        y_prev = jnp.concatenate([top, rolled[8:, :]], axis=0)
        prev_ref[:, c0:c1] = y[CHUNK - 1:CHUNK, :]
        return y + mu_ref[:, c0:c1] * (y_prev - y)

    tail = shifted(4 * D_RWKV, RWKV_COLS)
    wd = tail[:, 0:DECAY_RANK]
    ad = tail[:, DECAY_RANK:]
    w_raw = w0_ref[...] + _bdot(jnp.tanh(wd), wup_ref[...])
    half_c = -0.5 * math.exp(-0.5) * LOG2E
    logw = half_c * jnp.tanh(0.5 * w_raw) + half_c
    yield
    a = _sigmoid(a0_ref[...] + _bdot(ad, aup_ref[...]))
    yield
    tri = (_iota2((CHUNK, 3 * CHUNK), 1) % CHUNK <= _iota2((CHUNK, 3 * CHUNK), 0)).astype(F32)
    cl = _split_dot(tri, logw, 3)
    cl_last = cl[CHUNK - 1:CHUNK, :]
    buf["gall_r"][...] = jnp.exp2(cl_last)
    yield
    buf["gate_r"][...] = _silu(shifted(3 * D_RWKV, 4 * D_RWKV))
    yield
    v = shifted(2 * D_RWKV, 3 * D_RWKV)
    if has_vmix:
        gate = _sigmoid(v0_ref[...] + _bdot(_bdot(v, vdn_ref[...]), vup_ref[...]))
        v = v + (vf_ref[0] - v) * gate
    else:
        buf["vnat"][...] = v
    v_b = _bf(v)
    yield
    r = shifted(0, D_RWKV)
    r_t = _bf(r * jnp.exp2(cl))
    yield
    k = shifted(D_RWKV, 2 * D_RWKV)
    kk = k * kk_ref[...]
    k = k * (1.0 + (a - 1.0) * ka_ref[...])
    yield
    kk_sq, rk_sum = _head_sums([kk * kk, r * k * rk_ref[...]])
    kk = kk * lax.rsqrt(jnp.maximum(kk_sq, 1e-24))
    buf["bonus"][...] = rk_sum * v
    yield
    b = kk * a
    a_t = _bf(kk * -jnp.exp2(cl - logw))
    yield
    g_inv = jnp.exp2(-cl)
    b_t = _bf(b * g_inv)
    k_t = _bf(k * g_inv)
    yield
    g_end = jnp.exp2(cl_last - cl)
    b_h = _bf(b * g_end)
    k_h = _bf(k * g_end)
    yield

    mask = _head_mask().astype(jnp.bfloat16)

    def stack(x, p):
        xp = x[:, _PAIR_SL[p]]
        return jnp.concatenate([xp, xp], axis=0) * mask

    for p in range(RWKV_PAIRS):
        sl = _PAIR_SL[p]
        buf["ar"][p] = jnp.concatenate([a_t[:, sl], r_t[:, sl]], axis=0)
        buf["bkT"][p] = jnp.concatenate([stack(b_t, p), stack(k_t, p)], axis=0).T
        buf["bhkh"][p] = jnp.concatenate([b_h[:, sl], k_h[:, sl]], axis=0)
        buf["vs"][p] = stack(v_b, p)
        buf["uv"][p, CHUNK:, :] = v_b[:, sl]
        yield


def _rwkv_chain(half, emit_v, buf, gnw_ref, gnb_ref, o_ref, vf_out_ref, s_ref, st_ref):
    mask_b = _head_mask().astype(jnp.bfloat16)
    t_idx = _iota2((CHUNK, LANES), 0)
    s_idx = _iota2((CHUNK, LANES), 1) % CHUNK
    strict_b = (s_idx < t_idx).astype(jnp.bfloat16)
    incl_b = (s_idx <= t_idx).astype(jnp.bfloat16)
    eye = (s_idx == t_idx).astype(F32)
    prow = _iota2((PAIR_ROWS, LANES), 0)
    pcol = _iota2((PAIR_ROWS, LANES), 1)
    same_head = (prow // RWKV_HEAD) == (pcol // RWKV_HEAD)
    rows = slice(half * CHUNK, (half + 1) * CHUNK)

    def diag(x):
        return jnp.concatenate([x, x], axis=0) * mask_b

    pairs = range(RWKV_PAIRS)
    ar = [buf["ar"][p] for p in pairs]
    sc = [_bdot(ar[p], buf["bkT"][p]) for p in pairs]
    a_kv = [jnp.concatenate([_bf(sc[p][:CHUNK, LANES:]) * strict_b,
                             _bf(sc[p][CHUNK:, LANES:]) * incl_b], axis=0) for p in pairs]
    a_rb = [_bf(sc[p][CHUNK:, :LANES]) * incl_b for p in pairs]
    s_old = [s_ref[p] for p in pairs]

    def part(p, m):
        return jnp.where(m, sc[p][:CHUNK, :LANES], 0.0)

    lower = s_idx < t_idx
    blk = lambda b: (t_idx // b) == (s_idx // b)
    base = lower & blk(_INV_BASE)
    d0 = [part(p, base) for p in pairs]
    d0w = [diag(_bf(x)) for x in d0]
    yield
    sq = [_bdot(d0[p], d0w[p]) for p in pairs]
    yield
    tinv = [eye + d0[p] for p in pairs]
    tinv = [tinv[p] + _bdot(tinv[p], diag(_bf(sq[p]))) for p in pairs]
    yield
    half_a, half_b = pairs[:RWKV_PAIRS // 2], pairs[RWKV_PAIRS // 2:]
    b = _INV_BASE
    while b < CHUNK:
        off = lower & blk(2 * b) & ~blk(b)
        tb = [_bf(x) for x in tinv]
        x1 = [_bdot(tb[p], diag(_bf(part(p, off)))) for p in pairs]
        if b == 2 * _INV_BASE:
            ars = [_bdot(ar[p], st_ref[p]) for p in half_a]
        if b == 4 * _INV_BASE:
            ars += [_bdot(ar[p], st_ref[p]) for p in half_b]
            akv = [_bdot(a_kv[p], buf["vs"][p]) for p in half_a]
        if b == 8 * _INV_BASE:
            akv += [_bdot(a_kv[p], buf["vs"][p]) for p in half_b]
        yield
        tinv = [tinv[p] + _bdot(x1[p], diag(tb[p])) for p in pairs]
        yield
        b *= 2

    u = [_bf(_bdot(tinv[p], diag(_bf(ars[p][:CHUNK] + akv[p][:CHUNK])))) for p in pairs]
    yield
    for p in pairs:
        buf["uv"][p, :CHUNK, :] = u[p]
        upd = _bdot_tn(buf["uv"][p], buf["bhkh"][p])
        s_new = jnp.where(same_head, s_old[p] * buf["gall_r"][:, _PAIR_SL[p]] + upd, 0.0)
        s_ref[p] = s_new
        st_ref[p] = _bf(s_new).T
    o = jnp.concatenate([ars[p][CHUNK:] + _bdot(a_rb[p], diag(u[p])) + akv[p][CHUNK:]
                         for p in pairs], axis=1)
    yield

    d = o - _head_means_split(o)
    yield
    var = _head_sums([d * d], 1.0 / RWKV_HEAD)[0]
    on = d * lax.rsqrt(var + GN_EPS) * gnw_ref[...] + gnb_ref[...]
    o_ref[0, rows, 0:D_RWKV] = (on + buf["bonus"][...]) * buf["gate_r"][...]
    if emit_v:
        vf_out_ref[0, rows, :] = buf["vnat"][...]
    yield


def _hgrn_prepare(layer, main_ref, tail_ref, lbl_ref, gw_ref, buf):
    lg = lbl_ref[...]
    e = jnp.exp(lg - jnp.max(lg, axis=0, keepdims=True))
    sm = e / jnp.sum(e, axis=0, keepdims=True)
    lb = jnp.sum(sm[0:layer + 1, :], axis=0, keepdims=True) - sm[0:1, :]

    def cols(c0, c1):
        lo, hi = RWKV_COLS + c0, RWKV_COLS + c1
        if hi <= IN_MAIN:
            return main_ref[0, :, lo:hi]
        return jnp.concatenate([main_ref[0, :, lo:IN_MAIN], tail_ref[0, :, 0:hi - IN_MAIN]],
                               axis=1)

    sig = _sigmoid(cols(D_HGRN, 2 * D_HGRN))
    log_f = jnp.log2(jnp.maximum(lb, LB_FLOOR) + (1.0 - lb) * sig)
    k = (1.0 - lb) * (1.0 - sig)
    yield

    trow = _iota2((CHUNK, 2 * CHUNK), 0)
    tcol = _iota2((CHUNK, 2 * CHUNK), 1) % CHUNK
    mats = [tcol <= trow]
    for m in _HGRN_LEVELS:
        mid = (trow // (2 * m)) * (2 * m) + m
        after = trow >= mid
        mats.append((after & (tcol >= mid) & (tcol <= trow))
                    | (~after & (tcol > trow) & (tcol < mid)))
    cums = _split_dot(jnp.concatenate(mats, axis=0).astype(F32), log_f, 2)
    b = cums[0:CHUNK, :]
    b_last = b[CHUNK - 1:CHUNK, :]
    buf["gall_h"][...] = jnp.exp2(b_last)
    yield
    q = _silu(cols(0, D_HGRN))
    buf["qin"][...] = _bf(q * jnp.exp2(b))
    q_b = _bf(q)
    yield
    buf["kdec"][...] = _bf(k * jnp.exp2(b_last - b))
    k_b = _bf(k)
    yield
    i_in = cols(2 * D_HGRN, 3 * D_HGRN)
    buf["ib"][...] = _bf(i_in)
    qk = q * k
    for h in range(HGRN_HEADS):
        sl = _HEAD_SL[h]
        buf["odiag"][:, sl] = jnp.sum(qk[:, sl], axis=-1, keepdims=True) * i_in[:, sl]
    yield
    for li in range(_N_LEVELS):
        e = _bf(jnp.exp2(cums[(li + 1) * CHUNK:(li + 2) * CHUNK, :]))
        buf["qe"][li] = q_b * e
        yield
        ke = k_b * e
        for hp in range(HGRN_HEADS // 2):
            buf["keT"][li, hp] = _diag2(ke[:, _HEAD_SL[2 * hp]], ke[:, _HEAD_SL[2 * hp + 1]]).T
        yield
    buf["gate_h"][...] = gw_ref[...] * _silu(cols(3 * D_HGRN, 4 * D_HGRN))
    yield


def _diag2(x0, x1):
    return jnp.concatenate([jnp.concatenate([x0, jnp.zeros_like(x1)], axis=1),
                            jnp.concatenate([jnp.zeros_like(x0), x1], axis=1)], axis=0)


def _hgrn_chain(half, buf, o_ref, s_ref, st_ref):
    trow = _iota2((CHUNK, LANES), 0)
    tcol = _iota2((CHUNK, LANES), 1) % CHUNK
    rows = slice(half * CHUNK, (half + 1) * CHUNK)
    heads = range(HGRN_HEADS)
    pairs = range(HGRN_HEADS // 2)
    pair_sl = [slice(2 * hp * LANES, 2 * (hp + 1) * LANES) for hp in pairs]

    att = [None] * len(pairs)
    yield
    for li, m in enumerate(_HGRN_LEVELS):
        mid = (trow // (2 * m)) * (2 * m) + m
        keep = ((trow >= mid) & (tcol < mid) & (tcol // (2 * m) == trow // (2 * m))).astype(F32)
        qe = buf["qe"][li]
        for hp in pairs:
            part = keep * _bdot(qe[:, pair_sl[hp]], buf["keT"][li, hp])
            att[hp] = part if att[hp] is None else att[hp] + part
        yield
    s_old = [s_ref[h] for h in heads]
    i_b = buf["ib"][...]
    q_in = buf["qin"][...]
    k_dec = buf["kdec"][...]
    o_2 = []
    for hp in pairs:
        h0, h1 = 2 * hp, 2 * hp + 1
        i2 = _diag2(i_b[:, _HEAD_SL[h0]], i_b[:, _HEAD_SL[h1]])
        inter = [_bdot(q_in[:, _HEAD_SL[h]], st_ref[h]) for h in (h0, h1)]
        o_2.append(_bdot(att[hp], i2) + jnp.concatenate(inter, axis=1))
    for h in heads:
        sl = _HEAD_SL[h]
        s_new = s_old[h] * buf["gall_h"][:, sl] + _bdot_tn(i_b[:, sl], k_dec[:, sl])
        s_ref[h] = s_new
        st_ref[h] = _bf(s_new).T
    yield
    for h in heads:
        sl = _HEAD_SL[h]
        o = o_2[h // 2][:, (h % 2) * LANES:(h % 2 + 1) * LANES] + buf["odiag"][:, sl]
        o = o * lax.rsqrt(jnp.mean(o * o, axis=-1, keepdims=True) + RMS_EPS)
        o_ref[0, rows, D_RWKV + h * LANES:D_RWKV + (h + 1) * LANES] = o * buf["gate_h"][:, sl]
    yield


_STAGE_ORDER = (0, 1, 0, 0, 1, 0, 0, 1, 0, 0, 1, 0, 0, 1, 0, 0, 1, 0, 1, 0, 1, 0, 1, 0)
_DONE = object()


def _mixer_kernel(layer, *refs):
    has_vmix = layer > 0
    n_in = 3 * (3 if has_vmix else 2)
    n_prm = 13 if has_vmix else 10
    proj_refs = list(zip(refs[0:3], refs[3:6]))
    vf_refs = refs[6:9] if has_vmix else (None, None, None)
    prm = list(refs[n_in:n_in + n_prm])
    (mu_ref, w0_ref, wup_ref, a0_ref, aup_ref, kk_ref, ka_ref, rk_ref, gnw_ref, gnb_ref) = prm[:10]
    vm = prm[10:13] if has_vmix else [None, None, None]
    prep_prm = (mu_ref, w0_ref, wup_ref, a0_ref, aup_ref, kk_ref, ka_ref, rk_ref, *vm)
    pos = n_in + n_prm
    lbl_ref, gw_ref, o_ref = refs[pos:pos + 3]
    pos += 3
    vf_out_ref = None
    if not has_vmix:
        vf_out_ref = refs[pos]
        pos += 1
    s_r_ref, prev_ref, s_h_ref, st_r_ref, st_h_ref = refs[pos:pos + 5]
    pos += 5
    nb = len(_BUF_SHAPES)
    bufs = [dict(zip([n for n, _, _ in _BUF_SHAPES], refs[pos + i * nb:pos + (i + 1) * nb]))
            for i in range(2)]

    def prepare(first, proj_ref, vf_ref, buf):
        main_ref, tail_ref = proj_ref
        return [_rwkv_prepare(first, has_vmix, main_ref, vf_ref, prep_prm, prev_ref, buf),
                _hgrn_prepare(layer, main_ref, tail_ref, lbl_ref, gw_ref, buf)]

    def chain(half, buf):
        gens = (_rwkv_chain(half, not has_vmix, buf, gnw_ref, gnb_ref, o_ref, vf_out_ref, s_r_ref,
                            st_r_ref),
                _hgrn_chain(half, buf, o_ref, s_h_ref, st_h_ref))
        for g in _STAGE_ORDER:
            next(gens[g])
            yield

    def run(chain_gen, prep_gens):
        live = [chain_gen] + list(prep_gens)
        while live:
            for g in list(live):
                if next(g, _DONE) is _DONE:
                    live.remove(g)

    @pl.when(pl.program_id(1) == 0)
    def _():
        s_r_ref[...] = jnp.zeros_like(s_r_ref)
        s_h_ref[...] = jnp.zeros_like(s_h_ref)
        st_r_ref[...] = jnp.zeros_like(st_r_ref)
        st_h_ref[...] = jnp.zeros_like(st_h_ref)
        run(iter(()), prepare(True, proj_refs[0], vf_refs[0], bufs[0]))

    run(chain(0, bufs[0]), prepare(False, proj_refs[1], vf_refs[1], bufs[1]))
    run(chain(1, bufs[1]), prepare(False, proj_refs[2], vf_refs[2], bufs[0]))


def _mixer_layer(proj, proj_tail, layer, rwkv_params, vmix, v_first, lb_logits, g_norm_w):
    bsz, seq, _ = proj.shape
    has_vmix = layer > 0
    nc = seq // CHUNK
    row_spec = lambda n: pl.BlockSpec((1, n), lambda b, j: (0, 0))
    full_spec = lambda s: pl.BlockSpec(s, lambda b, j: (0, 0))
    chunk_specs = lambda n: [
        pl.BlockSpec((1, CHUNK, n), lambda b, j: (b, 0, 0)),
        pl.BlockSpec((1, CHUNK, n), lambda b, j: (b, 2 * j + 1, 0)),
        pl.BlockSpec((1, CHUNK, n), lambda b, j: (b, jnp.minimum(2 * j + 2, nc - 1), 0))]
    step_spec = lambda n: pl.BlockSpec((1, 2 * CHUNK, n), lambda b, j: (b, j, 0))
    args = [proj] * 3 + [proj_tail] * 3
    in_specs = chunk_specs(IN_MAIN) + chunk_specs(IN_COLS - IN_MAIN)
    if has_vmix:
        args += [v_first] * 3
        in_specs += chunk_specs(D_RWKV)
    args += list(rwkv_params)
    in_specs += [row_spec(RWKV_COLS), row_spec(D_RWKV), full_spec((DECAY_RANK, D_RWKV)),
                 row_spec(D_RWKV), full_spec((A_RANK, D_RWKV)), row_spec(D_RWKV),
                 row_spec(D_RWKV), row_spec(D_RWKV), row_spec(D_RWKV), row_spec(D_RWKV)]
    out_sds = jax.ShapeDtypeStruct((bsz, seq, D_MODEL), F32)
    if has_vmix:
        args += list(vmix)
        in_specs += [row_spec(D_RWKV), full_spec((D_RWKV, VRES_RANK)),
                     full_spec((VRES_RANK, D_RWKV))]
        out_shape, out_specs = out_sds, step_spec(D_MODEL)
    else:
        out_shape = (out_sds, jax.ShapeDtypeStruct((bsz, seq, D_RWKV), F32))
        out_specs = (step_spec(D_MODEL), step_spec(D_RWKV))
    args += [lb_logits, g_norm_w]
    in_specs += [full_spec((DEPTH, D_HGRN)), row_spec(D_HGRN)]
    scratch = [pltpu.VMEM((RWKV_PAIRS, PAIR_ROWS, LANES), F32),
               pltpu.VMEM((1, RWKV_COLS), F32),
               pltpu.VMEM((HGRN_HEADS, HGRN_EXPAND, LANES), F32),
               pltpu.VMEM((RWKV_PAIRS, LANES, PAIR_ROWS), jnp.bfloat16),
               pltpu.VMEM((HGRN_HEADS, LANES, HGRN_EXPAND), jnp.bfloat16)]
    scratch += [pltpu.VMEM(shape, dtype) for _ in range(2) for _, shape, dtype in _BUF_SHAPES]
    res = pl.pallas_call(
        functools.partial(_mixer_kernel, layer),
        out_shape=out_shape,
        grid=(bsz, nc // 2),
        in_specs=in_specs,
        out_specs=out_specs,
        scratch_shapes=scratch,
        compiler_params=pltpu.CompilerParams(
            dimension_semantics=("arbitrary", "arbitrary"), vmem_limit_bytes=VMEM_LIMIT),
        name="mixer",
    )(*args)
    if has_vmix:
        return res, v_first
    return res


def kernel(x, w_in, shift_mu, w_decay0, w_decay_up, a0, a_up, k_k, k_a, r_k, ln_x_w, ln_x_b,
           v_mix0, v_mix_down, v_mix_up, lb_logits, g_norm_w, w_out, ln_w, ln_b):
    out_dtype = x.dtype
    bsz, seq, _ = x.shape
    h = x.astype(F32).reshape(bsz * seq, D_MODEL)
    lb_logits = lb_logits.astype(F32)
    row = lambda t: t.reshape(1, -1)
    v_first = None
    for l in range(DEPTH):
        proj = _in_proj(h, w_in, l, 0, IN_MAIN, IN_TN, IN_TM).reshape(bsz, seq, IN_MAIN)
        tail = _in_proj(h, w_in, l, IN_MAIN, IN_COLS - IN_MAIN, IN_COLS - IN_MAIN, 2 * IN_TM)
        tail = tail.reshape(bsz, seq, IN_COLS - IN_MAIN)
        params = (row(shift_mu[l]), row(w_decay0[l]), w_decay_up[l], row(a0[l]), a_up[l],
                  row(k_k[l]), row(k_a[l]), row(r_k[l]), row(ln_x_w[l]), row(ln_x_b[l]))
        vmix = None if l == 0 else (row(v_mix0[l - 1]), v_mix_down[l - 1], v_mix_up[l - 1])
        o_mix, v_first = _mixer_layer(proj, tail, l, params, vmix, v_first, lb_logits,
                                      row(g_norm_w[l]))
        h = _out_proj_ln(o_mix.reshape(bsz * seq, D_MODEL), h, w_out, l, row(ln_w[l]),
                         row(ln_b[l]))
    return h.reshape(bsz, seq, D_MODEL).astype(out_dtype)
```
